```python
import jax, jax.numpy as jnp
from jax import lax
import numpy as np

D_MODEL = 1024
BATCH = 8
SEQ = 2048
DEPTH = 2

N_MIXERS = 2
N_RET_LAYERS = (DEPTH + 1) // 2
N_DSA_LAYERS = DEPTH // 2
RET_HEADS = 4
RET_DK = D_MODEL // RET_HEADS
RET_DV = 2 * RET_DK
RET_CHUNK = 128
RET_IN = 2 * RET_HEADS * RET_DK + 2 * RET_HEADS * RET_DV
ATT_HEADS = 8
ATT_DH = D_MODEL // ATT_HEADS
ATT_KV_HEADS = 2
IDX_HEADS = 8
IDX_DH = 64
TOPK_MAX = 256
Q_BLOCK = 64
DSA_IN = (ATT_HEADS * ATT_DH + 2 * ATT_KV_HEADS * ATT_DH
          + IDX_HEADS * IDX_DH + IDX_DH + IDX_HEADS)
D_FF = 4 * D_MODEL
ROPE_THETA = 10000.0
EPS = 1e-6

kernel_name = "hybrid_retention_dsa_trunk"


def rmsnorm(x, g):
    x32 = x.astype(jnp.float32)
    y = x32 * lax.rsqrt(jnp.mean(x32 * x32, axis=-1, keepdims=True) + EPS)
    return (y * g.astype(jnp.float32)).astype(x.dtype)


def rope(x, pos):
    half = x.shape[-1] // 2
    inv = ROPE_THETA ** (-jnp.arange(half, dtype=jnp.float32) / half)
    ang = pos.astype(jnp.float32)[..., None] * inv
    cos = jnp.cos(ang)[:, :, None, :]
    sin = jnp.sin(ang)[:, :, None, :]
    x32 = x.astype(jnp.float32)
    x1, x2 = x32[..., :half], x32[..., half:]
    out = jnp.concatenate([x1 * cos - x2 * sin, x2 * cos + x1 * sin], axis=-1)
    return out.astype(x.dtype)


def retention(h, w_in, out_gain, w_out, pos):
    B, S, _ = h.shape
    H, DK, DV, C = RET_HEADS, RET_DK, RET_DV, RET_CHUNK
    proj = h @ w_in
    q, k, v, g = jnp.split(proj, [H * DK, 2 * H * DK, 2 * H * DK + H * DV], axis=-1)
    q = rope(q.reshape(B, S, H, DK), pos)
    k = rope(k.reshape(B, S, H, DK), pos) * (DK ** -0.5)
    v = v.reshape(B, S, H, DV)
    N = S // C

    def to_chunks(t):
        return t.reshape(B, N, C, H, -1).transpose(1, 0, 3, 2, 4).astype(jnp.float32)

    qc, kc, vc = to_chunks(q), to_chunks(k), to_chunks(v)
    log_gamma = jnp.log1p(-(2.0 ** (-5.0 - jnp.arange(H, dtype=jnp.float32))))
    i = jnp.arange(C, dtype=jnp.float32)
    diff = i[:, None] - i[None, :]
    decay_mask = jnp.where(diff >= 0,
                           jnp.exp(log_gamma[:, None, None] * jnp.maximum(diff, 0.0)), 0.0)
    q_decay = jnp.exp(log_gamma[:, None] * (i + 1.0))
    k_decay = jnp.exp(log_gamma[:, None] * (C - 1.0 - i))
    chunk_decay = jnp.exp(log_gamma * C)

    def step(state, qkv):
        qb, kb, vb = qkv
        scores = jnp.einsum('bhid,bhjd->bhij', qb, kb) * decay_mask
        o = (jnp.einsum('bhij,bhjv->bhiv', scores, vb)
             + jnp.einsum('bhid,bhdv->bhiv', qb * q_decay[..., None], state))
        state = (state * chunk_decay[:, None, None]
                 + jnp.einsum('bhjd,bhjv->bhdv', kb * k_decay[..., None], vb))
        return state, o

    state0 = jnp.zeros((B, H, DK, DV), jnp.float32)
    _, o = lax.scan(step, state0, (qc, kc, vc))
    o = o.transpose(1, 0, 3, 2, 4).reshape(B, S, H, DV)
    o = rmsnorm(o, out_gain).reshape(B, S, H * DV).astype(h.dtype)
    return (o * jax.nn.silu(g)) @ w_out


def sparse_attention(h, w_in, q_gain, k_gain, kidx_gain, w_out, pos):
    B, S, _ = h.shape
    H, G, DH, HI, DI, QB = ATT_HEADS, ATT_KV_HEADS, ATT_DH, IDX_HEADS, IDX_DH, Q_BLOCK
    splits = np.cumsum([H * DH, G * DH, G * DH, HI * DI, DI]).tolist()
    q, k, v, qi, ki, wi = jnp.split(h @ w_in, splits, axis=-1)
    q = rope(rmsnorm(q.reshape(B, S, H, DH), q_gain), pos)
    k = rope(rmsnorm(k.reshape(B, S, G, DH), k_gain), pos)
    v = v.reshape(B, S, G, DH)
    qi = rope(qi.reshape(B, S, HI, DI), pos)
    ki = rope(rmsnorm(ki, kidx_gain)[:, :, None, :], pos)[:, :, 0, :]
    wi = wi * (HI ** -0.5 * DI ** -0.5)
    topk = min(TOPK_MAX, S // 4)
    nb = S // QB
    ki32 = ki.astype(jnp.float32)
    s_pos = jnp.arange(S)
    neg = jnp.finfo(jnp.float32).min

    def blocks(t):
        return t.reshape((B, nb, QB) + t.shape[2:]).swapaxes(0, 1)

    def block(args):
        blk, qb, qib, wb = args
        t = blk * QB + jnp.arange(QB)
        rel = jax.nn.relu(jnp.einsum('bqhd,bsd->bqhs', qib.astype(jnp.float32), ki32))
        score = jnp.einsum('bqhs,bqh->bqs', rel, wb.astype(jnp.float32))
        score = jnp.where((s_pos[None, :] <= t[:, None])[None], score, neg)
        _, idx = lax.top_k(score, topk)
        valid = idx <= t[None, :, None]
        ks = jax.vmap(lambda kk, ii: kk[ii])(k, idx)
        vs = jax.vmap(lambda vv, ii: vv[ii])(v, idx)
        qg = qb.reshape(B, QB, G, H // G, DH)
        logits = jnp.einsum('bqgrd,bqkgd->bqgrk', qg, ks).astype(jnp.float32) * (DH ** -0.5)
        logits = jnp.where(valid[:, :, None, None, :], logits, neg)
        p = jax.nn.softmax(logits, axis=-1).astype(vs.dtype)
        return jnp.einsum('bqgrk,bqkgd->bqgrd', p, vs).reshape(B, QB, H * DH)

    out = lax.map(block, (jnp.arange(nb), blocks(q), blocks(qi), blocks(wi)))
    return out.swapaxes(0, 1).reshape(B, S, H * DH) @ w_out


def sq_relu_mlp(h, w_up, w_down):
    return jnp.square(jax.nn.relu(h @ w_up)) @ w_down


def setup_inputs(seed: int = 0) -> dict:
    key = jax.random.key(seed)
    ks = jax.random.split(key, 16)
    nrm = lambda k, shape, scale: jax.random.normal(k, shape, jnp.float32) * scale
    gain = lambda k, shape: 1.0 + 0.02 * jax.random.normal(k, shape, jnp.float32)
    x = jax.random.normal(ks[0], (BATCH, SEQ, D_MODEL), jnp.float32)
    positions = jnp.broadcast_to(jnp.arange(SEQ, dtype=jnp.int32), (BATCH, SEQ))
    return {
        "x": x,
        "positions": positions,
        "attn_norm": gain(ks[1], (DEPTH, D_MODEL)),
        "ret_w_in": nrm(ks[2], (N_RET_LAYERS, D_MODEL, RET_IN), D_MODEL ** -0.5),
        "ret_out_norm": gain(ks[3], (N_RET_LAYERS, RET_HEADS, RET_DV)),
        "ret_w_out": nrm(ks[4], (N_RET_LAYERS, RET_HEADS * RET_DV, D_MODEL), (RET_HEADS * RET_DV) ** -0.5),
        "dsa_w_in": nrm(ks[5], (N_DSA_LAYERS, D_MODEL, DSA_IN), D_MODEL ** -0.5),
        "dsa_q_norm": gain(ks[6], (N_DSA_LAYERS, ATT_DH)),
        "dsa_k_norm": gain(ks[7], (N_DSA_LAYERS, ATT_DH)),
        "dsa_kidx_norm": gain(ks[8], (N_DSA_LAYERS, IDX_DH)),
        "dsa_w_out": nrm(ks[9], (N_DSA_LAYERS, ATT_HEADS * ATT_DH, D_MODEL), (ATT_HEADS * ATT_DH) ** -0.5),
        "mlp_norm": gain(ks[10], (DEPTH, D_MODEL)),
        "mlp_w_up": nrm(ks[11], (DEPTH, D_MODEL, D_FF), D_MODEL ** -0.5),
        "mlp_w_down": nrm(ks[12], (DEPTH, D_FF, D_MODEL), D_FF ** -0.5),
    }


def reference(x, positions, attn_norm, ret_w_in, ret_out_norm, ret_w_out,
              dsa_w_in, dsa_q_norm, dsa_k_norm, dsa_kidx_norm, dsa_w_out,
              mlp_norm, mlp_w_up, mlp_w_down):
    for i in range(DEPTH):
        h = rmsnorm(x, attn_norm[i])
        j = i // N_MIXERS
        if i % N_MIXERS == 0:
            x = x + retention(h, ret_w_in[j], ret_out_norm[j], ret_w_out[j], positions)
        else:
            x = x + sparse_attention(h, dsa_w_in[j], dsa_q_norm[j], dsa_k_norm[j],
                                     dsa_kidx_norm[j], dsa_w_out[j], positions)
        h = rmsnorm(x, mlp_norm[i])
        x = x + sq_relu_mlp(h, mlp_w_up[i], mlp_w_down[i])
    return x
```

```python
import functools

import jax
import jax.numpy as jnp
from jax import lax
from jax.experimental import pallas as pl
from jax.experimental.pallas import tpu as pltpu

F32 = jnp.float32
BF16 = jnp.bfloat16

EPS = 1e-6
ROPE_THETA = 10000.0
RET_HEADS = 4
RET_CHUNK = 128
ATT_HEADS = 8
ATT_KV_HEADS = 2
IDX_HEADS = 8
IDX_DH = 64
TOPK_MAX = 256
LANES = 128

ROW_TILE = 512
Q_TILE = 256
VMEM_LIMIT = 56 * 1024 * 1024
BISECT_STEPS = 24


def _params(*sem):
    return pltpu.CompilerParams(dimension_semantics=sem, vmem_limit_bytes=VMEM_LIMIT)


def _resident(shape):
    nd = len(shape)
    return pl.BlockSpec(shape, lambda *_: (0,) * nd, pipeline_mode=pl.Buffered(1))


def _rms(x, gain):
    return x * lax.rsqrt(jnp.mean(x * x, axis=-1, keepdims=True) + EPS) * gain


def _dot(a, b):
    return jnp.dot(a, b, preferred_element_type=F32)


def _dot_nt(a, b):
    return lax.dot_general(a, b, (((1,), (1,)), ((), ())), preferred_element_type=F32)


def _dot_tn(a, b):
    return lax.dot_general(a, b, (((0,), (0,)), ((), ())), preferred_element_type=F32)


def _rope_table_kernel(pos_ref, inv_ref, sign_ref, c128_ref, s128_ref, c64_ref, s64_ref,
                       c32_ref, s32_ref):
    pos = pos_ref[...].astype(F32)
    for row, (c_ref, s_ref) in enumerate(((c128_ref, s128_ref), (c64_ref, s64_ref),
                                          (c32_ref, s32_ref))):
        ang = pos * inv_ref[row:row + 1, :]
        c_ref[...] = jnp.cos(ang)
        s_ref[...] = jnp.sin(ang) * sign_ref[row:row + 1, :]


def _rope_tables(positions):
    t = positions.size
    tm = min(ROW_TILE, t)
    inv_rows, sign_rows = [], []
    for half in (128, 64, 32):
        inv = ROPE_THETA ** (-jnp.arange(half, dtype=F32) / half)
        reps = LANES // half
        inv_rows.append(jnp.tile(inv, reps))
        if reps == 1:
            sign_rows.append(jnp.ones((LANES,), F32))
        else:
            sgn = jnp.concatenate([-jnp.ones((half,), F32), jnp.ones((half,), F32)])
            sign_rows.append(jnp.tile(sgn, reps // 2))
    inv = jnp.stack(inv_rows)
    sign = jnp.stack(sign_rows)
    tab = jax.ShapeDtypeStruct((t, LANES), F32)
    row = pl.BlockSpec((tm, LANES), lambda i: (i, 0))
    return pl.pallas_call(
        _rope_table_kernel,
        grid=(t // tm,),
        in_specs=[pl.BlockSpec((tm, 1), lambda i: (i, 0)), _resident((3, LANES)),
                  _resident((3, LANES))],
        out_specs=[row] * 6,
        out_shape=[tab] * 6,
        compiler_params=_params("parallel"),
        name="rope_tables",
    )(positions.reshape(t, 1), inv, sign)


def _ret_inproj_kernel(x_ref, gain_ref, w_ref, cos_ref, sin_ref, q_ref, k_ref, v_ref, g_ref,
                       *, heads, dk, dv, kscale):
    hn = _rms(x_ref[...], gain_ref[...]).astype(BF16)
    cos = cos_ref[...]
    sin = sin_ref[...]
    half = dk // 2
    for h in range(heads):
        for off, out_ref, scale in ((0, q_ref, None), (heads * dk, k_ref, kscale)):
            z = _dot(hn, w_ref[:, off + h * dk:off + (h + 1) * dk])
            x1, x2 = z[:, :half], z[:, half:]
            o1 = x1 * cos - x2 * sin
            o2 = x2 * cos + x1 * sin
            if scale is not None:
                o1, o2 = o1 * scale, o2 * scale
            out_ref[:, h * dk:h * dk + half] = o1.astype(BF16)
            out_ref[:, h * dk + half:(h + 1) * dk] = o2.astype(BF16)
    width = heads * dv
    for off, out_ref in ((2 * heads * dk, v_ref), (2 * heads * dk + width, g_ref)):
        for c in range(width // dv):
            out_ref[:, c * dv:(c + 1) * dv] = _dot(
                hn, w_ref[:, off + c * dv:off + (c + 1) * dv]).astype(BF16)


def _ret_core_kernel(q_ref, k_ref, v_ref, g_ref, dm_ref, qd_ref, kd_ref, cd_ref, gain_ref,
                     o_ref, state_ref, *, chunk, nchunks):
    state_ref[...] = jnp.zeros_like(state_ref)
    dm = dm_ref[0]
    qd = qd_ref[0]
    kd = kd_ref[0]
    cd = cd_ref[0]
    gain = gain_ref[0]

    def body(c, carry):
        rows = pl.ds(pl.multiple_of(c * chunk, chunk), chunk)
        q = q_ref[0, rows, :]
        k = k_ref[0, rows, :]
        v = v_ref[0, rows, :]
        scores = _dot_nt(q, k) * dm
        state = state_ref[...]
        o = _dot(scores.astype(BF16), v) + _dot(q, state.astype(BF16)) * qd
        k_dec = (k.astype(F32) * kd).astype(BF16)
        state_ref[...] = state * cd + _dot_tn(k_dec, v)
        y = _rms(o, gain)
        gate = g_ref[0, rows, :].astype(F32)
        o_ref[0, rows, :] = (y * (gate * jax.nn.sigmoid(gate))).astype(BF16)
        return carry

    lax.fori_loop(0, nchunks, body, 0)


def _retention_mixer(x2d, b, s, attn_gain, w_in, out_gain, cos, sin):
    t, d = x2d.shape
    heads = RET_HEADS
    dk = d // heads
    dv = 2 * dk
    chunk = RET_CHUNK
    tm = min(ROW_TILE, t)
    w = w_in.astype(BF16)
    row = lambda width: pl.BlockSpec((tm, width), lambda i: (i, 0))
    q, k, v, g = pl.pallas_call(
        functools.partial(_ret_inproj_kernel, heads=heads, dk=dk, dv=dv, kscale=dk ** -0.5),
        grid=(t // tm,),
        in_specs=[row(d), _resident((1, d)), _resident(w.shape), row(LANES), row(LANES)],
        out_specs=[row(heads * dk), row(heads * dk), row(heads * dv), row(heads * dv)],
        out_shape=[jax.ShapeDtypeStruct((t, heads * dk), BF16),
                   jax.ShapeDtypeStruct((t, heads * dk), BF16),
                   jax.ShapeDtypeStruct((t, heads * dv), BF16),
                   jax.ShapeDtypeStruct((t, heads * dv), BF16)],
        compiler_params=_params("parallel"),
        name="ret_inproj",
    )(x2d, attn_gain.reshape(1, d), w, cos, sin)

    log_gamma = jnp.log1p(-(2.0 ** (-5.0 - jnp.arange(heads, dtype=F32))))
    i = jnp.arange(chunk, dtype=F32)
    diff = i[:, None] - i[None, :]
    dm = jnp.where(diff >= 0, jnp.exp(log_gamma[:, None, None] * jnp.maximum(diff, 0.0)), 0.0)
    qd = jnp.exp(log_gamma[:, None] * (i + 1.0))
    kd = jnp.exp(log_gamma[:, None] * (chunk - 1.0 - i))
    cd = jnp.exp(log_gamma * chunk)
    qd = jnp.broadcast_to(qd[:, :, None], (heads, chunk, dv))
    kd = jnp.broadcast_to(kd[:, :, None], (heads, chunk, dk))
    cd = jnp.broadcast_to(cd[:, None, None], (heads, 1, dv))

    seq = lambda width: pl.BlockSpec((1, s, width), lambda bi, h: (bi, 0, h))
    per_head = lambda r, c: pl.BlockSpec((1, r, c), lambda bi, h: (h, 0, 0))
    o = pl.pallas_call(
        functools.partial(_ret_core_kernel, chunk=chunk, nchunks=s // chunk),
        grid=(b, heads),
        in_specs=[seq(dk), seq(dk), seq(dv), seq(dv), per_head(chunk, chunk),
                  per_head(chunk, dv), per_head(chunk, dk), per_head(1, dv), per_head(1, dv)],
        out_specs=seq(dv),
        out_shape=jax.ShapeDtypeStruct((b, s, heads * dv), BF16),
        scratch_shapes=[pltpu.VMEM((dk, dv), F32)],
        compiler_params=_params("parallel", "parallel"),
        name="ret_core",
    )(q.reshape(b, s, -1), k.reshape(b, s, -1), v.reshape(b, s, -1), g.reshape(b, s, -1),
      dm, qd, kd, cd, out_gain.reshape(heads, 1, dv))
    return o.reshape(t, heads * dv)


def _tail_kernel(a_ref, x_ref, wo_ref, gain_ref, wup_ref, wdn_ref, o_ref, *, ff_chunk):
    x1 = x_ref[...] + _dot(a_ref[...], wo_ref[...])
    hn = _rms(x1, gain_ref[...]).astype(BF16)
    acc = x1
    for c in range(wup_ref.shape[1] // ff_chunk):
        cols = slice(c * ff_chunk, (c + 1) * ff_chunk)
        u = jnp.maximum(_dot(hn, wup_ref[:, cols]), 0.0)
        acc = acc + _dot((u * u).astype(BF16), wdn_ref[cols, :])
    o_ref[...] = acc


def _mixer_out_and_mlp(a2d, x2d, w_out, mlp_gain, w_up, w_down):
    t, d = x2d.shape
    tm = min(ROW_TILE, t)
    wo, wu, wd = w_out.astype(BF16), w_up.astype(BF16), w_down.astype(BF16)
    row = lambda width: pl.BlockSpec((tm, width), lambda i: (i, 0))
    return pl.pallas_call(
        functools.partial(_tail_kernel, ff_chunk=1024),
        grid=(t // tm,),
        in_specs=[row(a2d.shape[1]), row(d), _resident(wo.shape), _resident((1, d)),
                  _resident(wu.shape), _resident(wd.shape)],
        out_specs=row(d),
        out_shape=jax.ShapeDtypeStruct((t, d), F32),
        compiler_params=_params("parallel"),
        name="outproj_mlp",
    )(a2d, x2d, wo, mlp_gain.reshape(1, d), wu, wd)


def _rope_lanes(z, cos, sin_signed, half):
    if 2 * half == LANES:
        partner = pltpu.roll(z, half, 1)
    else:
        lane = lax.broadcasted_iota(jnp.int32, z.shape, 1)
        first = (lane % (2 * half)) < half
        partner = jnp.where(first, pltpu.roll(z, LANES - half, 1), pltpu.roll(z, half, 1))
    return z * cos + partner * sin_signed


def _dsa_inproj_kernel(x_ref, gain_ref, wq_ref, wkv_ref, wqi_ref, wki_ref, wwi_ref,
                       qg_ref, kg_ref, kig_ref, c64_ref, s64_ref, c32_ref, s32_ref,
                       q_ref, k_ref, v_ref, qi_ref, ki_ref, wi_ref,
                       *, heads, kv_heads, dh, idx_heads, di, qscale, wscale):
    hn = _rms(x_ref[...], gain_ref[...]).astype(BF16)
    c64, s64 = c64_ref[...], s64_ref[...]
    c32, s32 = c32_ref[...], s32_ref[...]
    for h in range(heads):
        z = _rms(_dot(hn, wq_ref[:, h * dh:(h + 1) * dh]), qg_ref[...])
        q_ref[:, h * dh:(h + 1) * dh] = (_rope_lanes(z, c64, s64, dh // 2) * qscale).astype(BF16)
    for h in range(kv_heads):
        z = _rms(_dot(hn, wkv_ref[:, h * dh:(h + 1) * dh]), kg_ref[...])
        k_ref[:, h * dh:(h + 1) * dh] = _rope_lanes(z, c64, s64, dh // 2).astype(BF16)
    voff = kv_heads * dh
    v_ref[...] = _dot(hn, wkv_ref[:, voff:voff + kv_heads * dh]).astype(BF16)
    for c in range(idx_heads * di // LANES):
        z = _dot(hn, wqi_ref[:, c * LANES:(c + 1) * LANES])
        qi_ref[:, c * LANES:(c + 1) * LANES] = _rope_lanes(z, c32, s32, di // 2).astype(BF16)
    z = _dot(hn, wki_ref[...])
    z = z * lax.rsqrt(jnp.sum(z * z, axis=-1, keepdims=True) * (1.0 / di) + EPS) * kig_ref[...]
    ki_ref[...] = _rope_lanes(z, c32, s32, di // 2)[:, :di].astype(BF16)
    wi_ref[...] = _dot(hn, wwi_ref[...])[:, :idx_heads] * wscale


def _dsa_attn_kernel(q_ref, k_ref, v_ref, qi_ref, ki_ref, wi_ref, o_ref, sc_ref, istar_ref,
                     *, q_start, topk, heads, kv_heads, dh, idx_heads, di):
    tq = q_ref.shape[1]
    ke = k_ref.shape[1]
    neg = float(jnp.finfo(F32).min)
    t_pos = q_start + lax.broadcasted_iota(jnp.int32, (tq, ke), 0)
    s_pos = lax.broadcasted_iota(jnp.int32, (tq, ke), 1)
    causal = s_pos <= t_pos

    if ke <= topk:
        sel = causal
    else:
        ki = ki_ref[0]
        wi = wi_ref[0]
        score = jnp.zeros((tq, ke), F32)
        for h in range(idx_heads):
            rel = jnp.maximum(_dot_nt(qi_ref[0, :, h * di:(h + 1) * di], ki), 0.0)
            score = score + rel * wi[:, h:h + 1]
        sc_ref[...] = jnp.where(causal, score, neg)

        kvec = jnp.minimum(q_start + lax.broadcasted_iota(jnp.int32, (tq, 1), 0) + 1,
                           topk).astype(F32)

        def count_ge(th):
            return jnp.sum((sc_ref[...] >= th).astype(F32), axis=1, keepdims=True)

        sc = sc_ref[...]
        row_max = jnp.max(sc, axis=1, keepdims=True)
        row_min = jnp.min(jnp.where(causal, sc, row_max), axis=1, keepdims=True)
        lo0 = row_min
        hi0 = row_max + (row_max - row_min) + jnp.abs(row_max) * 1e-3 + 1.0

        def bisect(carry):
            lo, hi, c_hi = carry
            mid = lo * 0.5 + hi * 0.5
            c = count_ge(mid)
            ge = c >= kvec
            return jnp.where(ge, mid, lo), jnp.where(ge, hi, mid), jnp.where(ge, c_hi, c)

        def snap(hi):
            s_all = sc_ref[...]
            v = jnp.max(jnp.where(s_all < hi, s_all, neg), axis=1, keepdims=True)
            cv = count_ge(v)
            return v, cv

        carry = lax.fori_loop(0, BISECT_STEPS, lambda _, c: bisect(c),
                              (lo0, hi0, jnp.zeros((tq, 1), F32)))

        def pending(carry):
            _, cv = snap(carry[1])
            return jnp.sum((cv < kvec).astype(F32))

        def tail_cond(state):
            return state[1] > 0.0

        def tail_body(state):
            c = state[0]
            for _ in range(4):
                c = bisect(c)
            return c, pending(c)

        carry, _ = lax.while_loop(tail_cond, tail_body, (carry, pending(carry)))
        _, hi, c_hi = carry
        v, cv = snap(hi)
        need = kvec - c_hi
        extra = jnp.sum((cv - c_hi > need).astype(F32))

        istar_ref[...] = jnp.full((tq, 1), ke, jnp.int32)

        @pl.when(extra > 0.0)
        def _():
            eq = sc_ref[...] == v
            cols = lax.broadcasted_iota(jnp.int32, (tq, ke), 1)

            def idx_step(_, c):
                lo_i, hi_i = c
                mid_i = (lo_i + hi_i) >> 1
                cnt = jnp.sum((eq & (cols <= mid_i)).astype(F32), axis=1, keepdims=True)
                ok = cnt >= need
                return jnp.where(ok, lo_i, mid_i), jnp.where(ok, mid_i, hi_i)

            steps = max(1, (ke - 1).bit_length()) + 1
            _, hi_i = lax.fori_loop(0, steps, idx_step,
                                    (jnp.full((tq, 1), -1, jnp.int32),
                                     jnp.full((tq, 1), ke - 1, jnp.int32)))
            istar_ref[...] = hi_i

        sc = sc_ref[...]
        sel = ((sc > v) | ((sc == v) & (s_pos <= istar_ref[...]))) & causal

    group = heads // kv_heads
    for g in range(kv_heads):
        kg = k_ref[0, :, g * dh:(g + 1) * dh]
        vg = v_ref[0, :, g * dh:(g + 1) * dh]
        for r in range(group):
            h = g * group + r
            logits = jnp.where(sel, _dot_nt(q_ref[0, :, h * dh:(h + 1) * dh], kg), neg)
            e = jnp.exp(logits - jnp.max(logits, axis=1, keepdims=True))
            denom = jnp.sum(e, axis=1, keepdims=True)
            o_ref[0, :, h * dh:(h + 1) * dh] = (_dot(e.astype(BF16), vg) / denom).astype(BF16)


def _sparse_attention_mixer(x2d, b, s, attn_gain, w_in, q_gain, k_gain, kidx_gain, tabs):
    t, d = x2d.shape
    heads, kv_heads, idx_heads, di = ATT_HEADS, ATT_KV_HEADS, IDX_HEADS, IDX_DH
    dh = d // heads
    c64, s64, c32, s32 = tabs
    tm = min(ROW_TILE, t)
    nq, nkv, nqi = heads * dh, 2 * kv_heads * dh, idx_heads * di
    pad_to = lambda a, n: jnp.pad(a, ((0, 0), (0, n - a.shape[-1])))
    wq = w_in[:, :nq].astype(BF16)
    wkv = w_in[:, nq:nq + nkv].astype(BF16)
    wqi = w_in[:, nq + nkv:nq + nkv + nqi].astype(BF16)
    off = nq + nkv + nqi
    wki = pad_to(w_in[:, off:off + di], LANES).astype(BF16)
    wwi = pad_to(w_in[:, off + di:off + di + idx_heads], LANES).astype(BF16)
    kig = pad_to(kidx_gain.reshape(1, di), LANES)
    row = lambda width: pl.BlockSpec((tm, width), lambda i: (i, 0))
    shp = lambda width, dt: jax.ShapeDtypeStruct((t, width), dt)
    q, k, v, qi, ki, wi = pl.pallas_call(
        functools.partial(_dsa_inproj_kernel, heads=heads, kv_heads=kv_heads, dh=dh,
                          idx_heads=idx_heads, di=di, qscale=dh ** -0.5,
                          wscale=idx_heads ** -0.5 * di ** -0.5),
        grid=(t // tm,),
        in_specs=[row(d), _resident((1, d)), _resident(wq.shape), _resident(wkv.shape),
                  _resident(wqi.shape), _resident(wki.shape), _resident(wwi.shape),
                  _resident((1, dh)), _resident((1, dh)), _resident((1, LANES)),
                  row(LANES), row(LANES), row(LANES), row(LANES)],
        out_specs=[row(nq), row(kv_heads * dh), row(kv_heads * dh), row(nqi), row(di),
                   row(idx_heads)],
        out_shape=[shp(nq, BF16), shp(kv_heads * dh, BF16), shp(kv_heads * dh, BF16),
                   shp(nqi, BF16), shp(di, BF16), shp(idx_heads, F32)],
        compiler_params=_params("parallel"),
        name="dsa_inproj",
    )(x2d, attn_gain.reshape(1, d), wq, wkv, wqi, wki, wwi, q_gain.reshape(1, dh),
      k_gain.reshape(1, dh), kig, c64, s64, c32, s32)

    topk = min(TOPK_MAX, s // 4)
    tq = min(Q_TILE, s)
    r3 = lambda a: a.reshape(b, s, a.shape[-1])
    q, k, v, qi, ki, wi = map(r3, (q, k, v, qi, ki, wi))
    outs = []
    for j in range(s // tq):
        ke = (j + 1) * tq
        qblk = lambda width, j=j: pl.BlockSpec((1, tq, width), lambda bi: (bi, j, 0))
        kblk = lambda width, ke=ke: pl.BlockSpec((1, ke, width), lambda bi: (bi, 0, 0))
        outs.append(pl.pallas_call(
            functools.partial(_dsa_attn_kernel, q_start=j * tq, topk=topk, heads=heads,
                              kv_heads=kv_heads, dh=dh, idx_heads=idx_heads, di=di),
            grid=(b,),
            in_specs=[qblk(nq), kblk(kv_heads * dh), kblk(kv_heads * dh), qblk(nqi), kblk(di),
                      qblk(idx_heads)],
            out_specs=pl.BlockSpec((1, tq, nq), lambda bi: (bi, 0, 0)),
            out_shape=jax.ShapeDtypeStruct((b, tq, nq), BF16),
            scratch_shapes=[pltpu.VMEM((tq, ke), F32), pltpu.VMEM((tq, 1), jnp.int32)],
            compiler_params=_params("parallel"),
            name=f"dsa_attn_{j}",
        )(q, k, v, qi, ki, wi))
    return jnp.concatenate(outs, axis=1).reshape(t, nq)


def kernel(x, positions, attn_norm, ret_w_in, ret_out_norm, ret_w_out, dsa_w_in, dsa_q_norm,
           dsa_k_norm, dsa_kidx_norm, dsa_w_out, mlp_norm, mlp_w_up, mlp_w_down):
    b, s, d = x.shape
    depth = attn_norm.shape[0]
    c128, s128, c64, s64, c32, s32 = _rope_tables(positions)
    x2d = x.reshape(b * s, d)
    for i in range(depth):
        j = i // 2
        if i % 2 == 0:
            a = _retention_mixer(x2d, b, s, attn_norm[i], ret_w_in[j], ret_out_norm[j],
                                 c128, s128)
            w_out = ret_w_out[j]
        else:
            a = _sparse_attention_mixer(x2d, b, s, attn_norm[i], dsa_w_in[j], dsa_q_norm[j],
                                        dsa_k_norm[j], dsa_kidx_norm[j], (c64, s64, c32, s32))
            w_out = dsa_w_out[j]
        x2d = _mixer_out_and_mlp(a, x2d, w_out, mlp_norm[i], mlp_w_up[i], mlp_w_down[i])
    return x2d.reshape(b, s, d)
```

```python
import functools

import jax
import jax.numpy as jnp
from jax import lax
from jax.experimental import pallas as pl
from jax.experimental.pallas import tpu as pltpu

F32 = jnp.float32
BF16 = jnp.bfloat16

EPS = 1e-6
ROPE_THETA = 10000.0
RET_HEADS = 4
RET_CHUNK = 128
ATT_HEADS = 8
ATT_KV_HEADS = 2
IDX_HEADS = 8
IDX_DH = 64
TOPK_MAX = 256
LANES = 128
SUBLANES = 8

ROW_TILE = 512
Q_TILE = 256
KEY_BLOCK = 128
SWEEP_UNROLL = 4
VMEM_LIMIT = 56 * 1024 * 1024
BISECT_STEPS = 13


def _params(*sem):
    return pltpu.CompilerParams(dimension_semantics=sem, vmem_limit_bytes=VMEM_LIMIT)


def _resident(shape):
    nd = len(shape)
    return pl.BlockSpec(shape, lambda *_: (0,) * nd, pipeline_mode=pl.Buffered(1))


def _rms(x, gain):
    return x * lax.rsqrt(jnp.mean(x * x, axis=-1, keepdims=True) + EPS) * gain


def _dot(a, b):
    return jnp.dot(a, b, preferred_element_type=F32)


def _dot_nt(a, b):
    return lax.dot_general(a, b, (((1,), (1,)), ((), ())), preferred_element_type=F32)


def _dot_tn(a, b):
    return lax.dot_general(a, b, (((0,), (0,)), ((), ())), preferred_element_type=F32)


def _rope_table_kernel(pos_ref, inv_ref, sign_ref, c128_ref, s128_ref, c64_ref, s64_ref,
                       c32_ref, s32_ref):
    pos = pos_ref[...].astype(F32)
    for row, (c_ref, s_ref) in enumerate(((c128_ref, s128_ref), (c64_ref, s64_ref),
                                          (c32_ref, s32_ref))):
        ang = pos * inv_ref[row:row + 1, :]
        c_ref[...] = jnp.cos(ang)
        s_ref[...] = jnp.sin(ang) * sign_ref[row:row + 1, :]


def _rope_tables(positions):
    t = positions.size
    tm = min(ROW_TILE, t)
    inv_rows, sign_rows = [], []
    for half in (128, 64, 32):
        inv = ROPE_THETA ** (-jnp.arange(half, dtype=F32) / half)
        reps = LANES // half
        inv_rows.append(jnp.tile(inv, reps))
        if reps == 1:
            sign_rows.append(jnp.ones((LANES,), F32))
        else:
            sgn = jnp.concatenate([-jnp.ones((half,), F32), jnp.ones((half,), F32)])
            sign_rows.append(jnp.tile(sgn, reps // 2))
    inv = jnp.stack(inv_rows)
    sign = jnp.stack(sign_rows)
    tab = jax.ShapeDtypeStruct((t, LANES), F32)
    row = pl.BlockSpec((tm, LANES), lambda i: (i, 0))
    return pl.pallas_call(
        _rope_table_kernel,
        grid=(t // tm,),
        in_specs=[pl.BlockSpec((tm, 1), lambda i: (i, 0)), _resident((3, LANES)),
                  _resident((3, LANES))],
        out_specs=[row] * 6,
        out_shape=[tab] * 6,
        compiler_params=_params("parallel"),
        name="rope_tables",
    )(positions.reshape(t, 1), inv, sign)


def _ret_inproj_kernel(x_ref, gain_ref, w_ref, cos_ref, sin_ref, q_ref, k_ref, v_ref, g_ref,
                       *, heads, dk, dv, kscale):
    hn = _rms(x_ref[...], gain_ref[...]).astype(BF16)
    cos = cos_ref[...]
    sin = sin_ref[...]
    half = dk // 2
    for h in range(heads):
        for off, out_ref, scale in ((0, q_ref, None), (heads * dk, k_ref, kscale)):
            z = _dot(hn, w_ref[:, off + h * dk:off + (h + 1) * dk])
            x1, x2 = z[:, :half], z[:, half:]
            o1 = x1 * cos - x2 * sin
            o2 = x2 * cos + x1 * sin
            if scale is not None:
                o1, o2 = o1 * scale, o2 * scale
            out_ref[:, h * dk:h * dk + half] = o1.astype(BF16)
            out_ref[:, h * dk + half:(h + 1) * dk] = o2.astype(BF16)
    width = heads * dv
    for off, out_ref in ((2 * heads * dk, v_ref), (2 * heads * dk + width, g_ref)):
        for c in range(width // dv):
            out_ref[:, c * dv:(c + 1) * dv] = _dot(
                hn, w_ref[:, off + c * dv:off + (c + 1) * dv]).astype(BF16)


def _ret_core_kernel(q_ref, k_ref, v_ref, g_ref, dm_ref, qd_ref, kd_ref, cd_ref, gain_ref,
                     o_ref, state_ref, *, chunk, nchunks):
    state_ref[...] = jnp.zeros_like(state_ref)
    dm = dm_ref[0]
    qd = qd_ref[0]
    kd = kd_ref[0]
    cd = cd_ref[0]
    gain = gain_ref[0]

    def body(c, carry):
        rows = pl.ds(pl.multiple_of(c * chunk, chunk), chunk)
        q = q_ref[0, rows, :]
        k = k_ref[0, rows, :]
        v = v_ref[0, rows, :]
        scores = _dot_nt(q, k) * dm
        state = state_ref[...]
        o = _dot(scores.astype(BF16), v) + _dot(q, state.astype(BF16)) * qd
        k_dec = (k.astype(F32) * kd).astype(BF16)
        state_ref[...] = state * cd + _dot_tn(k_dec, v)
        y = _rms(o, gain)
        gate = g_ref[0, rows, :].astype(F32)
        o_ref[0, rows, :] = (y * (gate * jax.nn.sigmoid(gate))).astype(BF16)
        return carry

    lax.fori_loop(0, nchunks, body, 0)


def _retention_mixer(x2d, b, s, attn_gain, w_in, out_gain, cos, sin):
    t, d = x2d.shape
    heads = RET_HEADS
    dk = d // heads
    dv = 2 * dk
    chunk = RET_CHUNK
    tm = min(ROW_TILE, t)
    w = w_in.astype(BF16)
    row = lambda width: pl.BlockSpec((tm, width), lambda i: (i, 0))
    q, k, v, g = pl.pallas_call(
        functools.partial(_ret_inproj_kernel, heads=heads, dk=dk, dv=dv, kscale=dk ** -0.5),
        grid=(t // tm,),
        in_specs=[row(d), _resident((1, d)), _resident(w.shape), row(LANES), row(LANES)],
        out_specs=[row(heads * dk), row(heads * dk), row(heads * dv), row(heads * dv)],
        out_shape=[jax.ShapeDtypeStruct((t, heads * dk), BF16),
                   jax.ShapeDtypeStruct((t, heads * dk), BF16),
                   jax.ShapeDtypeStruct((t, heads * dv), BF16),
                   jax.ShapeDtypeStruct((t, heads * dv), BF16)],
        compiler_params=_params("parallel"),
        name="ret_inproj",
    )(x2d, attn_gain.reshape(1, d), w, cos, sin)

    log_gamma = jnp.log1p(-(2.0 ** (-5.0 - jnp.arange(heads, dtype=F32))))
    i = jnp.arange(chunk, dtype=F32)
    diff = i[:, None] - i[None, :]
    dm = jnp.where(diff >= 0, jnp.exp(log_gamma[:, None, None] * jnp.maximum(diff, 0.0)), 0.0)
    qd = jnp.exp(log_gamma[:, None] * (i + 1.0))
    kd = jnp.exp(log_gamma[:, None] * (chunk - 1.0 - i))
    cd = jnp.exp(log_gamma * chunk)
    qd = jnp.broadcast_to(qd[:, :, None], (heads, chunk, dv))
    kd = jnp.broadcast_to(kd[:, :, None], (heads, chunk, dk))
    cd = jnp.broadcast_to(cd[:, None, None], (heads, 1, dv))

    seq = lambda width: pl.BlockSpec((1, s, width), lambda bi, h: (bi, 0, h))
    per_head = lambda r, c: pl.BlockSpec((1, r, c), lambda bi, h: (h, 0, 0))
    o = pl.pallas_call(
        functools.partial(_ret_core_kernel, chunk=chunk, nchunks=s // chunk),
        grid=(b, heads),
        in_specs=[seq(dk), seq(dk), seq(dv), seq(dv), per_head(chunk, chunk),
                  per_head(chunk, dv), per_head(chunk, dk), per_head(1, dv), per_head(1, dv)],
        out_specs=seq(dv),
        out_shape=jax.ShapeDtypeStruct((b, s, heads * dv), BF16),
        scratch_shapes=[pltpu.VMEM((dk, dv), F32)],
        compiler_params=_params("parallel", "parallel"),
        name="ret_core",
    )(q.reshape(b, s, -1), k.reshape(b, s, -1), v.reshape(b, s, -1), g.reshape(b, s, -1),
      dm, qd, kd, cd, out_gain.reshape(heads, 1, dv))
    return o.reshape(t, heads * dv)


def _tail_kernel(a_ref, x_ref, wo_ref, gain_ref, wup_ref, wdn_ref, o_ref, *, ff_chunk):
    x1 = x_ref[...] + _dot(a_ref[...], wo_ref[...])
    hn = _rms(x1, gain_ref[...]).astype(BF16)
    acc = x1
    for c in range(wup_ref.shape[1] // ff_chunk):
        cols = slice(c * ff_chunk, (c + 1) * ff_chunk)
        u = jnp.maximum(_dot(hn, wup_ref[:, cols]), 0.0)
        acc = acc + _dot((u * u).astype(BF16), wdn_ref[cols, :])
    o_ref[...] = acc


def _mixer_out_and_mlp(a2d, x2d, w_out, mlp_gain, w_up, w_down):
    t, d = x2d.shape
    tm = min(ROW_TILE, t)
    wo, wu, wd = w_out.astype(BF16), w_up.astype(BF16), w_down.astype(BF16)
    row = lambda width: pl.BlockSpec((tm, width), lambda i: (i, 0))
    return pl.pallas_call(
        functools.partial(_tail_kernel, ff_chunk=1024),
        grid=(t // tm,),
        in_specs=[row(a2d.shape[1]), row(d), _resident(wo.shape), _resident((1, d)),
                  _resident(wu.shape), _resident(wd.shape)],
        out_specs=row(d),
        out_shape=jax.ShapeDtypeStruct((t, d), F32),
        compiler_params=_params("parallel"),
        name="outproj_mlp",
    )(a2d, x2d, wo, mlp_gain.reshape(1, d), wu, wd)


def _rope_lanes(z, cos, sin_signed, half):
    if 2 * half == LANES:
        partner = pltpu.roll(z, half, 1)
    else:
        lane = lax.broadcasted_iota(jnp.int32, z.shape, 1)
        first = (lane % (2 * half)) < half
        partner = jnp.where(first, pltpu.roll(z, LANES - half, 1), pltpu.roll(z, half, 1))
    return z * cos + partner * sin_signed


def _dsa_inproj_kernel(x_ref, gain_ref, wq_ref, wk_ref, wvt_ref, wqi_ref, wki_ref, wwit_ref,
                       qg_ref, kg_ref, kig_ref, c64_ref, s64_ref, c32_ref, s32_ref,
                       q_ref, k_ref, vt_ref, qi_ref, ki_ref, wit_ref,
                       *, heads, kv_heads, dh, idx_heads, di, qscale, wscale):
    hn = _rms(x_ref[...], gain_ref[...]).astype(BF16)
    c64, s64 = c64_ref[...], s64_ref[...]
    c32, s32 = c32_ref[...], s32_ref[...]
    for h in range(heads):
        z = _rms(_dot(hn, wq_ref[:, h * dh:(h + 1) * dh]), qg_ref[...])
        q_ref[:, h * dh:(h + 1) * dh] = (_rope_lanes(z, c64, s64, dh // 2) * qscale).astype(BF16)
    for h in range(kv_heads):
        z = _rms(_dot(hn, wk_ref[:, h * dh:(h + 1) * dh]), kg_ref[...])
        k_ref[:, h * dh:(h + 1) * dh] = _rope_lanes(z, c64, s64, dh // 2).astype(BF16)
    vt_ref[0] = _dot_nt(wvt_ref[...], hn).astype(BF16)
    per_tile = LANES // di
    for c in range(idx_heads // per_tile):
        z = _rope_lanes(_dot(hn, wqi_ref[:, c * LANES:(c + 1) * LANES]), c32, s32, di // 2)
        for r in range(per_tile):
            qi_ref[c * per_tile + r] = z[:, r * di:(r + 1) * di].astype(BF16)
    z = _dot(hn, wki_ref[...])
    z = z * lax.rsqrt(jnp.sum(z * z, axis=-1, keepdims=True) * (1.0 / di) + EPS) * kig_ref[...]
    ki_ref[...] = _rope_lanes(z, c32, s32, di // 2)[:, :di].astype(BF16)
    wit_ref[0] = _dot_nt(wwit_ref[...], hn)[:idx_heads, :] * wscale


def _dsa_attn_kernel(q_ref, k_ref, vt_ref, qi_ref, ki_ref, wit_ref, o_ref, sc_ref,
                     *, tile, topk, heads, kv_heads, dh, idx_heads, index_steps):
    tq = q_ref.shape[1]
    ke = k_ref.shape[1]
    kb = KEY_BLOCK
    slabs = kb // SUBLANES
    j = tile
    nblk = ke // kb
    neg = float(jnp.finfo(F32).min)
    big = float(jnp.finfo(F32).max)
    diag = (lax.broadcasted_iota(jnp.int32, (tq, tq), 0)
            <= lax.broadcasted_iota(jnp.int32, (tq, tq), 1))

    def rows_of(i):
        return pl.ds(pl.multiple_of(i * kb, kb), kb)

    def full(val):
        return jnp.full((SUBLANES, tq), val, F32)

    def spread(row):
        return jnp.broadcast_to(row, (SUBLANES, tq))

    def sweep(fn, init):
        return lax.fori_loop(0, nblk, lambda i, a: fn(i, sc_ref[rows_of(i), :], a), init,
                             unroll=min(nblk, SWEEP_UNROLL))

    if ke <= topk:
        sc_ref[...] = jnp.where(diag, 0.0, neg)
    else:
        wit = wit_ref[0]
        ki = ki_ref[0]
        score = jnp.zeros((ke, tq), F32)
        for h in range(idx_heads):
            score = score + jnp.maximum(_dot_nt(ki, qi_ref[h, 0]), 0.0) * wit[h:h + 1, :]
        last = score[ke - tq:, :]
        row_max = jnp.max(jnp.where(diag, last, neg), axis=0, keepdims=True)
        row_min = jnp.min(jnp.where(diag, last, big), axis=0, keepdims=True)
        if ke > tq:
            row_max = jnp.maximum(row_max, jnp.max(score[:ke - tq, :], axis=0, keepdims=True))
            row_min = jnp.minimum(row_min, jnp.min(score[:ke - tq, :], axis=0, keepdims=True))
            sc_ref[:ke - tq, :] = score[:ke - tq, :]
        sc_ref[ke - tq:, :] = jnp.where(diag, last, neg)
        q_pos = j * tq + lax.broadcasted_iota(jnp.int32, (1, tq), 1)
        kvec = jnp.minimum(q_pos + 1, topk).astype(F32)

        def count_ge(th):
            thb = spread(th)[None]
            acc = sweep(lambda i, blk, a: a + jnp.sum(
                jnp.where(blk.reshape(slabs, SUBLANES, tq) >= thb, 1.0, 0.0), axis=0),
                jnp.zeros((SUBLANES, tq), F32))
            return jnp.sum(acc, axis=0, keepdims=True)

        def max_below(hi):
            hib = spread(hi)[None]

            def step(i, blk, a):
                blk = blk.reshape(slabs, SUBLANES, tq)
                return jnp.maximum(a, jnp.max(jnp.where(blk < hib, blk, neg), axis=0))

            return jnp.max(sweep(step, full(neg)), axis=0, keepdims=True)

        def bisect(_, carry):
            lo, hi, c_hi = carry
            mid = lo * 0.5 + hi * 0.5
            c = count_ge(mid)
            ge = c >= kvec
            return jnp.where(ge, mid, lo), jnp.where(ge, hi, mid), jnp.where(ge, c_hi, c)

        hi0 = row_max + jnp.abs(row_max) * 1e-3 + 1e-3
        _, hi, c_hi = lax.fori_loop(0, BISECT_STEPS, bisect,
                                    (row_min, hi0, jnp.zeros((1, tq), F32)))

        def snap(hi):
            v = max_below(hi)
            return v, count_ge(v)

        def pending(cv):
            return jnp.max(jnp.where(cv < kvec, 1.0, 0.0))

        def refine(state):
            hi, c_hi, v, cv, _ = state
            short = cv < kvec
            hi = jnp.where(short, v, hi)
            c_hi = jnp.where(short, cv, c_hi)
            v, cv = snap(hi)
            return hi, c_hi, v, cv, pending(cv)

        v, cv = snap(hi)
        _, c_hi, v, cv, _ = lax.while_loop(lambda st: st[4] > 0.0, refine,
                                           (hi, c_hi, v, cv, pending(cv)))
        need = kvec - c_hi
        surplus = jnp.max(jnp.where(cv > kvec, 1.0, 0.0)) > 0.0
        vb = spread(v)

        @pl.when(jnp.logical_not(surplus))
        def _():
            def to_bias(i, carry):
                blk = sc_ref[rows_of(i), :].reshape(slabs, SUBLANES, tq)
                sc_ref[rows_of(i), :] = jnp.where(blk >= vb[None], 0.0, neg).reshape(kb, tq)
                return carry
            lax.fori_loop(0, nblk, to_bias, 0)

        @pl.when(surplus)
        def _():
            def key_idx(i):
                return (lax.broadcasted_iota(jnp.int32, (slabs, SUBLANES, tq), 0) * SUBLANES
                        + lax.broadcasted_iota(jnp.int32, (slabs, SUBLANES, tq), 1) + i * kb)

            def idx_step(_, carry):
                lo_i, hi_i = carry
                mid_i = (lo_i + hi_i) >> 1
                midb = jnp.broadcast_to(mid_i, (SUBLANES, tq))
                acc = sweep(lambda i, blk, a: a + jnp.sum(jnp.where(
                    (blk.reshape(slabs, SUBLANES, tq) == vb[None]) & (key_idx(i) <= midb[None]),
                    1.0, 0.0), axis=0), jnp.zeros((SUBLANES, tq), F32))
                ok = jnp.sum(acc, axis=0, keepdims=True) >= need
                return jnp.where(ok, lo_i, mid_i), jnp.where(ok, mid_i, hi_i)

            _, istar = lax.fori_loop(0, index_steps, idx_step,
                                     (jnp.full((1, tq), -1, jnp.int32),
                                      jnp.full((1, tq), nblk * kb - 1, jnp.int32)))
            istarb = jnp.broadcast_to(istar, (SUBLANES, tq))

            def to_bias(i, carry):
                blk = sc_ref[rows_of(i), :].reshape(slabs, SUBLANES, tq)
                keep = (blk > vb[None]) | ((blk == vb[None]) & (key_idx(i) <= istarb[None]))
                sc_ref[rows_of(i), :] = jnp.where(keep, 0.0, neg).reshape(kb, tq)
                return carry
            lax.fori_loop(0, nblk, to_bias, 0)

    group = heads // kv_heads
    for h in range(heads):
        g = h // group
        logits = _dot_nt(k_ref[0, :, g * dh:(g + 1) * dh],
                         q_ref[0, :, h * dh:(h + 1) * dh]) + sc_ref[...]
        e = jnp.exp(logits - jnp.max(logits, axis=0, keepdims=True))
        denom = jnp.sum(e, axis=0, keepdims=True)
        out_t = _dot(vt_ref[0, g * dh:(g + 1) * dh, :], e.astype(BF16))
        o_ref[0, :, h * dh:(h + 1) * dh] = (out_t / denom).T.astype(BF16)


def _sparse_attention_mixer(x2d, b, s, attn_gain, w_in, q_gain, k_gain, kidx_gain, tabs):
    t, d = x2d.shape
    heads, kv_heads, idx_heads, di = ATT_HEADS, ATT_KV_HEADS, IDX_HEADS, IDX_DH
    dh = d // heads
    c64, s64, c32, s32 = tabs
    tm = min(ROW_TILE, s)
    per_b = s // tm
    kb = KEY_BLOCK
    nq, nkv, nqi = heads * dh, kv_heads * dh, idx_heads * di
    pad_cols = lambda a, n: jnp.pad(a, ((0, 0), (0, n - a.shape[-1])))
    wq = w_in[:, :nq].astype(BF16)
    wk = w_in[:, nq:nq + nkv].astype(BF16)
    wvt = w_in[:, nq + nkv:nq + 2 * nkv].T.astype(BF16)
    off = nq + 2 * nkv
    wqi = w_in[:, off:off + nqi].astype(BF16)
    off += nqi
    wki = pad_cols(w_in[:, off:off + di], LANES).astype(BF16)
    wwit = jnp.pad(w_in[:, off + di:off + di + idx_heads].T,
                   ((0, 2 * SUBLANES - idx_heads), (0, 0))).astype(BF16)
    kig = pad_cols(kidx_gain.reshape(1, di), LANES)
    row = lambda width: pl.BlockSpec((tm, width), lambda i: (i, 0))
    shp = lambda width, dt: jax.ShapeDtypeStruct((t, width), dt)
    q, k, vt, qi, ki, wit = pl.pallas_call(
        functools.partial(_dsa_inproj_kernel, heads=heads, kv_heads=kv_heads, dh=dh,
                          idx_heads=idx_heads, di=di, qscale=dh ** -0.5,
                          wscale=idx_heads ** -0.5 * di ** -0.5),
        grid=(t // tm,),
        in_specs=[row(d), _resident((1, d)), _resident(wq.shape), _resident(wk.shape),
                  _resident(wvt.shape), _resident(wqi.shape), _resident(wki.shape),
                  _resident(wwit.shape), _resident((1, dh)), _resident((1, dh)),
                  _resident((1, LANES)), row(LANES), row(LANES), row(LANES), row(LANES)],
        out_specs=[row(nq), row(nkv),
                   pl.BlockSpec((1, nkv, tm), lambda i: (i // per_b, 0, i % per_b)),
                   pl.BlockSpec((idx_heads, tm, di), lambda i: (0, i, 0)),
                   row(di),
                   pl.BlockSpec((1, idx_heads, tm), lambda i: (i // per_b, 0, i % per_b))],
        out_shape=[shp(nq, BF16), shp(nkv, BF16),
                   jax.ShapeDtypeStruct((b, nkv, s), BF16),
                   jax.ShapeDtypeStruct((idx_heads, t, di), BF16),
                   shp(di, BF16),
                   jax.ShapeDtypeStruct((b, idx_heads, s), F32)],
        compiler_params=_params("parallel"),
        name="dsa_inproj",
    )(x2d, attn_gain.reshape(1, d), wq, wk, wvt, wqi, wki, wwit, q_gain.reshape(1, dh),
      k_gain.reshape(1, dh), kig, c64, s64, c32, s32)

    topk = min(TOPK_MAX, s // 4)
    tq = min(Q_TILE, s)
    q, k, qi, ki = (q.reshape(b, s, nq), k.reshape(b, s, nkv), qi.reshape(idx_heads, b, s, di),
                    ki.reshape(b, s, di))
    outs = []
    for tile in range(s // tq):
        ke = (tile + 1) * tq
        outs.append(pl.pallas_call(
            functools.partial(_dsa_attn_kernel, tile=tile, topk=topk, heads=heads,
                              kv_heads=kv_heads, dh=dh, idx_heads=idx_heads,
                              index_steps=(ke - 1).bit_length() + 1),
            grid=(b,),
            in_specs=[pl.BlockSpec((1, tq, nq), lambda bi, tile=tile: (bi, tile, 0)),
                      pl.BlockSpec((1, ke, nkv), lambda bi: (bi, 0, 0)),
                      pl.BlockSpec((1, nkv, ke), lambda bi: (bi, 0, 0)),
                      pl.BlockSpec((idx_heads, 1, tq, di), lambda bi, tile=tile: (0, bi, tile, 0)),
                      pl.BlockSpec((1, ke, di), lambda bi: (bi, 0, 0)),
                      pl.BlockSpec((1, idx_heads, tq), lambda bi, tile=tile: (bi, 0, tile))],
            out_specs=pl.BlockSpec((1, tq, nq), lambda bi: (bi, 0, 0)),
            out_shape=jax.ShapeDtypeStruct((b, tq, nq), BF16),
            scratch_shapes=[pltpu.VMEM((ke, tq), F32)],
            compiler_params=_params("parallel"),
            name=f"dsa_attn_{tile}",
        )(q, k, vt, qi, ki, wit))
    return jnp.concatenate(outs, axis=1).reshape(t, nq)


def kernel(x, positions, attn_norm, ret_w_in, ret_out_norm, ret_w_out, dsa_w_in, dsa_q_norm,
           dsa_k_norm, dsa_kidx_norm, dsa_w_out, mlp_norm, mlp_w_up, mlp_w_down):
    b, s, d = x.shape
    depth = attn_norm.shape[0]
    c128, s128, c64, s64, c32, s32 = _rope_tables(positions)
    x2d = x.reshape(b * s, d)
    for i in range(depth):
        j = i // 2
        if i % 2 == 0:
            a = _retention_mixer(x2d, b, s, attn_norm[i], ret_w_in[j], ret_out_norm[j],
                                 c128, s128)
            w_out = ret_w_out[j]
        else:
            a = _sparse_attention_mixer(x2d, b, s, attn_norm[i], dsa_w_in[j], dsa_q_norm[j],
                                        dsa_k_norm[j], dsa_kidx_norm[j], (c64, s64, c32, s32))
            w_out = dsa_w_out[j]
        x2d = _mixer_out_and_mlp(a, x2d, w_out, mlp_norm[i], mlp_w_up[i], mlp_w_down[i])
    return x2d.reshape(b, s, d)
```

```python
import functools

import jax
import jax.numpy as jnp
from jax import lax
from jax.experimental import pallas as pl
from jax.experimental.pallas import tpu as pltpu

F32 = jnp.float32
BF16 = jnp.bfloat16

EPS = 1e-6
ROPE_THETA = 10000.0
RET_HEADS = 4
RET_CHUNK = 128
ATT_HEADS = 8
ATT_KV_HEADS = 2
IDX_HEADS = 8
IDX_DH = 64
TOPK_MAX = 256
LANES = 128
SUBLANES = 8

ROW_TILE = 512
Q_TILE = 256
KEY_BLOCK = 128
SWEEP_UNROLL = 4
VMEM_LIMIT = 56 * 1024 * 1024
BISECT_STEPS = 13


def _params(*sem):
    return pltpu.CompilerParams(dimension_semantics=sem, vmem_limit_bytes=VMEM_LIMIT)


def _resident(shape):
    nd = len(shape)
    return pl.BlockSpec(shape, lambda *_: (0,) * nd, pipeline_mode=pl.Buffered(1))


def _rms(x, gain):
    return x * lax.rsqrt(jnp.mean(x * x, axis=-1, keepdims=True) + EPS) * gain


def _dot(a, b):
    return jnp.dot(a, b, preferred_element_type=F32)


def _dot_nt(a, b):
    return lax.dot_general(a, b, (((1,), (1,)), ((), ())), preferred_element_type=F32)


def _dot_tn(a, b):
    return lax.dot_general(a, b, (((0,), (0,)), ((), ())), preferred_element_type=F32)


def _rope_table_kernel(pos_ref, inv_ref, sign_ref, c128_ref, s128_ref, c64_ref, s64_ref,
                       c32_ref, s32_ref):
    pos = pos_ref[...].astype(F32)
    for row, (c_ref, s_ref) in enumerate(((c128_ref, s128_ref), (c64_ref, s64_ref),
                                          (c32_ref, s32_ref))):
        ang = pos * inv_ref[row:row + 1, :]
        c_ref[...] = jnp.cos(ang)
        s_ref[...] = jnp.sin(ang) * sign_ref[row:row + 1, :]


def _rope_tables(positions):
    t = positions.size
    tm = min(ROW_TILE, t)
    inv_rows, sign_rows = [], []
    for half in (128, 64, 32):
        inv = ROPE_THETA ** (-jnp.arange(half, dtype=F32) / half)
        reps = LANES // half
        inv_rows.append(jnp.tile(inv, reps))
        if reps == 1:
            sign_rows.append(jnp.ones((LANES,), F32))
        else:
            sgn = jnp.concatenate([-jnp.ones((half,), F32), jnp.ones((half,), F32)])
            sign_rows.append(jnp.tile(sgn, reps // 2))
    inv = jnp.stack(inv_rows)
    sign = jnp.stack(sign_rows)
    tab = jax.ShapeDtypeStruct((t, LANES), F32)
    row = pl.BlockSpec((tm, LANES), lambda i: (i, 0))
    return pl.pallas_call(
        _rope_table_kernel,
        grid=(t // tm,),
        in_specs=[pl.BlockSpec((tm, 1), lambda i: (i, 0)), _resident((3, LANES)),
                  _resident((3, LANES))],
        out_specs=[row] * 6,
        out_shape=[tab] * 6,
        compiler_params=_params("parallel"),
        name="rope_tables",
    )(positions.reshape(t, 1), inv, sign)


def _ret_inproj_kernel(x_ref, gain_ref, w_ref, cos_ref, sin_ref, q_ref, k_ref, v_ref, g_ref,
                       *, heads, dk, dv, kscale):
    hn = _rms(x_ref[...], gain_ref[...]).astype(BF16)
    cos = cos_ref[...]
    sin = sin_ref[...]
    half = dk // 2
    for h in range(heads):
        for off, out_ref, scale in ((0, q_ref, None), (heads * dk, k_ref, kscale)):
            z = _dot(hn, w_ref[:, off + h * dk:off + (h + 1) * dk])
            x1, x2 = z[:, :half], z[:, half:]
            o1 = x1 * cos - x2 * sin
            o2 = x2 * cos + x1 * sin
            if scale is not None:
                o1, o2 = o1 * scale, o2 * scale
            out_ref[:, h * dk:h * dk + half] = o1.astype(BF16)
            out_ref[:, h * dk + half:(h + 1) * dk] = o2.astype(BF16)
    width = heads * dv
    for off, out_ref in ((2 * heads * dk, v_ref), (2 * heads * dk + width, g_ref)):
        for c in range(width // dv):
            out_ref[:, c * dv:(c + 1) * dv] = _dot(
                hn, w_ref[:, off + c * dv:off + (c + 1) * dv]).astype(BF16)


def _ret_core_kernel(q_ref, k_ref, v_ref, g_ref, dm_ref, qd_ref, kd_ref, cd_ref, gain_ref,
                     o_ref, state_ref, *, chunk, nchunks):
    state_ref[...] = jnp.zeros_like(state_ref)
    dm = dm_ref[0]
    qd = qd_ref[0]
    kd = kd_ref[0]
    cd = cd_ref[0]
    gain = gain_ref[0]

    def body(c, carry):
        rows = pl.ds(pl.multiple_of(c * chunk, chunk), chunk)
        q = q_ref[0, rows, :]
        k = k_ref[0, rows, :]
        v = v_ref[0, rows, :]
        scores = _dot_nt(q, k) * dm
        state = state_ref[...]
        o = _dot(scores.astype(BF16), v) + _dot(q, state.astype(BF16)) * qd
        k_dec = (k.astype(F32) * kd).astype(BF16)
        state_ref[...] = state * cd + _dot_tn(k_dec, v)
        y = _rms(o, gain)
        gate = g_ref[0, rows, :].astype(F32)
        o_ref[0, rows, :] = (y * (gate * jax.nn.sigmoid(gate))).astype(BF16)
        return carry

    lax.fori_loop(0, nchunks, body, 0)


def _retention_mixer(x2d, b, s, attn_gain, w_in, out_gain, cos, sin):
    t, d = x2d.shape
    heads = RET_HEADS
    dk = d // heads
    dv = 2 * dk
    chunk = RET_CHUNK
    tm = min(ROW_TILE, t)
    w = w_in.astype(BF16)
    row = lambda width: pl.BlockSpec((tm, width), lambda i: (i, 0))
    q, k, v, g = pl.pallas_call(
        functools.partial(_ret_inproj_kernel, heads=heads, dk=dk, dv=dv, kscale=dk ** -0.5),
        grid=(t // tm,),
        in_specs=[row(d), _resident((1, d)), _resident(w.shape), row(LANES), row(LANES)],
        out_specs=[row(heads * dk), row(heads * dk), row(heads * dv), row(heads * dv)],
        out_shape=[jax.ShapeDtypeStruct((t, heads * dk), BF16),
                   jax.ShapeDtypeStruct((t, heads * dk), BF16),
                   jax.ShapeDtypeStruct((t, heads * dv), BF16),
                   jax.ShapeDtypeStruct((t, heads * dv), BF16)],
        compiler_params=_params("parallel"),
        name="ret_inproj",
    )(x2d, attn_gain.reshape(1, d), w, cos, sin)

    log_gamma = jnp.log1p(-(2.0 ** (-5.0 - jnp.arange(heads, dtype=F32))))
    i = jnp.arange(chunk, dtype=F32)
    diff = i[:, None] - i[None, :]
    dm = jnp.where(diff >= 0, jnp.exp(log_gamma[:, None, None] * jnp.maximum(diff, 0.0)), 0.0)
    qd = jnp.exp(log_gamma[:, None] * (i + 1.0))
    kd = jnp.exp(log_gamma[:, None] * (chunk - 1.0 - i))
    cd = jnp.exp(log_gamma * chunk)
    qd = jnp.broadcast_to(qd[:, :, None], (heads, chunk, dv))
    kd = jnp.broadcast_to(kd[:, :, None], (heads, chunk, dk))
    cd = jnp.broadcast_to(cd[:, None, None], (heads, 1, dv))

    seq = lambda width: pl.BlockSpec((1, s, width), lambda bi, h: (bi, 0, h))
    per_head = lambda r, c: pl.BlockSpec((1, r, c), lambda bi, h: (h, 0, 0))
    o = pl.pallas_call(
        functools.partial(_ret_core_kernel, chunk=chunk, nchunks=s // chunk),
        grid=(b, heads),
        in_specs=[seq(dk), seq(dk), seq(dv), seq(dv), per_head(chunk, chunk),
                  per_head(chunk, dv), per_head(chunk, dk), per_head(1, dv), per_head(1, dv)],
        out_specs=seq(dv),
        out_shape=jax.ShapeDtypeStruct((b, s, heads * dv), BF16),
        scratch_shapes=[pltpu.VMEM((dk, dv), F32)],
        compiler_params=_params("parallel", "parallel"),
        name="ret_core",
    )(q.reshape(b, s, -1), k.reshape(b, s, -1), v.reshape(b, s, -1), g.reshape(b, s, -1),
      dm, qd, kd, cd, out_gain.reshape(heads, 1, dv))
    return o.reshape(t, heads * dv)


def _tail_kernel(a_ref, x_ref, wo_ref, gain_ref, wup_ref, wdn_ref, o_ref, *, ff_chunk):
    x1 = x_ref[...] + _dot(a_ref[...], wo_ref[...])
    hn = _rms(x1, gain_ref[...]).astype(BF16)
    acc = x1
    for c in range(wup_ref.shape[1] // ff_chunk):
        cols = slice(c * ff_chunk, (c + 1) * ff_chunk)
        u = jnp.maximum(_dot(hn, wup_ref[:, cols]), 0.0)
        acc = acc + _dot((u * u).astype(BF16), wdn_ref[cols, :])
    o_ref[...] = acc


def _mixer_out_and_mlp(a2d, x2d, w_out, mlp_gain, w_up, w_down):
    t, d = x2d.shape
    tm = min(ROW_TILE, t)
    wo, wu, wd = w_out.astype(BF16), w_up.astype(BF16), w_down.astype(BF16)
    row = lambda width: pl.BlockSpec((tm, width), lambda i: (i, 0))
    return pl.pallas_call(
        functools.partial(_tail_kernel, ff_chunk=1024),
        grid=(t // tm,),
        in_specs=[row(a2d.shape[1]), row(d), _resident(wo.shape), _resident((1, d)),
                  _resident(wu.shape), _resident(wd.shape)],
        out_specs=row(d),
        out_shape=jax.ShapeDtypeStruct((t, d), F32),
        compiler_params=_params("parallel"),
        name="outproj_mlp",
    )(a2d, x2d, wo, mlp_gain.reshape(1, d), wu, wd)


def _rope_lanes(z, cos, sin_signed, half):
    if 2 * half == LANES:
        partner = pltpu.roll(z, half, 1)
    else:
        lane = lax.broadcasted_iota(jnp.int32, z.shape, 1)
        first = (lane % (2 * half)) < half
        partner = jnp.where(first, pltpu.roll(z, LANES - half, 1), pltpu.roll(z, half, 1))
    return z * cos + partner * sin_signed


def _dsa_inproj_kernel(x_ref, gain_ref, wq_ref, wk_ref, wvt_ref, wqi_ref, wki_ref, wwit_ref,
                       qg_ref, kg_ref, kig_ref, c64_ref, s64_ref, c32_ref, s32_ref,
                       q_ref, k_ref, vt_ref, qi_ref, ki_ref, wit_ref,
                       *, heads, kv_heads, dh, idx_heads, di, qscale, wscale):
    hn = _rms(x_ref[...], gain_ref[...]).astype(BF16)
    c64, s64 = c64_ref[...], s64_ref[...]
    c32, s32 = c32_ref[...], s32_ref[...]
    for h in range(heads):
        z = _rms(_dot(hn, wq_ref[:, h * dh:(h + 1) * dh]), qg_ref[...])
        q_ref[:, h * dh:(h + 1) * dh] = (_rope_lanes(z, c64, s64, dh // 2) * qscale).astype(BF16)
    for h in range(kv_heads):
        z = _rms(_dot(hn, wk_ref[:, h * dh:(h + 1) * dh]), kg_ref[...])
        k_ref[:, h * dh:(h + 1) * dh] = _rope_lanes(z, c64, s64, dh // 2).astype(BF16)
    vt_ref[0] = _dot_nt(wvt_ref[...], hn).astype(BF16)
    per_tile = LANES // di
    for c in range(idx_heads // per_tile):
        z = _rope_lanes(_dot(hn, wqi_ref[:, c * LANES:(c + 1) * LANES]), c32, s32, di // 2)
        for r in range(per_tile):
            qi_ref[c * per_tile + r] = z[:, r * di:(r + 1) * di].astype(BF16)
    z = _dot(hn, wki_ref[...])
    z = z * lax.rsqrt(jnp.sum(z * z, axis=-1, keepdims=True) * (1.0 / di) + EPS) * kig_ref[...]
    ki_ref[...] = _rope_lanes(z, c32, s32, di // 2)[:, :di].astype(BF16)
    wit_ref[0] = _dot_nt(wwit_ref[...], hn)[:idx_heads, :] * wscale


def _dsa_attn_kernel(q_ref, k_ref, vt_ref, qi_ref, ki_ref, wit_ref, o_ref, sc_ref, lg_ref, e_ref,
                     *, tile, topk, heads, kv_heads, dh, idx_heads, index_steps):
    tq = q_ref.shape[1]
    ke = k_ref.shape[1]
    kb = KEY_BLOCK
    slabs = kb // SUBLANES
    j = tile
    nblk = ke // kb
    neg = float(jnp.finfo(F32).min)
    big = float(jnp.finfo(F32).max)
    diag = (lax.broadcasted_iota(jnp.int32, (tq, tq), 0)
            <= lax.broadcasted_iota(jnp.int32, (tq, tq), 1))

    def rows_of(i):
        return pl.ds(pl.multiple_of(i * kb, kb), kb)

    def full(val):
        return jnp.full((SUBLANES, tq), val, F32)

    def spread(row):
        return jnp.broadcast_to(row, (SUBLANES, tq))

    def sweep(fn, init):
        return lax.fori_loop(0, nblk, lambda i, a: fn(i, sc_ref[rows_of(i), :], a), init,
                             unroll=min(nblk, SWEEP_UNROLL))

    if ke <= topk:
        sc_ref[...] = jnp.where(diag, 0.0, neg)
    else:
        wit = wit_ref[0]
        ki = ki_ref[0]
        score = jnp.zeros((ke, tq), F32)
        for h in range(idx_heads):
            score = score + jnp.maximum(_dot_nt(ki, qi_ref[h, 0]), 0.0) * wit[h:h + 1, :]
        last = score[ke - tq:, :]
        row_max = jnp.max(jnp.where(diag, last, neg), axis=0, keepdims=True)
        row_min = jnp.min(jnp.where(diag, last, big), axis=0, keepdims=True)
        if ke > tq:
            row_max = jnp.maximum(row_max, jnp.max(score[:ke - tq, :], axis=0, keepdims=True))
            row_min = jnp.minimum(row_min, jnp.min(score[:ke - tq, :], axis=0, keepdims=True))
            sc_ref[:ke - tq, :] = score[:ke - tq, :]
        sc_ref[ke - tq:, :] = jnp.where(diag, last, neg)
        q_pos = j * tq + lax.broadcasted_iota(jnp.int32, (1, tq), 1)
        kvec = jnp.minimum(q_pos + 1, topk).astype(F32)

        def count_ge(th):
            thb = spread(th)[None]
            acc = sweep(lambda i, blk, a: a + jnp.sum(
                jnp.where(blk.reshape(slabs, SUBLANES, tq) >= thb, 1.0, 0.0), axis=0),
                jnp.zeros((SUBLANES, tq), F32))
            return jnp.sum(acc, axis=0, keepdims=True)

        def max_below(hi):
            hib = spread(hi)[None]

            def step(i, blk, a):
                blk = blk.reshape(slabs, SUBLANES, tq)
                return jnp.maximum(a, jnp.max(jnp.where(blk < hib, blk, neg), axis=0))

            return jnp.max(sweep(step, full(neg)), axis=0, keepdims=True)

        def bisect(_, carry):
            lo, hi, c_hi = carry
            mid = lo * 0.5 + hi * 0.5
            c = count_ge(mid)
            ge = c >= kvec
            return jnp.where(ge, mid, lo), jnp.where(ge, hi, mid), jnp.where(ge, c_hi, c)

        hi0 = row_max + jnp.abs(row_max) * 1e-3 + 1e-3
        _, hi, c_hi = lax.fori_loop(0, BISECT_STEPS, bisect,
                                    (row_min, hi0, jnp.zeros((1, tq), F32)))

        def snap(hi):
            v = max_below(hi)
            return v, count_ge(v)

        def pending(cv):
            return jnp.max(jnp.where(cv < kvec, 1.0, 0.0))

        def refine(state):
            hi, c_hi, v, cv, _ = state
            short = cv < kvec
            hi = jnp.where(short, v, hi)
            c_hi = jnp.where(short, cv, c_hi)
            v, cv = snap(hi)
            return hi, c_hi, v, cv, pending(cv)

        v, cv = snap(hi)
        _, c_hi, v, cv, _ = lax.while_loop(lambda st: st[4] > 0.0, refine,
                                           (hi, c_hi, v, cv, pending(cv)))
        need = kvec - c_hi
        surplus = jnp.max(jnp.where(cv > kvec, 1.0, 0.0)) > 0.0
        vb = spread(v)

        @pl.when(jnp.logical_not(surplus))
        def _():
            def to_bias(i, carry):
                blk = sc_ref[rows_of(i), :].reshape(slabs, SUBLANES, tq)
                sc_ref[rows_of(i), :] = jnp.where(blk >= vb[None], 0.0, neg).reshape(kb, tq)
                return carry
            lax.fori_loop(0, nblk, to_bias, 0)

        @pl.when(surplus)
        def _():
            def key_idx(i):
                return (lax.broadcasted_iota(jnp.int32, (slabs, SUBLANES, tq), 0) * SUBLANES
                        + lax.broadcasted_iota(jnp.int32, (slabs, SUBLANES, tq), 1) + i * kb)

            def idx_step(_, carry):
                lo_i, hi_i = carry
                mid_i = (lo_i + hi_i) >> 1
                midb = jnp.broadcast_to(mid_i, (SUBLANES, tq))
                acc = sweep(lambda i, blk, a: a + jnp.sum(jnp.where(
                    (blk.reshape(slabs, SUBLANES, tq) == vb[None]) & (key_idx(i) <= midb[None]),
                    1.0, 0.0), axis=0), jnp.zeros((SUBLANES, tq), F32))
                ok = jnp.sum(acc, axis=0, keepdims=True) >= need
                return jnp.where(ok, lo_i, mid_i), jnp.where(ok, mid_i, hi_i)

            _, istar = lax.fori_loop(0, index_steps, idx_step,
                                     (jnp.full((1, tq), -1, jnp.int32),
                                      jnp.full((1, tq), nblk * kb - 1, jnp.int32)))
            istarb = jnp.broadcast_to(istar, (SUBLANES, tq))

            def to_bias(i, carry):
                blk = sc_ref[rows_of(i), :].reshape(slabs, SUBLANES, tq)
                keep = (blk > vb[None]) | ((blk == vb[None]) & (key_idx(i) <= istarb[None]))
                sc_ref[rows_of(i), :] = jnp.where(keep, 0.0, neg).reshape(kb, tq)
                return carry
            lax.fori_loop(0, nblk, to_bias, 0)

    group = heads // kv_heads
    unroll = min(nblk, 2)

    def fold(x, op):
        return op(x.reshape(slabs, SUBLANES, tq), axis=0)

    def logits_pass(i, maxima):
        rows = rows_of(i)
        bias = sc_ref[rows, :]
        new = []
        for h in range(heads):
            g = h // group
            lg = _dot_nt(k_ref[0, rows, g * dh:(g + 1) * dh],
                         q_ref[0, :, h * dh:(h + 1) * dh]) + bias
            lg_ref[h, rows, :] = lg
            new.append(jnp.maximum(maxima[h], fold(lg, jnp.max)))
        return tuple(new)

    maxima = lax.fori_loop(0, nblk, logits_pass, (full(neg),) * heads, unroll=unroll)
    maxima = [jnp.max(m8, axis=0, keepdims=True) for m8 in maxima]

    def exp_pass(i, sums):
        rows = rows_of(i)
        new = []
        for h in range(heads):
            e = jnp.exp(lg_ref[h, rows, :] - maxima[h])
            e_ref[h, rows, :] = e.astype(BF16)
            new.append(sums[h] + fold(e, jnp.sum))
        return tuple(new)

    sums = lax.fori_loop(0, nblk, exp_pass, (jnp.zeros((SUBLANES, tq), F32),) * heads,
                         unroll=unroll)

    for h in range(heads):
        g = h // group
        denom = jnp.sum(sums[h], axis=0, keepdims=True)
        out_t = _dot(vt_ref[0, g * dh:(g + 1) * dh, :], e_ref[h])
        o_ref[0, :, h * dh:(h + 1) * dh] = (out_t / denom).T.astype(BF16)


def _sparse_attention_mixer(x2d, b, s, attn_gain, w_in, q_gain, k_gain, kidx_gain, tabs):
    t, d = x2d.shape
    heads, kv_heads, idx_heads, di = ATT_HEADS, ATT_KV_HEADS, IDX_HEADS, IDX_DH
    dh = d // heads
    c64, s64, c32, s32 = tabs
    tm = min(ROW_TILE, s)
    per_b = s // tm
    kb = KEY_BLOCK
    nq, nkv, nqi = heads * dh, kv_heads * dh, idx_heads * di
    pad_cols = lambda a, n: jnp.pad(a, ((0, 0), (0, n - a.shape[-1])))
    wq = w_in[:, :nq].astype(BF16)
    wk = w_in[:, nq:nq + nkv].astype(BF16)
    wvt = w_in[:, nq + nkv:nq + 2 * nkv].T.astype(BF16)
    off = nq + 2 * nkv
    wqi = w_in[:, off:off + nqi].astype(BF16)
    off += nqi
    wki = pad_cols(w_in[:, off:off + di], LANES).astype(BF16)
    wwit = jnp.pad(w_in[:, off + di:off + di + idx_heads].T,
                   ((0, 2 * SUBLANES - idx_heads), (0, 0))).astype(BF16)
    kig = pad_cols(kidx_gain.reshape(1, di), LANES)
    row = lambda width: pl.BlockSpec((tm, width), lambda i: (i, 0))
    shp = lambda width, dt: jax.ShapeDtypeStruct((t, width), dt)
    q, k, vt, qi, ki, wit = pl.pallas_call(
        functools.partial(_dsa_inproj_kernel, heads=heads, kv_heads=kv_heads, dh=dh,
                          idx_heads=idx_heads, di=di, qscale=dh ** -0.5,
                          wscale=idx_heads ** -0.5 * di ** -0.5),
        grid=(t // tm,),
        in_specs=[row(d), _resident((1, d)), _resident(wq.shape), _resident(wk.shape),
                  _resident(wvt.shape), _resident(wqi.shape), _resident(wki.shape),
                  _resident(wwit.shape), _resident((1, dh)), _resident((1, dh)),
                  _resident((1, LANES)), row(LANES), row(LANES), row(LANES), row(LANES)],
        out_specs=[row(nq), row(nkv),
                   pl.BlockSpec((1, nkv, tm), lambda i: (i // per_b, 0, i % per_b)),
                   pl.BlockSpec((idx_heads, tm, di), lambda i: (0, i, 0)),
                   row(di),
                   pl.BlockSpec((1, idx_heads, tm), lambda i: (i // per_b, 0, i % per_b))],
        out_shape=[shp(nq, BF16), shp(nkv, BF16),
                   jax.ShapeDtypeStruct((b, nkv, s), BF16),
                   jax.ShapeDtypeStruct((idx_heads, t, di), BF16),
                   shp(di, BF16),
                   jax.ShapeDtypeStruct((b, idx_heads, s), F32)],
        compiler_params=_params("parallel"),
        name="dsa_inproj",
    )(x2d, attn_gain.reshape(1, d), wq, wk, wvt, wqi, wki, wwit, q_gain.reshape(1, dh),
      k_gain.reshape(1, dh), kig, c64, s64, c32, s32)

    topk = min(TOPK_MAX, s // 4)
    tq = min(Q_TILE, s)
    q, k, qi, ki = (q.reshape(b, s, nq), k.reshape(b, s, nkv), qi.reshape(idx_heads, b, s, di),
                    ki.reshape(b, s, di))
    outs = []
    for tile in range(s // tq):
        ke = (tile + 1) * tq
        outs.append(pl.pallas_call(
            functools.partial(_dsa_attn_kernel, tile=tile, topk=topk, heads=heads,
                              kv_heads=kv_heads, dh=dh, idx_heads=idx_heads,
                              index_steps=(ke - 1).bit_length() + 1),
            grid=(b,),
            in_specs=[pl.BlockSpec((1, tq, nq), lambda bi, tile=tile: (bi, tile, 0)),
                      pl.BlockSpec((1, ke, nkv), lambda bi: (bi, 0, 0)),
                      pl.BlockSpec((1, nkv, ke), lambda bi: (bi, 0, 0)),
                      pl.BlockSpec((idx_heads, 1, tq, di), lambda bi, tile=tile: (0, bi, tile, 0)),
                      pl.BlockSpec((1, ke, di), lambda bi: (bi, 0, 0)),
                      pl.BlockSpec((1, idx_heads, tq), lambda bi, tile=tile: (bi, 0, tile))],
            out_specs=pl.BlockSpec((1, tq, nq), lambda bi: (bi, 0, 0)),
            out_shape=jax.ShapeDtypeStruct((b, tq, nq), BF16),
            scratch_shapes=[pltpu.VMEM((ke, tq), F32), pltpu.VMEM((heads, ke, tq), F32),
                            pltpu.VMEM((heads, ke, tq), BF16)],
            compiler_params=_params("parallel"),
            name=f"dsa_attn_{tile}",
        )(q, k, vt, qi, ki, wit))
    return jnp.concatenate(outs, axis=1).reshape(t, nq)


def kernel(x, positions, attn_norm, ret_w_in, ret_out_norm, ret_w_out, dsa_w_in, dsa_q_norm,
           dsa_k_norm, dsa_kidx_norm, dsa_w_out, mlp_norm, mlp_w_up, mlp_w_down):
    b, s, d = x.shape
    depth = attn_norm.shape[0]
    c128, s128, c64, s64, c32, s32 = _rope_tables(positions)
    x2d = x.reshape(b * s, d)
    for i in range(depth):
        j = i // 2
        if i % 2 == 0:
            a = _retention_mixer(x2d, b, s, attn_norm[i], ret_w_in[j], ret_out_norm[j],
                                 c128, s128)
            w_out = ret_w_out[j]
        else:
            a = _sparse_attention_mixer(x2d, b, s, attn_norm[i], dsa_w_in[j], dsa_q_norm[j],
                                        dsa_k_norm[j], dsa_kidx_norm[j], (c64, s64, c32, s32))
            w_out = dsa_w_out[j]
        x2d = _mixer_out_and_mlp(a, x2d, w_out, mlp_norm[i], mlp_w_up[i], mlp_w_down[i])
    return x2d.reshape(b, s, d)
```

```python
import functools

import jax
import jax.numpy as jnp
from jax import lax
from jax.experimental import pallas as pl
from jax.experimental.pallas import tpu as pltpu

F32 = jnp.float32
BF16 = jnp.bfloat16

EPS = 1e-6
ROPE_THETA = 10000.0
RET_HEADS = 4
RET_CHUNK = 128
ATT_HEADS = 8
ATT_KV_HEADS = 2
IDX_HEADS = 8
IDX_DH = 64
TOPK_MAX = 256
LANES = 128
SUBLANES = 8

ROW_TILE = 512
RET_ROW_BLOCK = 512
Q_TILE = 256
KEY_BLOCK = 128
SWEEP_UNROLL = 4
VMEM_LIMIT = 56 * 1024 * 1024
BISECT_STEPS = 13


def _params(*sem):
    return pltpu.CompilerParams(dimension_semantics=sem, vmem_limit_bytes=VMEM_LIMIT)


def _resident(shape):
    nd = len(shape)
    return pl.BlockSpec(shape, lambda *_: (0,) * nd, pipeline_mode=pl.Buffered(1))


def _rms(x, gain):
    return x * lax.rsqrt(jnp.mean(x * x, axis=-1, keepdims=True) + EPS) * gain


def _dot(a, b):
    return jnp.dot(a, b, preferred_element_type=F32)


def _dot_nt(a, b):
    return lax.dot_general(a, b, (((1,), (1,)), ((), ())), preferred_element_type=F32)


def _dot_tn(a, b):
    return lax.dot_general(a, b, (((0,), (0,)), ((), ())), preferred_element_type=F32)


def _rope_table_kernel(pos_ref, inv_ref, c128_ref, s128_ref, c64t_ref, s64t_ref, c32t_ref,
                       s32t_ref, ct_ref, st_ref):
    ang_t = inv_ref[...] * pos_ref[...].astype(F32)
    cos_t = jnp.cos(ang_t)
    sin_t = jnp.sin(ang_t)
    c128_ref[...] = cos_t.T
    s128_ref[...] = sin_t.T
    for c in range(ct_ref.shape[0]):
        lanes = slice(c * LANES, (c + 1) * LANES)
        ct_ref[c] = cos_t[:, lanes]
        st_ref[c] = sin_t[:, lanes]
        for step, c_ref, s_ref in ((2, c64t_ref, s64t_ref), (4, c32t_ref, s32t_ref)):
            rows = pl.ds(0, LANES // step, stride=step)
            c_ref[:, lanes] = ct_ref[c, rows, :]
            s_ref[:, lanes] = st_ref[c, rows, :]


def _rope_tables(positions):
    t = positions.size
    tm = min(ROW_TILE, t)
    inv = ROPE_THETA ** (-jnp.arange(LANES, dtype=F32) / LANES)
    row = pl.BlockSpec((tm, LANES), lambda i: (i, 0))
    col = lambda n: pl.BlockSpec((n, tm), lambda i: (0, i))
    tab = lambda n: jax.ShapeDtypeStruct((n, t), F32)
    return pl.pallas_call(
        _rope_table_kernel,
        grid=(t // tm,),
        in_specs=[pl.BlockSpec((1, tm), lambda i: (0, i)), _resident((LANES, 1))],
        out_specs=[row, row, col(64), col(64), col(32), col(32)],
        out_shape=[jax.ShapeDtypeStruct((t, LANES), F32)] * 2 + [tab(64), tab(64), tab(32),
                                                                  tab(32)],
        scratch_shapes=[pltpu.VMEM((tm // LANES, LANES, LANES), F32)] * 2,
        compiler_params=_params("parallel"),
        name="rope_tables",
    )(positions.reshape(1, t), inv.reshape(LANES, 1))


def _ret_inproj_kernel(x_ref, gain_ref, w_ref, cos_ref, sin_ref, q_ref, k_ref, v_ref, g_ref,
                       *, heads, dk, dv, kscale):
    hn = _rms(x_ref[...], gain_ref[...]).astype(BF16)
    cos = cos_ref[...]
    sin = sin_ref[...]
    half = dk // 2
    for h in range(heads):
        for off, out_ref, scale in ((0, q_ref, None), (heads * dk, k_ref, kscale)):
            z = _dot(hn, w_ref[:, off + h * dk:off + (h + 1) * dk])
            x1, x2 = z[:, :half], z[:, half:]
            o1 = x1 * cos - x2 * sin
            o2 = x2 * cos + x1 * sin
            if scale is not None:
                o1, o2 = o1 * scale, o2 * scale
            out_ref[:, h * dk:h * dk + half] = o1.astype(BF16)
            out_ref[:, h * dk + half:(h + 1) * dk] = o2.astype(BF16)
    width = heads * dv
    for off, out_ref in ((2 * heads * dk, v_ref), (2 * heads * dk + width, g_ref)):
        for c in range(width // dv):
            out_ref[:, c * dv:(c + 1) * dv] = _dot(
                hn, w_ref[:, off + c * dv:off + (c + 1) * dv]).astype(BF16)


def _ret_core_kernel(q_ref, k_ref, v_ref, g_ref, dm_ref, qd_ref, kd_ref, cd_ref, gain_ref,
                     o_ref, state_ref, *, heads, chunk):
    @pl.when(pl.program_id(1) == 0)
    def _():
        state_ref[...] = jnp.zeros_like(state_ref)

    dk = q_ref.shape[2] // heads
    dv = v_ref.shape[2] // heads
    for h in range(heads):
        dm = dm_ref[h]
        qd = qd_ref[h]
        kd = kd_ref[h]
        cd = cd_ref[h]
        gain = gain_ref[h]
        state = state_ref[h]
        for c in range(q_ref.shape[1] // chunk):
            rows = slice(c * chunk, (c + 1) * chunk)
            q = q_ref[0, rows, h * dk:(h + 1) * dk]
            k = k_ref[0, rows, h * dk:(h + 1) * dk]
            v = v_ref[0, rows, h * dv:(h + 1) * dv]
            scores = _dot_nt(q, k) * dm
            o = _dot(scores.astype(BF16), v) + _dot(q, state.astype(BF16)) * qd
            k_dec = (k.astype(F32) * kd).astype(BF16)
            state = state * cd + _dot_tn(k_dec, v)
            y = _rms(o, gain)
            gate = g_ref[0, rows, h * dv:(h + 1) * dv].astype(F32)
            o_ref[0, rows, h * dv:(h + 1) * dv] = (y * (gate * jax.nn.sigmoid(gate))).astype(BF16)
        state_ref[h] = state


def _retention_mixer(x2d, b, s, attn_gain, w_in, out_gain, cos, sin):
    t, d = x2d.shape
    heads = RET_HEADS
    dk = d // heads
    dv = 2 * dk
    chunk = RET_CHUNK
    tm = min(ROW_TILE, t)
    w = w_in.astype(BF16)
    row = lambda width: pl.BlockSpec((tm, width), lambda i: (i, 0))
    q, k, v, g = pl.pallas_call(
        functools.partial(_ret_inproj_kernel, heads=heads, dk=dk, dv=dv, kscale=dk ** -0.5),
        grid=(t // tm,),
        in_specs=[row(d), _resident((1, d)), _resident(w.shape), row(LANES), row(LANES)],
        out_specs=[row(heads * dk), row(heads * dk), row(heads * dv), row(heads * dv)],
        out_shape=[jax.ShapeDtypeStruct((t, heads * dk), BF16),
                   jax.ShapeDtypeStruct((t, heads * dk), BF16),
                   jax.ShapeDtypeStruct((t, heads * dv), BF16),
                   jax.ShapeDtypeStruct((t, heads * dv), BF16)],
        compiler_params=_params("parallel"),
        name="ret_inproj",
    )(x2d, attn_gain.reshape(1, d), w, cos, sin)

    log_gamma = jnp.log1p(-(2.0 ** (-5.0 - jnp.arange(heads, dtype=F32))))
    i = jnp.arange(chunk, dtype=F32)
    diff = i[:, None] - i[None, :]
    dm = jnp.where(diff >= 0, jnp.exp(log_gamma[:, None, None] * jnp.maximum(diff, 0.0)), 0.0)
    qd = jnp.exp(log_gamma[:, None] * (i + 1.0))
    kd = jnp.exp(log_gamma[:, None] * (chunk - 1.0 - i))
    cd = jnp.exp(log_gamma * chunk)
    qd = jnp.broadcast_to(qd[:, :, None], (heads, chunk, dv))
    kd = jnp.broadcast_to(kd[:, :, None], (heads, chunk, dk))
    cd = jnp.broadcast_to(cd[:, None, None], (heads, 1, dv))

    rb = min(RET_ROW_BLOCK, s)
    seq = lambda width: pl.BlockSpec((1, rb, width), lambda bi, r: (bi, r, 0))
    o = pl.pallas_call(
        functools.partial(_ret_core_kernel, heads=heads, chunk=chunk),
        grid=(b, s // rb),
        in_specs=[seq(heads * dk), seq(heads * dk), seq(heads * dv), seq(heads * dv),
                  _resident(dm.shape), _resident(qd.shape), _resident(kd.shape),
                  _resident(cd.shape), _resident((heads, 1, dv))],
        out_specs=seq(heads * dv),
        out_shape=jax.ShapeDtypeStruct((b, s, heads * dv), BF16),
        scratch_shapes=[pltpu.VMEM((heads, dk, dv), F32)],
        compiler_params=_params("parallel", "arbitrary"),
        name="ret_core",
    )(q.reshape(b, s, -1), k.reshape(b, s, -1), v.reshape(b, s, -1), g.reshape(b, s, -1),
      dm, qd, kd, cd, out_gain.reshape(heads, 1, dv))
    return o.reshape(t, heads * dv)


def _tail_kernel(a_ref, x_ref, wo_ref, gain_ref, wup_ref, wdn_ref, o_ref, *, ff_chunk):
    x1 = x_ref[...] + _dot(a_ref[...], wo_ref[...])
    hn = _rms(x1, gain_ref[...]).astype(BF16)
    acc = x1
    for c in range(wup_ref.shape[1] // ff_chunk):
        cols = slice(c * ff_chunk, (c + 1) * ff_chunk)
        u = jnp.maximum(_dot(hn, wup_ref[:, cols]), 0.0)
        acc = acc + _dot((u * u).astype(BF16), wdn_ref[cols, :])
    o_ref[...] = acc


def _mixer_out_and_mlp(a2d, x2d, w_out, mlp_gain, w_up, w_down):
    t, d = x2d.shape
    tm = min(ROW_TILE, t)
    wo, wu, wd = w_out.astype(BF16), w_up.astype(BF16), w_down.astype(BF16)
    row = lambda width: pl.BlockSpec((tm, width), lambda i: (i, 0))
    return pl.pallas_call(
        functools.partial(_tail_kernel, ff_chunk=1024),
        grid=(t // tm,),
        in_specs=[row(a2d.shape[1]), row(d), _resident(wo.shape), _resident((1, d)),
                  _resident(wu.shape), _resident(wd.shape)],
        out_specs=row(d),
        out_shape=jax.ShapeDtypeStruct((t, d), F32),
        compiler_params=_params("parallel"),
        name="outproj_mlp",
    )(a2d, x2d, wo, mlp_gain.reshape(1, d), wu, wd)


def _norm_rope_rows(z, gain, cos, sin):
    if gain is not None:
        z = z * lax.rsqrt(jnp.mean(z * z, axis=0, keepdims=True) + EPS) * gain
    half = z.shape[0] // 2
    x1, x2 = z[:half], z[half:]
    return x1 * cos - x2 * sin, x2 * cos + x1 * sin


def _dsa_inproj_kernel(x_ref, gain_ref, wt_ref, qg_ref, kg_ref, kig_ref, c64_ref, s64_ref,
                       c32_ref, s32_ref, qt_ref, k_ref, vt_ref, qit_ref, ki_ref, wit_ref,
                       *, heads, kv_heads, dh, idx_heads, di, wscale):
    hn = _rms(x_ref[...], gain_ref[...]).astype(BF16)
    tm = hn.shape[0]
    proj = _dot_nt(wt_ref[...], hn)
    c64, s64 = c64_ref[...], s64_ref[...]
    c32, s32 = c32_ref[...], s32_ref[...]
    qg = jnp.broadcast_to(qg_ref[...], (dh, tm))
    kg = jnp.broadcast_to(kg_ref[...], (dh, tm))
    for h in range(heads):
        o1, o2 = _norm_rope_rows(proj[h * dh:(h + 1) * dh], qg, c64, s64)
        qt_ref[0, h * dh:h * dh + dh // 2, :] = o1.astype(BF16)
        qt_ref[0, h * dh + dh // 2:(h + 1) * dh, :] = o2.astype(BF16)
    off = heads * dh
    for h in range(kv_heads):
        o1, o2 = _norm_rope_rows(proj[off + h * dh:off + (h + 1) * dh], kg, c64, s64)
        k_ref[:, h * dh:(h + 1) * dh] = jnp.concatenate([o1, o2], axis=0).T.astype(BF16)
    off += kv_heads * dh
    vt_ref[0] = proj[off:off + kv_heads * dh].astype(BF16)
    off += kv_heads * dh
    for h in range(idx_heads):
        o1, o2 = _norm_rope_rows(proj[off + h * di:off + (h + 1) * di], None, c32, s32)
        qit_ref[0, h * di:h * di + di // 2, :] = o1.astype(BF16)
        qit_ref[0, h * di + di // 2:(h + 1) * di, :] = o2.astype(BF16)
    off += idx_heads * di
    kig = jnp.broadcast_to(kig_ref[...], (di, tm))
    o1, o2 = _norm_rope_rows(proj[off:off + di], kig, c32, s32)
    ki_t = jnp.concatenate([o1, o2, proj[off + di:off + LANES]], axis=0)
    ki_ref[...] = ki_t.T[:, :di].astype(BF16)
    off += LANES
    wit_ref[0] = proj[off:off + idx_heads] * wscale


def _dsa_attn_kernel(qt_ref, k_ref, vt_ref, qit_ref, ki_ref, wit_ref, o_ref, sc_ref, lg_ref,
                     e_ref, *, tile, topk, heads, kv_heads, dh, idx_heads, index_steps):
    tq = qt_ref.shape[2]
    ke = k_ref.shape[1]
    di = qit_ref.shape[1] // idx_heads
    kb = KEY_BLOCK
    slabs = kb // SUBLANES
    j = tile
    nblk = ke // kb
    neg = float(jnp.finfo(F32).min)
    big = float(jnp.finfo(F32).max)
    diag = (lax.broadcasted_iota(jnp.int32, (tq, tq), 0)
            <= lax.broadcasted_iota(jnp.int32, (tq, tq), 1))

    def rows_of(i):
        return pl.ds(pl.multiple_of(i * kb, kb), kb)

    def full(val):
        return jnp.full((SUBLANES, tq), val, F32)

    def spread(row):
        return jnp.broadcast_to(row, (SUBLANES, tq))

    def sweep(fn, init):
        return lax.fori_loop(0, nblk, lambda i, a: fn(i, sc_ref[rows_of(i), :], a), init,
                             unroll=min(nblk, SWEEP_UNROLL))

    if ke <= topk:
        sc_ref[...] = jnp.where(diag, 0.0, neg)
    else:
        wit = wit_ref[0]
        ki = ki_ref[0]
        score = jnp.zeros((ke, tq), F32)
        for h in range(idx_heads):
            rel = jnp.maximum(_dot(ki, qit_ref[0, h * di:(h + 1) * di, :]), 0.0)
            score = score + rel * wit[h:h + 1, :]
        last = score[ke - tq:, :]
        row_max = jnp.max(jnp.where(diag, last, neg), axis=0, keepdims=True)
        row_min = jnp.min(jnp.where(diag, last, big), axis=0, keepdims=True)
        if ke > tq:
            row_max = jnp.maximum(row_max, jnp.max(score[:ke - tq, :], axis=0, keepdims=True))
            row_min = jnp.minimum(row_min, jnp.min(score[:ke - tq, :], axis=0, keepdims=True))
            sc_ref[:ke - tq, :] = score[:ke - tq, :]
        sc_ref[ke - tq:, :] = jnp.where(diag, last, neg)
        q_pos = j * tq + lax.broadcasted_iota(jnp.int32, (1, tq), 1)
        kvec = jnp.minimum(q_pos + 1, topk).astype(F32)

        def count_ge(th):
            thb = spread(th)[None]
            acc = sweep(lambda i, blk, a: a + jnp.sum(
                jnp.where(blk.reshape(slabs, SUBLANES, tq) >= thb, 1.0, 0.0), axis=0),
                jnp.zeros((SUBLANES, tq), F32))
            return jnp.sum(acc, axis=0, keepdims=True)

        def max_below(hi):
            hib = spread(hi)[None]

            def step(i, blk, a):
                blk = blk.reshape(slabs, SUBLANES, tq)
                return jnp.maximum(a, jnp.max(jnp.where(blk < hib, blk, neg), axis=0))

            return jnp.max(sweep(step, full(neg)), axis=0, keepdims=True)

        def bisect(_, carry):
            lo, hi, c_hi = carry
            mid = lo * 0.5 + hi * 0.5
            c = count_ge(mid)
            ge = c >= kvec
            return jnp.where(ge, mid, lo), jnp.where(ge, hi, mid), jnp.where(ge, c_hi, c)

        hi0 = row_max + jnp.abs(row_max) * 1e-3 + 1e-3
        _, hi, c_hi = lax.fori_loop(0, BISECT_STEPS, bisect,
                                    (row_min, hi0, jnp.zeros((1, tq), F32)))

        def snap(hi):
            v = max_below(hi)
            return v, count_ge(v)

        def pending(cv):
            return jnp.max(jnp.where(cv < kvec, 1.0, 0.0))

        def refine(state):
            hi, c_hi, v, cv, _ = state
            short = cv < kvec
            hi = jnp.where(short, v, hi)
            c_hi = jnp.where(short, cv, c_hi)
            v, cv = snap(hi)
            return hi, c_hi, v, cv, pending(cv)

        v, cv = snap(hi)
        _, c_hi, v, cv, _ = lax.while_loop(lambda st: st[4] > 0.0, refine,
                                           (hi, c_hi, v, cv, pending(cv)))
        need = kvec - c_hi
        surplus = jnp.max(jnp.where(cv > kvec, 1.0, 0.0)) > 0.0
        vb = spread(v)

        @pl.when(jnp.logical_not(surplus))
        def _():
            def to_bias(i, carry):
                blk = sc_ref[rows_of(i), :].reshape(slabs, SUBLANES, tq)
                sc_ref[rows_of(i), :] = jnp.where(blk >= vb[None], 0.0, neg).reshape(kb, tq)
                return carry
            lax.fori_loop(0, nblk, to_bias, 0)

        @pl.when(surplus)
        def _():
            def key_idx(i):
                return (lax.broadcasted_iota(jnp.int32, (slabs, SUBLANES, tq), 0) * SUBLANES
                        + lax.broadcasted_iota(jnp.int32, (slabs, SUBLANES, tq), 1) + i * kb)

            def idx_step(_, carry):
                lo_i, hi_i = carry
                mid_i = (lo_i + hi_i) >> 1
                midb = jnp.broadcast_to(mid_i, (SUBLANES, tq))
                acc = sweep(lambda i, blk, a: a + jnp.sum(jnp.where(
                    (blk.reshape(slabs, SUBLANES, tq) == vb[None]) & (key_idx(i) <= midb[None]),
                    1.0, 0.0), axis=0), jnp.zeros((SUBLANES, tq), F32))
                ok = jnp.sum(acc, axis=0, keepdims=True) >= need
                return jnp.where(ok, lo_i, mid_i), jnp.where(ok, mid_i, hi_i)

            _, istar = lax.fori_loop(0, index_steps, idx_step,
                                     (jnp.full((1, tq), -1, jnp.int32),
                                      jnp.full((1, tq), nblk * kb - 1, jnp.int32)))
            istarb = jnp.broadcast_to(istar, (SUBLANES, tq))

            def to_bias(i, carry):
                blk = sc_ref[rows_of(i), :].reshape(slabs, SUBLANES, tq)
                keep = (blk > vb[None]) | ((blk == vb[None]) & (key_idx(i) <= istarb[None]))
                sc_ref[rows_of(i), :] = jnp.where(keep, 0.0, neg).reshape(kb, tq)
                return carry
            lax.fori_loop(0, nblk, to_bias, 0)

    group = heads // kv_heads
    unroll = min(nblk, 2)

    def fold(x, op):
        return op(x.reshape(slabs, SUBLANES, tq), axis=0)

    def logits_pass(i, maxima):
        rows = rows_of(i)
        bias = sc_ref[rows, :]
        new = []
        for h in range(heads):
            g = h // group
            lg = _dot(k_ref[0, rows, g * dh:(g + 1) * dh],
                      qt_ref[0, h * dh:(h + 1) * dh, :]) + bias
            lg_ref[h, rows, :] = lg
            new.append(jnp.maximum(maxima[h], fold(lg, jnp.max)))
        return tuple(new)

    maxima = lax.fori_loop(0, nblk, logits_pass, (full(neg),) * heads, unroll=unroll)
    maxima = [jnp.max(m8, axis=0, keepdims=True) for m8 in maxima]

    def exp_pass(i, sums):
        rows = rows_of(i)
        new = []
        for h in range(heads):
            e = jnp.exp(lg_ref[h, rows, :] - maxima[h])
            e_ref[h, rows, :] = e.astype(BF16)
            new.append(sums[h] + fold(e, jnp.sum))
        return tuple(new)

    sums = lax.fori_loop(0, nblk, exp_pass, (jnp.zeros((SUBLANES, tq), F32),) * heads,
                         unroll=unroll)

    for h in range(heads):
        g = h // group
        denom = jnp.sum(sums[h], axis=0, keepdims=True)
        out_t = _dot(vt_ref[0, g * dh:(g + 1) * dh, :], e_ref[h])
        o_ref[0, :, h * dh:(h + 1) * dh] = (out_t / denom).T.astype(BF16)


def _sparse_attention_mixer(x2d, b, s, attn_gain, w_in, q_gain, k_gain, kidx_gain, tabs):
    t, d = x2d.shape
    heads, kv_heads, idx_heads, di = ATT_HEADS, ATT_KV_HEADS, IDX_HEADS, IDX_DH
    dh = d // heads
    c64, s64, c32, s32 = tabs
    tm = min(ROW_TILE, s)
    per_b = s // tm
    nq, nkv, nqi = heads * dh, kv_heads * dh, idx_heads * di
    n_main = nq + 2 * nkv + nqi + di
    zeros = lambda n: jnp.zeros((d, n), w_in.dtype)
    wt = jnp.concatenate([w_in[:, :n_main], zeros(LANES - di), w_in[:, n_main:n_main + idx_heads],
                          zeros(2 * SUBLANES - idx_heads)], axis=1).T.astype(BF16)
    qg = (q_gain * dh ** -0.5).reshape(dh, 1)
    row = lambda width: pl.BlockSpec((tm, width), lambda i: (i, 0))
    col = lambda n: pl.BlockSpec((n, tm), lambda i: (0, i))
    seq_t = lambda n: pl.BlockSpec((1, n, tm), lambda i: (i // per_b, 0, i % per_b))
    shp = lambda width, dt: jax.ShapeDtypeStruct((t, width), dt)
    shp_t = lambda n, dt: jax.ShapeDtypeStruct((b, n, s), dt)
    qt, k, vt, qit, ki, wit = pl.pallas_call(
        functools.partial(_dsa_inproj_kernel, heads=heads, kv_heads=kv_heads, dh=dh,
                          idx_heads=idx_heads, di=di, wscale=idx_heads ** -0.5 * di ** -0.5),
        grid=(t // tm,),
        in_specs=[row(d), _resident((1, d)), _resident(wt.shape), _resident((dh, 1)),
                  _resident((dh, 1)), _resident((di, 1)), col(dh // 2), col(dh // 2),
                  col(di // 2), col(di // 2)],
        out_specs=[seq_t(nq), row(nkv), seq_t(nkv), seq_t(nqi), row(di), seq_t(idx_heads)],
        out_shape=[shp_t(nq, BF16), shp(nkv, BF16), shp_t(nkv, BF16), shp_t(nqi, BF16),
                   shp(di, BF16), shp_t(idx_heads, F32)],
        compiler_params=_params("parallel"),
        name="dsa_inproj",
    )(x2d, attn_gain.reshape(1, d), wt, qg, k_gain.reshape(dh, 1), kidx_gain.reshape(di, 1),
      c64, s64, c32, s32)

    topk = min(TOPK_MAX, s // 4)
    tq = min(Q_TILE, s)
    k, ki = k.reshape(b, s, nkv), ki.reshape(b, s, di)
    outs = []
    for tile in range(s // tq):
        ke = (tile + 1) * tq
        outs.append(pl.pallas_call(
            functools.partial(_dsa_attn_kernel, tile=tile, topk=topk, heads=heads,
                              kv_heads=kv_heads, dh=dh, idx_heads=idx_heads,
                              index_steps=(ke - 1).bit_length() + 1),
            grid=(b,),
            in_specs=[pl.BlockSpec((1, nq, tq), lambda bi, tile=tile: (bi, 0, tile)),
                      pl.BlockSpec((1, ke, nkv), lambda bi: (bi, 0, 0)),
                      pl.BlockSpec((1, nkv, ke), lambda bi: (bi, 0, 0)),
                      pl.BlockSpec((1, nqi, tq), lambda bi, tile=tile: (bi, 0, tile)),
                      pl.BlockSpec((1, ke, di), lambda bi: (bi, 0, 0)),
                      pl.BlockSpec((1, idx_heads, tq), lambda bi, tile=tile: (bi, 0, tile))],
            out_specs=pl.BlockSpec((1, tq, nq), lambda bi: (bi, 0, 0)),
            out_shape=jax.ShapeDtypeStruct((b, tq, nq), BF16),
            scratch_shapes=[pltpu.VMEM((ke, tq), F32), pltpu.VMEM((heads, ke, tq), F32),
                            pltpu.VMEM((heads, ke, tq), BF16)],
            compiler_params=_params("parallel"),
            name=f"dsa_attn_{tile}",
        )(qt, k, vt, qit, ki, wit))
    return jnp.concatenate(outs, axis=1).reshape(t, nq)


def kernel(x, positions, attn_norm, ret_w_in, ret_out_norm, ret_w_out, dsa_w_in, dsa_q_norm,
           dsa_k_norm, dsa_kidx_norm, dsa_w_out, mlp_norm, mlp_w_up, mlp_w_down):
    b, s, d = x.shape
    depth = attn_norm.shape[0]
    c128, s128, c64, s64, c32, s32 = _rope_tables(positions)
    x2d = x.reshape(b * s, d)
    for i in range(depth):
        j = i // 2
        if i % 2 == 0:
            a = _retention_mixer(x2d, b, s, attn_norm[i], ret_w_in[j], ret_out_norm[j],
                                 c128, s128)
            w_out = ret_w_out[j]
        else:
            a = _sparse_attention_mixer(x2d, b, s, attn_norm[i], dsa_w_in[j], dsa_q_norm[j],
                                        dsa_k_norm[j], dsa_kidx_norm[j], (c64, s64, c32, s32))
            w_out = dsa_w_out[j]
        x2d = _mixer_out_and_mlp(a, x2d, w_out, mlp_norm[i], mlp_w_up[i], mlp_w_down[i])
    return x2d.reshape(b, s, d)
```

```python
import functools

import jax
import jax.numpy as jnp
from jax import lax
from jax.experimental import pallas as pl
from jax.experimental.pallas import tpu as pltpu

F32 = jnp.float32
BF16 = jnp.bfloat16

EPS = 1e-6
ROPE_THETA = 10000.0
RET_HEADS = 4
RET_CHUNK = 128
ATT_HEADS = 8
ATT_KV_HEADS = 2
IDX_HEADS = 8
IDX_DH = 64
TOPK_MAX = 256
LANES = 128
SUBLANES = 8

ROW_TILE = 512
RET_ROW_BLOCK = 512
Q_TILE = 256
KEY_BLOCK = 128
SWEEP_UNROLL = 4
VMEM_LIMIT = 56 * 1024 * 1024
BISECT_STEPS = 13


def _params(*sem):
    return pltpu.CompilerParams(dimension_semantics=sem, vmem_limit_bytes=VMEM_LIMIT)


def _resident(shape):
    nd = len(shape)
    return pl.BlockSpec(shape, lambda *_: (0,) * nd, pipeline_mode=pl.Buffered(1))


def _rms(x, gain):
    return x * lax.rsqrt(jnp.mean(x * x, axis=-1, keepdims=True) + EPS) * gain


def _dot(a, b):
    return jnp.dot(a, b, preferred_element_type=F32)


def _dot_nt(a, b):
    return lax.dot_general(a, b, (((1,), (1,)), ((), ())), preferred_element_type=F32)


def _dot_tn(a, b):
    return lax.dot_general(a, b, (((0,), (0,)), ((), ())), preferred_element_type=F32)


def _rope_table_kernel(pos_ref, inv_ref, c128_ref, s128_ref, c64t_ref, s64t_ref, c32t_ref,
                       s32t_ref, ct_ref, st_ref):
    ang_t = inv_ref[...] * pos_ref[...].astype(F32)
    cos_t = jnp.cos(ang_t)
    sin_t = jnp.sin(ang_t)
    c128_ref[...] = cos_t.T
    s128_ref[...] = sin_t.T
    for c in range(ct_ref.shape[0]):
        lanes = slice(c * LANES, (c + 1) * LANES)
        ct_ref[c] = cos_t[:, lanes]
        st_ref[c] = sin_t[:, lanes]
        for step, c_ref, s_ref in ((2, c64t_ref, s64t_ref), (4, c32t_ref, s32t_ref)):
            rows = pl.ds(0, LANES // step, stride=step)
            c_ref[:, lanes] = ct_ref[c, rows, :]
            s_ref[:, lanes] = st_ref[c, rows, :]


def _rope_tables(positions):
    t = positions.size
    tm = min(ROW_TILE, t)
    inv = ROPE_THETA ** (-jnp.arange(LANES, dtype=F32) / LANES)
    row = pl.BlockSpec((tm, LANES), lambda i: (i, 0))
    col = lambda n: pl.BlockSpec((n, tm), lambda i: (0, i))
    tab = lambda n: jax.ShapeDtypeStruct((n, t), F32)
    return pl.pallas_call(
        _rope_table_kernel,
        grid=(t // tm,),
        in_specs=[pl.BlockSpec((1, tm), lambda i: (0, i)), _resident((LANES, 1))],
        out_specs=[row, row, col(64), col(64), col(32), col(32)],
        out_shape=[jax.ShapeDtypeStruct((t, LANES), F32)] * 2 + [tab(64), tab(64), tab(32),
                                                                  tab(32)],
        scratch_shapes=[pltpu.VMEM((tm // LANES, LANES, LANES), F32)] * 2,
        compiler_params=_params("parallel"),
        name="rope_tables",
    )(positions.reshape(1, t), inv.reshape(LANES, 1))


def _ret_inproj_kernel(x_ref, gain_ref, w_ref, cos_ref, sin_ref, q_ref, k_ref, v_ref, g_ref,
                       *, heads, dk, dv, kscale):
    hn = _rms(x_ref[...], gain_ref[...]).astype(BF16)
    cos = cos_ref[...]
    sin = sin_ref[...]
    half = dk // 2
    for h in range(heads):
        for off, out_ref, scale in ((0, q_ref, None), (heads * dk, k_ref, kscale)):
            z = _dot(hn, w_ref[:, off + h * dk:off + (h + 1) * dk])
            x1, x2 = z[:, :half], z[:, half:]
            o1 = x1 * cos - x2 * sin
            o2 = x2 * cos + x1 * sin
            if scale is not None:
                o1, o2 = o1 * scale, o2 * scale
            out_ref[:, h * dk:h * dk + half] = o1.astype(BF16)
            out_ref[:, h * dk + half:(h + 1) * dk] = o2.astype(BF16)
    width = heads * dv
    for off, out_ref in ((2 * heads * dk, v_ref), (2 * heads * dk + width, g_ref)):
        for c in range(width // dv):
            out_ref[:, c * dv:(c + 1) * dv] = _dot(
                hn, w_ref[:, off + c * dv:off + (c + 1) * dv]).astype(BF16)


def _ret_core_kernel(q_ref, k_ref, v_ref, g_ref, dm_ref, qd_ref, kd_ref, cd_ref, gain_ref,
                     o_ref, state_ref, *, heads, chunk):
    @pl.when(pl.program_id(1) == 0)
    def _():
        state_ref[...] = jnp.zeros_like(state_ref)

    dk = q_ref.shape[2] // heads
    dv = v_ref.shape[2] // heads
    for h in range(heads):
        dm = dm_ref[h]
        qd = qd_ref[h]
        kd = kd_ref[h]
        cd = cd_ref[h]
        gain = gain_ref[h]
        state = state_ref[h]
        for c in range(q_ref.shape[1] // chunk):
            rows = slice(c * chunk, (c + 1) * chunk)
            q = q_ref[0, rows, h * dk:(h + 1) * dk]
            k = k_ref[0, rows, h * dk:(h + 1) * dk]
            v = v_ref[0, rows, h * dv:(h + 1) * dv]
            scores = _dot_nt(q, k) * dm
            o = _dot(scores.astype(BF16), v) + _dot(q, state.astype(BF16)) * qd
            k_dec = (k.astype(F32) * kd).astype(BF16)
            state = state * cd + _dot_tn(k_dec, v)
            y = _rms(o, gain)
            gate = g_ref[0, rows, h * dv:(h + 1) * dv].astype(F32)
            o_ref[0, rows, h * dv:(h + 1) * dv] = (y * (gate * jax.nn.sigmoid(gate))).astype(BF16)
        state_ref[h] = state


def _retention_mixer(x2d, b, s, attn_gain, w_in, out_gain, cos, sin):
    t, d = x2d.shape
    heads = RET_HEADS
    dk = d // heads
    dv = 2 * dk
    chunk = RET_CHUNK
    tm = min(ROW_TILE, t)
    w = w_in.astype(BF16)
    row = lambda width: pl.BlockSpec((tm, width), lambda i: (i, 0))
    q, k, v, g = pl.pallas_call(
        functools.partial(_ret_inproj_kernel, heads=heads, dk=dk, dv=dv, kscale=dk ** -0.5),
        grid=(t // tm,),
        in_specs=[row(d), _resident((1, d)), _resident(w.shape), row(LANES), row(LANES)],
        out_specs=[row(heads * dk), row(heads * dk), row(heads * dv), row(heads * dv)],
        out_shape=[jax.ShapeDtypeStruct((t, heads * dk), BF16),
                   jax.ShapeDtypeStruct((t, heads * dk), BF16),
                   jax.ShapeDtypeStruct((t, heads * dv), BF16),
                   jax.ShapeDtypeStruct((t, heads * dv), BF16)],
        compiler_params=_params("parallel"),
        name="ret_inproj",
    )(x2d, attn_gain.reshape(1, d), w, cos, sin)

    log_gamma = jnp.log1p(-(2.0 ** (-5.0 - jnp.arange(heads, dtype=F32))))
    i = jnp.arange(chunk, dtype=F32)
    diff = i[:, None] - i[None, :]
    dm = jnp.where(diff >= 0, jnp.exp(log_gamma[:, None, None] * jnp.maximum(diff, 0.0)), 0.0)
    qd = jnp.exp(log_gamma[:, None] * (i + 1.0))
    kd = jnp.exp(log_gamma[:, None] * (chunk - 1.0 - i))
    cd = jnp.exp(log_gamma * chunk)
    qd = jnp.broadcast_to(qd[:, :, None], (heads, chunk, dv))
    kd = jnp.broadcast_to(kd[:, :, None], (heads, chunk, dk))
    cd = jnp.broadcast_to(cd[:, None, None], (heads, 1, dv))

    rb = min(RET_ROW_BLOCK, s)
    seq = lambda width: pl.BlockSpec((1, rb, width), lambda bi, r: (bi, r, 0))
    o = pl.pallas_call(
        functools.partial(_ret_core_kernel, heads=heads, chunk=chunk),
        grid=(b, s // rb),
        in_specs=[seq(heads * dk), seq(heads * dk), seq(heads * dv), seq(heads * dv),
                  _resident(dm.shape), _resident(qd.shape), _resident(kd.shape),
                  _resident(cd.shape), _resident((heads, 1, dv))],
        out_specs=seq(heads * dv),
        out_shape=jax.ShapeDtypeStruct((b, s, heads * dv), BF16),
        scratch_shapes=[pltpu.VMEM((heads, dk, dv), F32)],
        compiler_params=_params("parallel", "arbitrary"),
        name="ret_core",
    )(q.reshape(b, s, -1), k.reshape(b, s, -1), v.reshape(b, s, -1), g.reshape(b, s, -1),
      dm, qd, kd, cd, out_gain.reshape(heads, 1, dv))
    return o.reshape(t, heads * dv)


def _tail_kernel(a_ref, x_ref, wo_ref, gain_ref, wup_ref, wdn_ref, o_ref, *, ff_chunk):
    x1 = x_ref[...] + _dot(a_ref[...], wo_ref[...])
    hn = _rms(x1, gain_ref[...]).astype(BF16)
    acc = x1
    for c in range(wup_ref.shape[1] // ff_chunk):
        cols = slice(c * ff_chunk, (c + 1) * ff_chunk)
        u = jnp.maximum(_dot(hn, wup_ref[:, cols]), 0.0)
        acc = acc + _dot((u * u).astype(BF16), wdn_ref[cols, :])
    o_ref[...] = acc


def _mixer_out_and_mlp(a2d, x2d, w_out, mlp_gain, w_up, w_down):
    t, d = x2d.shape
    tm = min(ROW_TILE, t)
    wo, wu, wd = w_out.astype(BF16), w_up.astype(BF16), w_down.astype(BF16)
    row = lambda width: pl.BlockSpec((tm, width), lambda i: (i, 0))
    return pl.pallas_call(
        functools.partial(_tail_kernel, ff_chunk=1024),
        grid=(t // tm,),
        in_specs=[row(a2d.shape[1]), row(d), _resident(wo.shape), _resident((1, d)),
                  _resident(wu.shape), _resident(wd.shape)],
        out_specs=row(d),
        out_shape=jax.ShapeDtypeStruct((t, d), F32),
        compiler_params=_params("parallel"),
        name="outproj_mlp",
    )(a2d, x2d, wo, mlp_gain.reshape(1, d), wu, wd)


def _norm_rope_rows(z, gain, cos, sin):
    if gain is not None:
        z = z * lax.rsqrt(jnp.mean(z * z, axis=0, keepdims=True) + EPS) * gain
    half = z.shape[0] // 2
    x1, x2 = z[:half], z[half:]
    return x1 * cos - x2 * sin, x2 * cos + x1 * sin


def _dsa_inproj_kernel(x_ref, gain_ref, wt_ref, qg_ref, kg_ref, kig_ref, c64_ref, s64_ref,
                       c32_ref, s32_ref, qt_ref, k_ref, vt_ref, qit_ref, ki_ref, wit_ref,
                       *, heads, kv_heads, dh, idx_heads, di, wscale):
    hn = _rms(x_ref[...], gain_ref[...]).astype(BF16)
    tm = hn.shape[0]
    proj = _dot_nt(wt_ref[...], hn)
    c64, s64 = c64_ref[...], s64_ref[...]
    c32, s32 = c32_ref[...], s32_ref[...]
    qg = jnp.broadcast_to(qg_ref[...], (dh, tm))
    kg = jnp.broadcast_to(kg_ref[...], (dh, tm))
    for h in range(heads):
        o1, o2 = _norm_rope_rows(proj[h * dh:(h + 1) * dh], qg, c64, s64)
        qt_ref[0, h * dh:h * dh + dh // 2, :] = o1.astype(BF16)
        qt_ref[0, h * dh + dh // 2:(h + 1) * dh, :] = o2.astype(BF16)
    off = heads * dh
    for h in range(kv_heads):
        o1, o2 = _norm_rope_rows(proj[off + h * dh:off + (h + 1) * dh], kg, c64, s64)
        k_ref[:, h * dh:(h + 1) * dh] = jnp.concatenate([o1, o2], axis=0).T.astype(BF16)
    off += kv_heads * dh
    vt_ref[0] = proj[off:off + kv_heads * dh].astype(BF16)
    off += kv_heads * dh
    for h in range(idx_heads):
        o1, o2 = _norm_rope_rows(proj[off + h * di:off + (h + 1) * di], None, c32, s32)
        qit_ref[0, h * di:h * di + di // 2, :] = o1.astype(BF16)
        qit_ref[0, h * di + di // 2:(h + 1) * di, :] = o2.astype(BF16)
    off += idx_heads * di
    kig = jnp.broadcast_to(kig_ref[...], (di, tm))
    o1, o2 = _norm_rope_rows(proj[off:off + di], kig, c32, s32)
    ki_t = jnp.concatenate([o1, o2, proj[off + di:off + LANES]], axis=0)
    ki_ref[...] = ki_t.T[:, :di].astype(BF16)
    off += LANES
    wit_ref[0] = proj[off:off + idx_heads] * wscale


def _dsa_attn_kernel(qt_ref, k_ref, vt_ref, qit_ref, ki_ref, wit_ref, o_ref, sc_ref, lg_ref,
                     e_ref, *, tile, topk, heads, kv_heads, dh, idx_heads, index_steps):
    tq = qt_ref.shape[2]
    ke = k_ref.shape[1]
    di = qit_ref.shape[1] // idx_heads
    kb = KEY_BLOCK
    slabs = kb // SUBLANES
    j = tile
    nblk = ke // kb
    neg = float(jnp.finfo(F32).min)
    big = float(jnp.finfo(F32).max)
    diag = (lax.broadcasted_iota(jnp.int32, (tq, tq), 0)
            <= lax.broadcasted_iota(jnp.int32, (tq, tq), 1))

    def rows_of(i):
        return pl.ds(pl.multiple_of(i * kb, kb), kb)

    def full(val):
        return jnp.full((SUBLANES, tq), val, F32)

    def spread(row):
        return jnp.broadcast_to(row, (SUBLANES, tq))

    group = heads // kv_heads

    def raw_logits(head, i):
        g = head // group
        lg_ref[head, rows_of(i), :] = _dot(k_ref[0, rows_of(i), g * dh:(g + 1) * dh],
                                           qt_ref[0, head * dh:(head + 1) * dh, :])

    def sweep(fn, init, head=None):
        def body(i, a):
            if head is not None:
                raw_logits(head, i)
            return fn(i, sc_ref[rows_of(i), :], a)
        unroll = nblk if head is not None else min(nblk, SWEEP_UNROLL)
        return lax.fori_loop(0, nblk, body, init, unroll=unroll)

    if ke <= topk:
        sc_ref[...] = jnp.where(diag, 0.0, neg)
        for h in range(heads):
            lax.fori_loop(0, nblk, lambda i, c, h=h: (raw_logits(h, i), c)[1], 0, unroll=True)
    else:
        wit = wit_ref[0]
        ki = ki_ref[0]
        score = jnp.zeros((ke, tq), F32)
        for h in range(idx_heads):
            rel = jnp.maximum(_dot(ki, qit_ref[0, h * di:(h + 1) * di, :]), 0.0)
            score = score + rel * wit[h:h + 1, :]
        last = score[ke - tq:, :]
        row_max = jnp.max(jnp.where(diag, last, neg), axis=0, keepdims=True)
        row_min = jnp.min(jnp.where(diag, last, big), axis=0, keepdims=True)
        if ke > tq:
            row_max = jnp.maximum(row_max, jnp.max(score[:ke - tq, :], axis=0, keepdims=True))
            row_min = jnp.minimum(row_min, jnp.min(score[:ke - tq, :], axis=0, keepdims=True))
            sc_ref[:ke - tq, :] = score[:ke - tq, :]
        sc_ref[ke - tq:, :] = jnp.where(diag, last, neg)
        q_pos = j * tq + lax.broadcasted_iota(jnp.int32, (1, tq), 1)
        kvec = jnp.minimum(q_pos + 1, topk).astype(F32)

        def count_ge(th, head=None):
            thb = spread(th)[None]
            acc = sweep(lambda i, blk, a: a + jnp.sum(
                jnp.where(blk.reshape(slabs, SUBLANES, tq) >= thb, 1.0, 0.0), axis=0),
                jnp.zeros((SUBLANES, tq), F32), head)
            return jnp.sum(acc, axis=0, keepdims=True)

        def max_below(hi):
            hib = spread(hi)[None]

            def step(i, blk, a):
                blk = blk.reshape(slabs, SUBLANES, tq)
                return jnp.maximum(a, jnp.max(jnp.where(blk < hib, blk, neg), axis=0))

            return jnp.max(sweep(step, full(neg)), axis=0, keepdims=True)

        lo, hi, c_hi = row_min, row_max + jnp.abs(row_max) * 1e-3 + 1e-3, jnp.zeros((1, tq), F32)
        for step in range(max(BISECT_STEPS, heads)):
            mid = lo * 0.5 + hi * 0.5
            c = count_ge(mid, step if step < heads else None)
            ge = c >= kvec
            lo, hi, c_hi = jnp.where(ge, mid, lo), jnp.where(ge, hi, mid), jnp.where(ge, c_hi, c)

        def snap(hi):
            v = max_below(hi)
            return v, count_ge(v)

        def pending(cv):
            return jnp.max(jnp.where(cv < kvec, 1.0, 0.0))

        def refine(state):
            hi, c_hi, v, cv, _ = state
            short = cv < kvec
            hi = jnp.where(short, v, hi)
            c_hi = jnp.where(short, cv, c_hi)
            v, cv = snap(hi)
            return hi, c_hi, v, cv, pending(cv)

        v, cv = snap(hi)
        _, c_hi, v, cv, _ = lax.while_loop(lambda st: st[4] > 0.0, refine,
                                           (hi, c_hi, v, cv, pending(cv)))
        need = kvec - c_hi
        surplus = jnp.max(jnp.where(cv > kvec, 1.0, 0.0)) > 0.0
        vb = spread(v)

        @pl.when(jnp.logical_not(surplus))
        def _():
            def to_bias(i, carry):
                blk = sc_ref[rows_of(i), :].reshape(slabs, SUBLANES, tq)
                sc_ref[rows_of(i), :] = jnp.where(blk >= vb[None], 0.0, neg).reshape(kb, tq)
                return carry
            lax.fori_loop(0, nblk, to_bias, 0)

        @pl.when(surplus)
        def _():
            def key_idx(i):
                return (lax.broadcasted_iota(jnp.int32, (slabs, SUBLANES, tq), 0) * SUBLANES
                        + lax.broadcasted_iota(jnp.int32, (slabs, SUBLANES, tq), 1) + i * kb)

            def idx_step(_, carry):
                lo_i, hi_i = carry
                mid_i = (lo_i + hi_i) >> 1
                midb = jnp.broadcast_to(mid_i, (SUBLANES, tq))
                acc = sweep(lambda i, blk, a: a + jnp.sum(jnp.where(
                    (blk.reshape(slabs, SUBLANES, tq) == vb[None]) & (key_idx(i) <= midb[None]),
                    1.0, 0.0), axis=0), jnp.zeros((SUBLANES, tq), F32))
                ok = jnp.sum(acc, axis=0, keepdims=True) >= need
                return jnp.where(ok, lo_i, mid_i), jnp.where(ok, mid_i, hi_i)

            _, istar = lax.fori_loop(0, index_steps, idx_step,
                                     (jnp.full((1, tq), -1, jnp.int32),
                                      jnp.full((1, tq), nblk * kb - 1, jnp.int32)))
            istarb = jnp.broadcast_to(istar, (SUBLANES, tq))

            def to_bias(i, carry):
                blk = sc_ref[rows_of(i), :].reshape(slabs, SUBLANES, tq)
                keep = (blk > vb[None]) | ((blk == vb[None]) & (key_idx(i) <= istarb[None]))
                sc_ref[rows_of(i), :] = jnp.where(keep, 0.0, neg).reshape(kb, tq)
                return carry
            lax.fori_loop(0, nblk, to_bias, 0)

    def fold(x, op):
        return op(x.reshape(x.shape[0] // SUBLANES, SUBLANES, tq), axis=0)

    def bias_pass(i, maxima):
        rows = rows_of(i)
        bias = sc_ref[rows, :]
        new = []
        for h in range(heads):
            lg = lg_ref[h, rows, :] + bias
            lg_ref[h, rows, :] = lg
            new.append(jnp.maximum(maxima[h], fold(lg, jnp.max)))
        return tuple(new)

    maxima = lax.fori_loop(0, nblk, bias_pass, (full(neg),) * heads, unroll=min(nblk, 2))

    for h in range(heads):
        g = h // group
        m = jnp.max(maxima[h], axis=0, keepdims=True)
        s8 = jnp.zeros((SUBLANES, tq), F32)
        for c in range(ke // tq):
            rows = slice(c * tq, (c + 1) * tq)
            e = jnp.exp(lg_ref[h, rows, :] - m)
            e_ref[h, rows, :] = e.astype(BF16)
            s8 = s8 + fold(e, jnp.sum)
        denom = jnp.sum(s8, axis=0, keepdims=True)
        out_t = _dot(vt_ref[0, g * dh:(g + 1) * dh, :], e_ref[h])
        o_ref[0, :, h * dh:(h + 1) * dh] = (out_t / denom).T.astype(BF16)


def _sparse_attention_mixer(x2d, b, s, attn_gain, w_in, q_gain, k_gain, kidx_gain, tabs):
    t, d = x2d.shape
    heads, kv_heads, idx_heads, di = ATT_HEADS, ATT_KV_HEADS, IDX_HEADS, IDX_DH
    dh = d // heads
    c64, s64, c32, s32 = tabs
    tm = min(ROW_TILE, s)
    per_b = s // tm
    nq, nkv, nqi = heads * dh, kv_heads * dh, idx_heads * di
    n_main = nq + 2 * nkv + nqi + di
    zeros = lambda n: jnp.zeros((d, n), w_in.dtype)
    wt = jnp.concatenate([w_in[:, :n_main], zeros(LANES - di), w_in[:, n_main:n_main + idx_heads],
                          zeros(2 * SUBLANES - idx_heads)], axis=1).T.astype(BF16)
    qg = (q_gain * dh ** -0.5).reshape(dh, 1)
    row = lambda width: pl.BlockSpec((tm, width), lambda i: (i, 0))
    col = lambda n: pl.BlockSpec((n, tm), lambda i: (0, i))
    seq_t = lambda n: pl.BlockSpec((1, n, tm), lambda i: (i // per_b, 0, i % per_b))
    shp = lambda width, dt: jax.ShapeDtypeStruct((t, width), dt)
    shp_t = lambda n, dt: jax.ShapeDtypeStruct((b, n, s), dt)
    qt, k, vt, qit, ki, wit = pl.pallas_call(
        functools.partial(_dsa_inproj_kernel, heads=heads, kv_heads=kv_heads, dh=dh,
                          idx_heads=idx_heads, di=di, wscale=idx_heads ** -0.5 * di ** -0.5),
        grid=(t // tm,),
        in_specs=[row(d), _resident((1, d)), _resident(wt.shape), _resident((dh, 1)),
                  _resident((dh, 1)), _resident((di, 1)), col(dh // 2), col(dh // 2),
                  col(di // 2), col(di // 2)],
        out_specs=[seq_t(nq), row(nkv), seq_t(nkv), seq_t(nqi), row(di), seq_t(idx_heads)],
        out_shape=[shp_t(nq, BF16), shp(nkv, BF16), shp_t(nkv, BF16), shp_t(nqi, BF16),
                   shp(di, BF16), shp_t(idx_heads, F32)],
        compiler_params=_params("parallel"),
        name="dsa_inproj",
    )(x2d, attn_gain.reshape(1, d), wt, qg, k_gain.reshape(dh, 1), kidx_gain.reshape(di, 1),
      c64, s64, c32, s32)

    topk = min(TOPK_MAX, s // 4)
    tq = min(Q_TILE, s)
    k, ki = k.reshape(b, s, nkv), ki.reshape(b, s, di)
    outs = []
    for tile in range(s // tq):
        ke = (tile + 1) * tq
        outs.append(pl.pallas_call(
            functools.partial(_dsa_attn_kernel, tile=tile, topk=topk, heads=heads,
                              kv_heads=kv_heads, dh=dh, idx_heads=idx_heads,
                              index_steps=(ke - 1).bit_length() + 1),
            grid=(b,),
            in_specs=[pl.BlockSpec((1, nq, tq), lambda bi, tile=tile: (bi, 0, tile)),
                      pl.BlockSpec((1, ke, nkv), lambda bi: (bi, 0, 0)),
                      pl.BlockSpec((1, nkv, ke), lambda bi: (bi, 0, 0)),
                      pl.BlockSpec((1, nqi, tq), lambda bi, tile=tile: (bi, 0, tile)),
                      pl.BlockSpec((1, ke, di), lambda bi: (bi, 0, 0)),
                      pl.BlockSpec((1, idx_heads, tq), lambda bi, tile=tile: (bi, 0, tile))],
            out_specs=pl.BlockSpec((1, tq, nq), lambda bi: (bi, 0, 0)),
            out_shape=jax.ShapeDtypeStruct((b, tq, nq), BF16),
            scratch_shapes=[pltpu.VMEM((ke, tq), F32), pltpu.VMEM((heads, ke, tq), F32),
                            pltpu.VMEM((heads, ke, tq), BF16)],
            compiler_params=_params("parallel"),
            name=f"dsa_attn_{tile}",
        )(qt, k, vt, qit, ki, wit))
    return jnp.concatenate(outs, axis=1).reshape(t, nq)


def kernel(x, positions, attn_norm, ret_w_in, ret_out_norm, ret_w_out, dsa_w_in, dsa_q_norm,
           dsa_k_norm, dsa_kidx_norm, dsa_w_out, mlp_norm, mlp_w_up, mlp_w_down):
    b, s, d = x.shape
    depth = attn_norm.shape[0]
    c128, s128, c64, s64, c32, s32 = _rope_tables(positions)
    x2d = x.reshape(b * s, d)
    for i in range(depth):
        j = i // 2
        if i % 2 == 0:
            a = _retention_mixer(x2d, b, s, attn_norm[i], ret_w_in[j], ret_out_norm[j],
                                 c128, s128)
            w_out = ret_w_out[j]
        else:
            a = _sparse_attention_mixer(x2d, b, s, attn_norm[i], dsa_w_in[j], dsa_q_norm[j],
                                        dsa_k_norm[j], dsa_kidx_norm[j], (c64, s64, c32, s32))
            w_out = dsa_w_out[j]
        x2d = _mixer_out_and_mlp(a, x2d, w_out, mlp_norm[i], mlp_w_up[i], mlp_w_down[i])
    return x2d.reshape(b, s, d)
```

```python
import functools

import jax
import jax.numpy as jnp
from jax import lax
from jax.experimental import pallas as pl
from jax.experimental.pallas import tpu as pltpu

F32 = jnp.float32
BF16 = jnp.bfloat16

EPS = 1e-6
ROPE_THETA = 10000.0
RET_HEADS = 4
RET_CHUNK = 128
ATT_HEADS = 8
ATT_KV_HEADS = 2
IDX_HEADS = 8
IDX_DH = 64
TOPK_MAX = 256
LANES = 128
SUBLANES = 8

ROW_TILE = 512
RET_ROW_BLOCK = 512
Q_TILE = 256
KEY_BLOCK = 128
SWEEP_UNROLL = 8
VMEM_LIMIT = 56 * 1024 * 1024
BISECT_STEPS = 13


def _params(*sem):
    return pltpu.CompilerParams(dimension_semantics=sem, vmem_limit_bytes=VMEM_LIMIT)


def _resident(shape):
    nd = len(shape)
    return pl.BlockSpec(shape, lambda *_: (0,) * nd, pipeline_mode=pl.Buffered(1))


def _rms(x, gain):
    return x * lax.rsqrt(jnp.mean(x * x, axis=-1, keepdims=True) + EPS) * gain


def _dot(a, b):
    return jnp.dot(a, b, preferred_element_type=F32)


def _dot_nt(a, b):
    return lax.dot_general(a, b, (((1,), (1,)), ((), ())), preferred_element_type=F32)


def _dot_tn(a, b):
    return lax.dot_general(a, b, (((0,), (0,)), ((), ())), preferred_element_type=F32)


def _rope_table_kernel(pos_ref, inv_ref, c128_ref, s128_ref, c64t_ref, s64t_ref, c32t_ref,
                       s32t_ref, ct_ref, st_ref):
    ang_t = inv_ref[...] * pos_ref[...].astype(F32)
    cos_t = jnp.cos(ang_t)
    sin_t = jnp.sin(ang_t)
    c128_ref[...] = cos_t.T
    s128_ref[...] = sin_t.T
    for c in range(ct_ref.shape[0]):
        lanes = slice(c * LANES, (c + 1) * LANES)
        ct_ref[c] = cos_t[:, lanes]
        st_ref[c] = sin_t[:, lanes]
        for step, c_ref, s_ref in ((2, c64t_ref, s64t_ref), (4, c32t_ref, s32t_ref)):
            rows = pl.ds(0, LANES // step, stride=step)
            c_ref[:, lanes] = ct_ref[c, rows, :]
            s_ref[:, lanes] = st_ref[c, rows, :]


def _rope_tables(positions):
    t = positions.size
    tm = min(ROW_TILE, t)
    inv = ROPE_THETA ** (-jnp.arange(LANES, dtype=F32) / LANES)
    row = pl.BlockSpec((tm, LANES), lambda i: (i, 0))
    col = lambda n: pl.BlockSpec((n, tm), lambda i: (0, i))
    tab = lambda n: jax.ShapeDtypeStruct((n, t), F32)
    return pl.pallas_call(
        _rope_table_kernel,
        grid=(t // tm,),
        in_specs=[pl.BlockSpec((1, tm), lambda i: (0, i)), _resident((LANES, 1))],
        out_specs=[row, row, col(64), col(64), col(32), col(32)],
        out_shape=[jax.ShapeDtypeStruct((t, LANES), F32)] * 2 + [tab(64), tab(64), tab(32),
                                                                  tab(32)],
        scratch_shapes=[pltpu.VMEM((tm // LANES, LANES, LANES), F32)] * 2,
        compiler_params=_params("parallel"),
        name="rope_tables",
    )(positions.reshape(1, t), inv.reshape(LANES, 1))


def _ret_inproj_kernel(x_ref, gain_ref, w_ref, cos_ref, sin_ref, q_ref, k_ref, v_ref, g_ref,
                       *, heads, dk, dv, kscale):
    hn = _rms(x_ref[...], gain_ref[...]).astype(BF16)
    cos = cos_ref[...]
    sin = sin_ref[...]
    half = dk // 2
    for h in range(heads):
        for off, out_ref, scale in ((0, q_ref, None), (heads * dk, k_ref, kscale)):
            z = _dot(hn, w_ref[:, off + h * dk:off + (h + 1) * dk])
            x1, x2 = z[:, :half], z[:, half:]
            o1 = x1 * cos - x2 * sin
            o2 = x2 * cos + x1 * sin
            if scale is not None:
                o1, o2 = o1 * scale, o2 * scale
            out_ref[:, h * dk:h * dk + half] = o1.astype(BF16)
            out_ref[:, h * dk + half:(h + 1) * dk] = o2.astype(BF16)
    width = heads * dv
    for off, out_ref in ((2 * heads * dk, v_ref), (2 * heads * dk + width, g_ref)):
        for c in range(width // dv):
            out_ref[:, c * dv:(c + 1) * dv] = _dot(
                hn, w_ref[:, off + c * dv:off + (c + 1) * dv]).astype(BF16)


def _ret_core_kernel(q_ref, k_ref, v_ref, g_ref, dm_ref, qd_ref, kd_ref, cd_ref, gain_ref,
                     o_ref, state_ref, *, heads, chunk):
    @pl.when(pl.program_id(1) == 0)
    def _():
        state_ref[...] = jnp.zeros_like(state_ref)

    dk = q_ref.shape[2] // heads
    dv = v_ref.shape[2] // heads
    for h in range(heads):
        dm = dm_ref[h]
        qd = qd_ref[h]
        kd = kd_ref[h]
        cd = cd_ref[h]
        gain = gain_ref[h]
        state = state_ref[h]
        for c in range(q_ref.shape[1] // chunk):
            rows = slice(c * chunk, (c + 1) * chunk)
            q = q_ref[0, rows, h * dk:(h + 1) * dk]
            k = k_ref[0, rows, h * dk:(h + 1) * dk]
            v = v_ref[0, rows, h * dv:(h + 1) * dv]
            scores = _dot_nt(q, k) * dm
            o = _dot(scores.astype(BF16), v) + _dot(q, state.astype(BF16)) * qd
            k_dec = (k.astype(F32) * kd).astype(BF16)
            state = state * cd + _dot_tn(k_dec, v)
            y = _rms(o, gain)
            gate = g_ref[0, rows, h * dv:(h + 1) * dv].astype(F32)
            o_ref[0, rows, h * dv:(h + 1) * dv] = (y * (gate * jax.nn.sigmoid(gate))).astype(BF16)
        state_ref[h] = state


def _retention_mixer(x2d, b, s, attn_gain, w_in, out_gain, cos, sin):
    t, d = x2d.shape
    heads = RET_HEADS
    dk = d // heads
    dv = 2 * dk
    chunk = RET_CHUNK
    tm = min(ROW_TILE, t)
    w = w_in.astype(BF16)
    row = lambda width: pl.BlockSpec((tm, width), lambda i: (i, 0))
    q, k, v, g = pl.pallas_call(
        functools.partial(_ret_inproj_kernel, heads=heads, dk=dk, dv=dv, kscale=dk ** -0.5),
        grid=(t // tm,),
        in_specs=[row(d), _resident((1, d)), _resident(w.shape), row(LANES), row(LANES)],
        out_specs=[row(heads * dk), row(heads * dk), row(heads * dv), row(heads * dv)],
        out_shape=[jax.ShapeDtypeStruct((t, heads * dk), BF16),
                   jax.ShapeDtypeStruct((t, heads * dk), BF16),
                   jax.ShapeDtypeStruct((t, heads * dv), BF16),
                   jax.ShapeDtypeStruct((t, heads * dv), BF16)],
        compiler_params=_params("parallel"),
        name="ret_inproj",
    )(x2d, attn_gain.reshape(1, d), w, cos, sin)

    log_gamma = jnp.log1p(-(2.0 ** (-5.0 - jnp.arange(heads, dtype=F32))))
    i = jnp.arange(chunk, dtype=F32)
    diff = i[:, None] - i[None, :]
    dm = jnp.where(diff >= 0, jnp.exp(log_gamma[:, None, None] * jnp.maximum(diff, 0.0)), 0.0)
    qd = jnp.exp(log_gamma[:, None] * (i + 1.0))
    kd = jnp.exp(log_gamma[:, None] * (chunk - 1.0 - i))
    cd = jnp.exp(log_gamma * chunk)
    qd = jnp.broadcast_to(qd[:, :, None], (heads, chunk, dv))
    kd = jnp.broadcast_to(kd[:, :, None], (heads, chunk, dk))
    cd = jnp.broadcast_to(cd[:, None, None], (heads, 1, dv))

    rb = min(RET_ROW_BLOCK, s)
    seq = lambda width: pl.BlockSpec((1, rb, width), lambda bi, r: (bi, r, 0))
    o = pl.pallas_call(
        functools.partial(_ret_core_kernel, heads=heads, chunk=chunk),
        grid=(b, s // rb),
        in_specs=[seq(heads * dk), seq(heads * dk), seq(heads * dv), seq(heads * dv),
                  _resident(dm.shape), _resident(qd.shape), _resident(kd.shape),
                  _resident(cd.shape), _resident((heads, 1, dv))],
        out_specs=seq(heads * dv),
        out_shape=jax.ShapeDtypeStruct((b, s, heads * dv), BF16),
        scratch_shapes=[pltpu.VMEM((heads, dk, dv), F32)],
        compiler_params=_params("parallel", "arbitrary"),
        name="ret_core",
    )(q.reshape(b, s, -1), k.reshape(b, s, -1), v.reshape(b, s, -1), g.reshape(b, s, -1),
      dm, qd, kd, cd, out_gain.reshape(heads, 1, dv))
    return o.reshape(t, heads * dv)


def _tail_kernel(a_ref, x_ref, wo_ref, gain_ref, wup_ref, wdn_ref, o_ref, *, ff_chunk):
    x1 = x_ref[...] + _dot(a_ref[...], wo_ref[...])
    hn = _rms(x1, gain_ref[...]).astype(BF16)
    acc = x1
    for c in range(wup_ref.shape[1] // ff_chunk):
        cols = slice(c * ff_chunk, (c + 1) * ff_chunk)
        u = jnp.maximum(_dot(hn, wup_ref[:, cols]), 0.0)
        acc = acc + _dot((u * u).astype(BF16), wdn_ref[cols, :])
    o_ref[...] = acc


def _mixer_out_and_mlp(a2d, x2d, w_out, mlp_gain, w_up, w_down):
    t, d = x2d.shape
    tm = min(ROW_TILE, t)
    wo, wu, wd = w_out.astype(BF16), w_up.astype(BF16), w_down.astype(BF16)
    row = lambda width: pl.BlockSpec((tm, width), lambda i: (i, 0))
    return pl.pallas_call(
        functools.partial(_tail_kernel, ff_chunk=1024),
        grid=(t // tm,),
        in_specs=[row(a2d.shape[1]), row(d), _resident(wo.shape), _resident((1, d)),
                  _resident(wu.shape), _resident(wd.shape)],
        out_specs=row(d),
        out_shape=jax.ShapeDtypeStruct((t, d), F32),
        compiler_params=_params("parallel"),
        name="outproj_mlp",
    )(a2d, x2d, wo, mlp_gain.reshape(1, d), wu, wd)


def _norm_rope_rows(z, gain, cos, sin):
    if gain is not None:
        z = z * lax.rsqrt(jnp.mean(z * z, axis=0, keepdims=True) + EPS) * gain
    half = z.shape[0] // 2
    x1, x2 = z[:half], z[half:]
    return x1 * cos - x2 * sin, x2 * cos + x1 * sin


def _dsa_inproj_kernel(x_ref, gain_ref, wt_ref, qg_ref, kg_ref, kig_ref, c64_ref, s64_ref,
                       c32_ref, s32_ref, qt_ref, k_ref, vt_ref, qit_ref, ki_ref, wit_ref,
                       *, heads, kv_heads, dh, idx_heads, di, wscale):
    hn = _rms(x_ref[...], gain_ref[...]).astype(BF16)
    tm = hn.shape[0]
    proj = _dot_nt(wt_ref[...], hn)
    c64, s64 = c64_ref[...], s64_ref[...]
    c32, s32 = c32_ref[...], s32_ref[...]
    qg = jnp.broadcast_to(qg_ref[...], (dh, tm))
    kg = jnp.broadcast_to(kg_ref[...], (dh, tm))
    for h in range(heads):
        o1, o2 = _norm_rope_rows(proj[h * dh:(h + 1) * dh], qg, c64, s64)
        qt_ref[0, h * dh:h * dh + dh // 2, :] = o1.astype(BF16)
        qt_ref[0, h * dh + dh // 2:(h + 1) * dh, :] = o2.astype(BF16)
    off = heads * dh
    for h in range(kv_heads):
        o1, o2 = _norm_rope_rows(proj[off + h * dh:off + (h + 1) * dh], kg, c64, s64)
        k_ref[:, h * dh:(h + 1) * dh] = jnp.concatenate([o1, o2], axis=0).T.astype(BF16)
    off += kv_heads * dh
    vrows = vt_ref.shape[1] // kv_heads
    for g in range(kv_heads):
        vt_ref[0, g * vrows:g * vrows + dh, :] = proj[off + g * dh:off + (g + 1) * dh].astype(BF16)
        vt_ref[0, g * vrows + dh:(g + 1) * vrows, :] = jnp.ones((vrows - dh, tm), BF16)
    off += kv_heads * dh
    for h in range(idx_heads):
        o1, o2 = _norm_rope_rows(proj[off + h * di:off + (h + 1) * di], None, c32, s32)
        qit_ref[0, h * di:h * di + di // 2, :] = o1.astype(BF16)
        qit_ref[0, h * di + di // 2:(h + 1) * di, :] = o2.astype(BF16)
    off += idx_heads * di
    kig = jnp.broadcast_to(kig_ref[...], (di, tm))
    o1, o2 = _norm_rope_rows(proj[off:off + di], kig, c32, s32)
    ki_t = jnp.concatenate([o1, o2, proj[off + di:off + LANES]], axis=0)
    ki_ref[...] = ki_t.T[:, :di].astype(BF16)
    off += LANES
    wit_ref[0] = proj[off:off + idx_heads] * wscale


def _dsa_attn_kernel(qt_ref, k_ref, vt_ref, qit_ref, ki_ref, wit_ref, o_ref, sc_ref, lg_ref,
                     e_ref, cap_ref, *, tile, topk, heads, kv_heads, dh, idx_heads, index_steps):
    tq = qt_ref.shape[2]
    ke = k_ref.shape[1]
    di = qit_ref.shape[1] // idx_heads
    kb = KEY_BLOCK
    slabs = kb // SUBLANES
    j = tile
    nblk = ke // kb
    neg = float(jnp.finfo(F32).min)
    big = float(jnp.finfo(F32).max)
    diag = (lax.broadcasted_iota(jnp.int32, (tq, tq), 0)
            <= lax.broadcasted_iota(jnp.int32, (tq, tq), 1))

    def rows_of(i):
        return pl.ds(pl.multiple_of(i * kb, kb), kb)

    def full(val):
        return jnp.full((SUBLANES, tq), val, F32)

    def spread(row):
        return jnp.broadcast_to(row, (SUBLANES, tq))

    group = heads // kv_heads

    def raw_logits(head, i):
        g = head // group
        lg_ref[head, rows_of(i), :] = _dot(k_ref[0, rows_of(i), g * dh:(g + 1) * dh],
                                           qt_ref[0, head * dh:(head + 1) * dh, :])

    def sweep(fn, init, head=None):
        def body(i, a):
            if head is not None:
                raw_logits(head, i)
            return fn(i, sc_ref[rows_of(i), :], a)
        unroll = nblk if head is not None else max(
            u for u in range(1, SWEEP_UNROLL + 1) if nblk % u == 0)
        return lax.fori_loop(0, nblk, body, init, unroll=unroll)

    if ke <= topk:
        sc_ref[...] = jnp.where(diag, 0.0, neg)
        for h in range(heads):
            lax.fori_loop(0, nblk, lambda i, c, h=h: (raw_logits(h, i), c)[1], 0, unroll=True)
    else:
        wit = wit_ref[0]
        ki = ki_ref[0]
        score = jnp.zeros((ke, tq), F32)
        for h in range(idx_heads):
            rel = jnp.maximum(_dot(ki, qit_ref[0, h * di:(h + 1) * di, :]), 0.0)
            score = score + rel * wit[h:h + 1, :]
        last = score[ke - tq:, :]
        row_max = jnp.max(jnp.where(diag, last, neg), axis=0, keepdims=True)
        row_min = jnp.min(jnp.where(diag, last, big), axis=0, keepdims=True)
        if ke > tq:
            row_max = jnp.maximum(row_max, jnp.max(score[:ke - tq, :], axis=0, keepdims=True))
            row_min = jnp.minimum(row_min, jnp.min(score[:ke - tq, :], axis=0, keepdims=True))
            sc_ref[:ke - tq, :] = score[:ke - tq, :]
        sc_ref[ke - tq:, :] = jnp.where(diag, last, neg)
        q_pos = j * tq + lax.broadcasted_iota(jnp.int32, (1, tq), 1)
        kvec = jnp.minimum(q_pos + 1, topk).astype(F32)

        def count_ge(th, head=None):
            thb = spread(th)[None]
            acc = sweep(lambda i, blk, a: a + jnp.sum(
                jnp.where(blk.reshape(slabs, SUBLANES, tq) >= thb, 1.0, 0.0), axis=0),
                jnp.zeros((SUBLANES, tq), F32), head)
            return jnp.sum(acc, axis=0, keepdims=True)

        def max_below(hi):
            hib = spread(hi)[None]

            def step(i, blk, a):
                blk = blk.reshape(slabs, SUBLANES, tq)
                return jnp.maximum(a, jnp.max(jnp.where(blk < hib, blk, neg), axis=0))

            return jnp.max(sweep(step, full(neg)), axis=0, keepdims=True)

        lo, hi, c_hi = row_min, row_max + jnp.abs(row_max) * 1e-3 + 1e-3, jnp.zeros((1, tq), F32)
        for step in range(max(BISECT_STEPS, heads)):
            mid = lo * 0.5 + hi * 0.5
            c = count_ge(mid, step if step < heads else None)
            ge = c >= kvec
            lo, hi, c_hi = jnp.where(ge, mid, lo), jnp.where(ge, hi, mid), jnp.where(ge, c_hi, c)

        def snap(hi):
            v = max_below(hi)
            return v, count_ge(v)

        def pending(cv):
            return jnp.max(jnp.where(cv < kvec, 1.0, 0.0))

        def refine(state):
            hi, c_hi, v, cv, _ = state
            short = cv < kvec
            hi = jnp.where(short, v, hi)
            c_hi = jnp.where(short, cv, c_hi)
            v, cv = snap(hi)
            return hi, c_hi, v, cv, pending(cv)

        v, cv = snap(hi)
        _, c_hi, v, cv, _ = lax.while_loop(lambda st: st[4] > 0.0, refine,
                                           (hi, c_hi, v, cv, pending(cv)))
        need = kvec - c_hi
        surplus = jnp.max(jnp.where(cv > kvec, 1.0, 0.0)) > 0.0
        vb = spread(v)

        @pl.when(jnp.logical_not(surplus))
        def _():
            def to_bias(i, carry):
                blk = sc_ref[rows_of(i), :].reshape(slabs, SUBLANES, tq)
                sc_ref[rows_of(i), :] = jnp.where(blk >= vb[None], 0.0, neg).reshape(kb, tq)
                return carry
            lax.fori_loop(0, nblk, to_bias, 0)

        @pl.when(surplus)
        def _():
            def key_idx(i):
                return (lax.broadcasted_iota(jnp.int32, (slabs, SUBLANES, tq), 0) * SUBLANES
                        + lax.broadcasted_iota(jnp.int32, (slabs, SUBLANES, tq), 1) + i * kb)

            def idx_step(_, carry):
                lo_i, hi_i = carry
                mid_i = (lo_i + hi_i) >> 1
                midb = jnp.broadcast_to(mid_i, (SUBLANES, tq))
                acc = sweep(lambda i, blk, a: a + jnp.sum(jnp.where(
                    (blk.reshape(slabs, SUBLANES, tq) == vb[None]) & (key_idx(i) <= midb[None]),
                    1.0, 0.0), axis=0), jnp.zeros((SUBLANES, tq), F32))
                ok = jnp.sum(acc, axis=0, keepdims=True) >= need
                return jnp.where(ok, lo_i, mid_i), jnp.where(ok, mid_i, hi_i)

            _, istar = lax.fori_loop(0, index_steps, idx_step,
                                     (jnp.full((1, tq), -1, jnp.int32),
                                      jnp.full((1, tq), nblk * kb - 1, jnp.int32)))
            istarb = jnp.broadcast_to(istar, (SUBLANES, tq))

            def to_bias(i, carry):
                blk = sc_ref[rows_of(i), :].reshape(slabs, SUBLANES, tq)
                keep = (blk > vb[None]) | ((blk == vb[None]) & (key_idx(i) <= istarb[None]))
                sc_ref[rows_of(i), :] = jnp.where(keep, 0.0, neg).reshape(kb, tq)
                return carry
            lax.fori_loop(0, nblk, to_bias, 0)

    def fold(x, op):
        return op(x.reshape(x.shape[0] // SUBLANES, SUBLANES, tq), axis=0)

    cap_val = float(jnp.finfo(BF16).max)

    def max_pass(i, maxima):
        rows = rows_of(i)
        bias = sc_ref[rows, :]
        cap_ref[rows, :] = jnp.where(bias == 0.0, cap_val, 0.0).astype(BF16)
        return tuple(jnp.maximum(maxima[h], fold(lg_ref[h, rows, :] + bias, jnp.max))
                     for h in range(heads))

    maxima = lax.fori_loop(0, nblk, max_pass, (full(neg),) * heads, unroll=min(nblk, 2))

    vrows = vt_ref.shape[1] // kv_heads
    for h in range(heads):
        g = h // group
        m = jnp.max(maxima[h], axis=0, keepdims=True)
        for c in range(ke // tq):
            rows = slice(c * tq, (c + 1) * tq)
            e = jnp.exp(lg_ref[h, rows, :] - m).astype(BF16)
            e_ref[h, rows, :] = jnp.minimum(e, cap_ref[rows, :])
        out_t = _dot(vt_ref[0, g * vrows:(g + 1) * vrows, :], e_ref[h])
        o_ref[0, :, h * dh:(h + 1) * dh] = (out_t[:dh] / out_t[dh:dh + 1]).T.astype(BF16)


def _sparse_attention_mixer(x2d, b, s, attn_gain, w_in, q_gain, k_gain, kidx_gain, tabs):
    t, d = x2d.shape
    heads, kv_heads, idx_heads, di = ATT_HEADS, ATT_KV_HEADS, IDX_HEADS, IDX_DH
    dh = d // heads
    c64, s64, c32, s32 = tabs
    tm = min(ROW_TILE, s)
    per_b = s // tm
    nq, nkv, nqi = heads * dh, kv_heads * dh, idx_heads * di
    nvt = kv_heads * (dh + 2 * SUBLANES)
    n_main = nq + 2 * nkv + nqi + di
    zeros = lambda n: jnp.zeros((d, n), w_in.dtype)
    wt = jnp.concatenate([w_in[:, :n_main], zeros(LANES - di), w_in[:, n_main:n_main + idx_heads],
                          zeros(2 * SUBLANES - idx_heads)], axis=1).T.astype(BF16)
    qg = (q_gain * dh ** -0.5).reshape(dh, 1)
    row = lambda width: pl.BlockSpec((tm, width), lambda i: (i, 0))
    col = lambda n: pl.BlockSpec((n, tm), lambda i: (0, i))
    seq_t = lambda n: pl.BlockSpec((1, n, tm), lambda i: (i // per_b, 0, i % per_b))
    shp = lambda width, dt: jax.ShapeDtypeStruct((t, width), dt)
    shp_t = lambda n, dt: jax.ShapeDtypeStruct((b, n, s), dt)
    qt, k, vt, qit, ki, wit = pl.pallas_call(
        functools.partial(_dsa_inproj_kernel, heads=heads, kv_heads=kv_heads, dh=dh,
                          idx_heads=idx_heads, di=di, wscale=idx_heads ** -0.5 * di ** -0.5),
        grid=(t // tm,),
        in_specs=[row(d), _resident((1, d)), _resident(wt.shape), _resident((dh, 1)),
                  _resident((dh, 1)), _resident((di, 1)), col(dh // 2), col(dh // 2),
                  col(di // 2), col(di // 2)],
        out_specs=[seq_t(nq), row(nkv), seq_t(nvt), seq_t(nqi), row(di), seq_t(idx_heads)],
        out_shape=[shp_t(nq, BF16), shp(nkv, BF16), shp_t(nvt, BF16), shp_t(nqi, BF16),
                   shp(di, BF16), shp_t(idx_heads, F32)],
        compiler_params=_params("parallel"),
        name="dsa_inproj",
    )(x2d, attn_gain.reshape(1, d), wt, qg, k_gain.reshape(dh, 1), kidx_gain.reshape(di, 1),
      c64, s64, c32, s32)

    topk = min(TOPK_MAX, s // 4)
    tq = min(Q_TILE, s)
    k, ki = k.reshape(b, s, nkv), ki.reshape(b, s, di)
    outs = []
    for tile in range(s // tq):
        ke = (tile + 1) * tq
        outs.append(pl.pallas_call(
            functools.partial(_dsa_attn_kernel, tile=tile, topk=topk, heads=heads,
                              kv_heads=kv_heads, dh=dh, idx_heads=idx_heads,
                              index_steps=(ke - 1).bit_length() + 1),
            grid=(b,),
            in_specs=[pl.BlockSpec((1, nq, tq), lambda bi, tile=tile: (bi, 0, tile)),
                      pl.BlockSpec((1, ke, nkv), lambda bi: (bi, 0, 0)),
                      pl.BlockSpec((1, nvt, ke), lambda bi: (bi, 0, 0)),
                      pl.BlockSpec((1, nqi, tq), lambda bi, tile=tile: (bi, 0, tile)),
                      pl.BlockSpec((1, ke, di), lambda bi: (bi, 0, 0)),
                      pl.BlockSpec((1, idx_heads, tq), lambda bi, tile=tile: (bi, 0, tile))],
            out_specs=pl.BlockSpec((1, tq, nq), lambda bi: (bi, 0, 0)),
            out_shape=jax.ShapeDtypeStruct((b, tq, nq), BF16),
            scratch_shapes=[pltpu.VMEM((ke, tq), F32), pltpu.VMEM((heads, ke, tq), F32),
                            pltpu.VMEM((heads, ke, tq), BF16), pltpu.VMEM((ke, tq), BF16)],
            compiler_params=_params("parallel"),
            name=f"dsa_attn_{tile}",
        )(qt, k, vt, qit, ki, wit))
    return jnp.concatenate(outs, axis=1).reshape(t, nq)


def kernel(x, positions, attn_norm, ret_w_in, ret_out_norm, ret_w_out, dsa_w_in, dsa_q_norm,
           dsa_k_norm, dsa_kidx_norm, dsa_w_out, mlp_norm, mlp_w_up, mlp_w_down):
    b, s, d = x.shape
    depth = attn_norm.shape[0]
    c128, s128, c64, s64, c32, s32 = _rope_tables(positions)
    x2d = x.reshape(b * s, d)
    for i in range(depth):
        j = i // 2
        if i % 2 == 0:
            a = _retention_mixer(x2d, b, s, attn_norm[i], ret_w_in[j], ret_out_norm[j],
                                 c128, s128)
            w_out = ret_w_out[j]
        else:
            a = _sparse_attention_mixer(x2d, b, s, attn_norm[i], dsa_w_in[j], dsa_q_norm[j],
                                        dsa_k_norm[j], dsa_kidx_norm[j], (c64, s64, c32, s32))
            w_out = dsa_w_out[j]
        x2d = _mixer_out_and_mlp(a, x2d, w_out, mlp_norm[i], mlp_w_up[i], mlp_w_down[i])
    return x2d.reshape(b, s, d)
```

```python
import functools
import math

import jax
import jax.numpy as jnp
from jax import lax
from jax.experimental import pallas as pl
from jax.experimental.pallas import tpu as pltpu

F32 = jnp.float32
BF16 = jnp.bfloat16

EPS = 1e-6
ROPE_THETA = 10000.0
RET_HEADS = 4
RET_CHUNK = 128
ATT_HEADS = 8
ATT_KV_HEADS = 2
IDX_HEADS = 8
IDX_DH = 64
TOPK_MAX = 256
LANES = 128
SUBLANES = 8

ROW_TILE = 512
RET_ROW_BLOCK = 512
Q_TILE = 256
KEY_BLOCK = 128
SWEEP_UNROLL = 8
VMEM_LIMIT = 56 * 1024 * 1024
BISECT_STEPS = 13


def _params(*sem):
    return pltpu.CompilerParams(dimension_semantics=sem, vmem_limit_bytes=VMEM_LIMIT)


def _resident(shape):
    nd = len(shape)
    return pl.BlockSpec(shape, lambda *_: (0,) * nd, pipeline_mode=pl.Buffered(1))


def _rms(x, gain):
    return x * lax.rsqrt(jnp.mean(x * x, axis=-1, keepdims=True) + EPS) * gain


def _dot(a, b):
    return jnp.dot(a, b, preferred_element_type=F32)


def _dot_nt(a, b):
    return lax.dot_general(a, b, (((1,), (1,)), ((), ())), preferred_element_type=F32)


def _dot_tn(a, b):
    return lax.dot_general(a, b, (((0,), (0,)), ((), ())), preferred_element_type=F32)


def _rope_table_kernel(pos_ref, inv_ref, c128_ref, s128_ref, c64t_ref, s64t_ref, c32t_ref,
                       s32t_ref, ct_ref, st_ref):
    ang_t = inv_ref[...] * pos_ref[...].astype(F32)
    cos_t = jnp.cos(ang_t)
    sin_t = jnp.sin(ang_t)
    c128_ref[...] = cos_t.T
    s128_ref[...] = sin_t.T
    for c in range(ct_ref.shape[0]):
        lanes = slice(c * LANES, (c + 1) * LANES)
        ct_ref[c] = cos_t[:, lanes]
        st_ref[c] = sin_t[:, lanes]
        for step, c_ref, s_ref in ((2, c64t_ref, s64t_ref), (4, c32t_ref, s32t_ref)):
            rows = pl.ds(0, LANES // step, stride=step)
            c_ref[:, lanes] = ct_ref[c, rows, :]
            s_ref[:, lanes] = st_ref[c, rows, :]


def _rope_tables(positions):
    t = positions.size
    tm = min(ROW_TILE, t)
    inv = ROPE_THETA ** (-jnp.arange(LANES, dtype=F32) / LANES)
    row = pl.BlockSpec((tm, LANES), lambda i: (i, 0))
    col = lambda n: pl.BlockSpec((n, tm), lambda i: (0, i))
    tab = lambda n: jax.ShapeDtypeStruct((n, t), F32)
    return pl.pallas_call(
        _rope_table_kernel,
        grid=(t // tm,),
        in_specs=[pl.BlockSpec((1, tm), lambda i: (0, i)), _resident((LANES, 1))],
        out_specs=[row, row, col(64), col(64), col(32), col(32)],
        out_shape=[jax.ShapeDtypeStruct((t, LANES), F32)] * 2 + [tab(64), tab(64), tab(32),
                                                                  tab(32)],
        scratch_shapes=[pltpu.VMEM((tm // LANES, LANES, LANES), F32)] * 2,
        compiler_params=_params("parallel"),
        name="rope_tables",
    )(positions.reshape(1, t), inv.reshape(LANES, 1))


def _ret_inproj_kernel(x_ref, gain_ref, w_ref, cos_ref, sin_ref, q_ref, k_ref, v_ref, g_ref,
                       *, heads, dk, dv, kscale):
    hn = _rms(x_ref[...], gain_ref[...]).astype(BF16)
    cos = cos_ref[...]
    sin = sin_ref[...]
    half = dk // 2
    width = heads * dv
    for off, out_ref, swish in ((2 * heads * dk + width, g_ref, True),
                                (2 * heads * dk, v_ref, False)):
        for c in range(width // dv):
            z = _dot(hn, w_ref[:, off + c * dv:off + (c + 1) * dv])
            if swish:
                z = z * jax.nn.sigmoid(z)
            out_ref[:, c * dv:(c + 1) * dv] = z.astype(BF16)
    for h in range(heads):
        for off, out_ref, scale in ((0, q_ref, None), (heads * dk, k_ref, kscale)):
            z = _dot(hn, w_ref[:, off + h * dk:off + (h + 1) * dk])
            x1, x2 = z[:, :half], z[:, half:]
            o1 = x1 * cos - x2 * sin
            o2 = x2 * cos + x1 * sin
            if scale is not None:
                o1, o2 = o1 * scale, o2 * scale
            out_ref[:, h * dk:h * dk + half] = o1.astype(BF16)
            out_ref[:, h * dk + half:(h + 1) * dk] = o2.astype(BF16)


def _ret_core_kernel(q_ref, k_ref, v_ref, g_ref, dm_ref, qd_ref, kd_ref, cd_ref,
                     o_ref, state_ref, *, heads, chunk):
    @pl.when(pl.program_id(1) == 0)
    def _():
        state_ref[...] = jnp.zeros_like(state_ref)

    dk = q_ref.shape[2] // heads
    dv = v_ref.shape[2] // heads
    hs = range(heads)
    states = [state_ref[h] for h in hs]
    for c in range(q_ref.shape[1] // chunk):
        rows = slice(c * chunk, (c + 1) * chunk)
        q = [q_ref[0, rows, h * dk:(h + 1) * dk] for h in hs]
        k = [k_ref[0, rows, h * dk:(h + 1) * dk] for h in hs]
        v = [v_ref[0, rows, h * dv:(h + 1) * dv] for h in hs]
        scores = [_dot_nt(q[h], k[h]) for h in hs]
        kv = [_dot_tn((k[h].astype(F32) * kd_ref[h]).astype(BF16), v[h]) for h in hs]
        cross = [_dot(q[h], states[h].astype(BF16)) for h in hs]
        intra = [_dot((scores[h] * dm_ref[h]).astype(BF16), v[h]) for h in hs]
        for h in hs:
            o = intra[h] + cross[h] * qd_ref[h]
            y = o * lax.rsqrt(jnp.mean(o * o, axis=-1, keepdims=True) + EPS)
            gate = g_ref[0, rows, h * dv:(h + 1) * dv].astype(F32)
            o_ref[0, rows, h * dv:(h + 1) * dv] = (y * gate).astype(BF16)
            states[h] = states[h] * cd_ref[h] + kv[h]
    for h in hs:
        state_ref[h] = states[h]


def _retention_mixer(x2d, b, s, attn_gain, w_in, cos, sin):
    t, d = x2d.shape
    heads = RET_HEADS
    dk = d // heads
    dv = 2 * dk
    chunk = RET_CHUNK
    tm = min(ROW_TILE, t)
    w = w_in.astype(BF16)
    row = lambda width: pl.BlockSpec((tm, width), lambda i: (i, 0))
    q, k, v, g = pl.pallas_call(
        functools.partial(_ret_inproj_kernel, heads=heads, dk=dk, dv=dv, kscale=dk ** -0.5),
        grid=(t // tm,),
        in_specs=[row(d), _resident((1, d)), _resident(w.shape), row(LANES), row(LANES)],
        out_specs=[row(heads * dk), row(heads * dk), row(heads * dv), row(heads * dv)],
        out_shape=[jax.ShapeDtypeStruct((t, heads * dk), BF16),
                   jax.ShapeDtypeStruct((t, heads * dk), BF16),
                   jax.ShapeDtypeStruct((t, heads * dv), BF16),
                   jax.ShapeDtypeStruct((t, heads * dv), BF16)],
        compiler_params=_params("parallel"),
        name="ret_inproj",
    )(x2d, attn_gain.reshape(1, d), w, cos, sin)

    log_gamma = jnp.log1p(-(2.0 ** (-5.0 - jnp.arange(heads, dtype=F32))))
    i = jnp.arange(chunk, dtype=F32)
    diff = i[:, None] - i[None, :]
    dm = jnp.where(diff >= 0, jnp.exp(log_gamma[:, None, None] * jnp.maximum(diff, 0.0)), 0.0)
    qd = jnp.exp(log_gamma[:, None] * (i + 1.0))
    kd = jnp.exp(log_gamma[:, None] * (chunk - 1.0 - i))
    cd = jnp.exp(log_gamma * chunk)
    qd = jnp.broadcast_to(qd[:, :, None], (heads, chunk, dv))
    kd = jnp.broadcast_to(kd[:, :, None], (heads, chunk, dk))
    cd = jnp.broadcast_to(cd[:, None, None], (heads, 1, dv))

    rb = min(RET_ROW_BLOCK, s)
    seq = lambda width: pl.BlockSpec((1, rb, width), lambda bi, r: (bi, r, 0))
    o = pl.pallas_call(
        functools.partial(_ret_core_kernel, heads=heads, chunk=chunk),
        grid=(b, s // rb),
        in_specs=[seq(heads * dk), seq(heads * dk), seq(heads * dv), seq(heads * dv),
                  _resident(dm.shape), _resident(qd.shape), _resident(kd.shape),
                  _resident(cd.shape)],
        out_specs=seq(heads * dv),
        out_shape=jax.ShapeDtypeStruct((b, s, heads * dv), BF16),
        scratch_shapes=[pltpu.VMEM((heads, dk, dv), F32)],
        compiler_params=_params("parallel", "arbitrary"),
        name="ret_core",
    )(q.reshape(b, s, -1), k.reshape(b, s, -1), v.reshape(b, s, -1), g.reshape(b, s, -1),
      dm, qd, kd, cd)
    return o.reshape(t, heads * dv)


def _tail_kernel(a_ref, x_ref, wo_ref, gain_ref, wup_ref, wdn_ref, o_ref, *, ff_chunk):
    x1 = x_ref[...] + _dot(a_ref[...], wo_ref[...])
    hn = _rms(x1, gain_ref[...]).astype(BF16)
    acc = x1
    for c in range(wup_ref.shape[1] // ff_chunk):
        cols = slice(c * ff_chunk, (c + 1) * ff_chunk)
        u = jnp.maximum(_dot(hn, wup_ref[:, cols]), 0.0)
        acc = acc + _dot((u * u).astype(BF16), wdn_ref[cols, :])
    o_ref[...] = acc


def _mixer_out_and_mlp(a2d, x2d, w_out, mlp_gain, w_up, w_down):
    t, d = x2d.shape
    tm = min(ROW_TILE, t)
    wo, wu, wd = w_out.astype(BF16), w_up.astype(BF16), w_down.astype(BF16)
    row = lambda width: pl.BlockSpec((tm, width), lambda i: (i, 0))
    return pl.pallas_call(
        functools.partial(_tail_kernel, ff_chunk=1024),
        grid=(t // tm,),
        in_specs=[row(a2d.shape[1]), row(d), _resident(wo.shape), _resident((1, d)),
                  _resident(wu.shape), _resident(wd.shape)],
        out_specs=row(d),
        out_shape=jax.ShapeDtypeStruct((t, d), F32),
        compiler_params=_params("parallel"),
        name="outproj_mlp",
    )(a2d, x2d, wo, mlp_gain.reshape(1, d), wu, wd)


def _norm_rope_rows(z, gain, cos, sin):
    if gain is not None:
        z = z * lax.rsqrt(jnp.mean(z * z, axis=0, keepdims=True) + EPS) * gain
    half = z.shape[0] // 2
    x1, x2 = z[:half], z[half:]
    return x1 * cos - x2 * sin, x2 * cos + x1 * sin


def _dsa_inproj_kernel(x_ref, gain_ref, wt_ref, qg_ref, kg_ref, kig_ref, c64_ref, s64_ref,
                       c32_ref, s32_ref, qt_ref, k_ref, vt_ref, qit_ref, ki_ref, wit_ref,
                       *, heads, kv_heads, dh, idx_heads, di, wscale):
    hn = _rms(x_ref[...], gain_ref[...]).astype(BF16)
    tm = hn.shape[0]
    proj = _dot_nt(wt_ref[...], hn)
    c64, s64 = c64_ref[...], s64_ref[...]
    c32, s32 = c32_ref[...], s32_ref[...]
    qg = jnp.broadcast_to(qg_ref[...], (dh, tm))
    kg = jnp.broadcast_to(kg_ref[...], (dh, tm))
    for h in range(heads):
        o1, o2 = _norm_rope_rows(proj[h * dh:(h + 1) * dh], qg, c64, s64)
        qt_ref[0, h * dh:h * dh + dh // 2, :] = o1.astype(BF16)
        qt_ref[0, h * dh + dh // 2:(h + 1) * dh, :] = o2.astype(BF16)
    off = heads * dh
    for h in range(kv_heads):
        o1, o2 = _norm_rope_rows(proj[off + h * dh:off + (h + 1) * dh], kg, c64, s64)
        k_ref[:, h * dh:(h + 1) * dh] = jnp.concatenate([o1, o2], axis=0).T.astype(BF16)
    off += kv_heads * dh
    vrows = vt_ref.shape[1] // kv_heads
    for g in range(kv_heads):
        vt_ref[0, g * vrows:g * vrows + dh, :] = proj[off + g * dh:off + (g + 1) * dh].astype(BF16)
        vt_ref[0, g * vrows + dh:(g + 1) * vrows, :] = jnp.ones((vrows - dh, tm), BF16)
    off += kv_heads * dh
    for h in range(idx_heads):
        o1, o2 = _norm_rope_rows(proj[off + h * di:off + (h + 1) * di], None, c32, s32)
        qit_ref[0, h * di:h * di + di // 2, :] = o1.astype(BF16)
        qit_ref[0, h * di + di // 2:(h + 1) * di, :] = o2.astype(BF16)
    off += idx_heads * di
    kig = jnp.broadcast_to(kig_ref[...], (di, tm))
    o1, o2 = _norm_rope_rows(proj[off:off + di], kig, c32, s32)
    ki_t = jnp.concatenate([o1, o2, proj[off + di:off + LANES]], axis=0)
    ki_ref[...] = ki_t.T[:, :di].astype(BF16)
    off += LANES
    wit_ref[0] = proj[off:off + idx_heads] * wscale


def _dsa_attn_kernel(*refs, tile, topk, heads, kv_heads, dh, idx_heads, index_steps):
    qt_ref, k_ref, vt_ref, qit_ref, ki_ref, wit_ref = refs[:6]
    o_ref, sc_ref, lg_ref, e_ref, cap_ref = refs[-5:]
    tq = qt_ref.shape[2]
    ke = k_ref.shape[1]
    di = qit_ref.shape[1] // idx_heads
    kb = KEY_BLOCK
    slabs = kb // SUBLANES
    j = tile
    nblk = ke // kb
    neg = float(jnp.finfo(F32).min)
    big = float(jnp.finfo(F32).max)
    diag = (lax.broadcasted_iota(jnp.int32, (tq, tq), 0)
            <= lax.broadcasted_iota(jnp.int32, (tq, tq), 1))

    def rows_of(i):
        return pl.ds(pl.multiple_of(i * kb, kb), kb)

    def full(val):
        return jnp.full((SUBLANES, tq), val, F32)

    def spread(row):
        return jnp.broadcast_to(row, (SUBLANES, tq))

    group = heads // kv_heads

    def raw_logits(head, i):
        g = head // group
        lg_ref[head, rows_of(i), :] = _dot(k_ref[0, rows_of(i), g * dh:(g + 1) * dh],
                                           qt_ref[0, head * dh:(head + 1) * dh, :])

    def sweep(fn, init, head=None):
        def body(i, a):
            if head is not None:
                raw_logits(head, i)
            return fn(i, sc_ref[rows_of(i), :], a)
        unroll = nblk if head is not None else max(
            u for u in range(1, SWEEP_UNROLL + 1) if nblk % u == 0)
        return lax.fori_loop(0, nblk, body, init, unroll=unroll)

    if ke <= topk:
        sc_ref[...] = jnp.where(diag, 0.0, neg)
        for h in range(heads):
            lax.fori_loop(0, nblk, lambda i, c, h=h: (raw_logits(h, i), c)[1], 0, unroll=True)
    else:
        wit = wit_ref[0]
        ki = ki_ref[0]
        score = jnp.zeros((ke, tq), F32)
        for h in range(idx_heads):
            rel = jnp.maximum(_dot(ki, qit_ref[0, h * di:(h + 1) * di, :]), 0.0)
            score = score + rel * wit[h:h + 1, :]
        last = score[ke - tq:, :]
        row_max = jnp.max(jnp.where(diag, last, neg), axis=0, keepdims=True)
        row_min = jnp.min(jnp.where(diag, last, big), axis=0, keepdims=True)
        if ke > tq:
            row_max = jnp.maximum(row_max, jnp.max(score[:ke - tq, :], axis=0, keepdims=True))
            row_min = jnp.minimum(row_min, jnp.min(score[:ke - tq, :], axis=0, keepdims=True))
            sc_ref[:ke - tq, :] = score[:ke - tq, :]
        sc_ref[ke - tq:, :] = jnp.where(diag, last, neg)
        q_pos = j * tq + lax.broadcasted_iota(jnp.int32, (1, tq), 1)
        kvec = jnp.minimum(q_pos + 1, topk).astype(F32)

        def count_ge(th, head=None):
            thb = spread(th)[None]
            acc = sweep(lambda i, blk, a: a + jnp.sum(
                jnp.where(blk.reshape(slabs, SUBLANES, tq) >= thb, 1.0, 0.0), axis=0),
                jnp.zeros((SUBLANES, tq), F32), head)
            return jnp.sum(acc, axis=0, keepdims=True)

        def max_below(hi):
            hib = spread(hi)[None]

            def step(i, blk, a):
                blk = blk.reshape(slabs, SUBLANES, tq)
                return jnp.maximum(a, jnp.max(jnp.where(blk < hib, blk, neg), axis=0))

            return jnp.max(sweep(step, full(neg)), axis=0, keepdims=True)

        lo, hi, c_hi = row_min, row_max + jnp.abs(row_max) * 1e-3 + 1e-3, jnp.zeros((1, tq), F32)
        for step in range(max(BISECT_STEPS, heads)):
            mid = lo * 0.5 + hi * 0.5
            c = count_ge(mid, step if step < heads else None)
            ge = c >= kvec
            lo, hi, c_hi = jnp.where(ge, mid, lo), jnp.where(ge, hi, mid), jnp.where(ge, c_hi, c)

        def snap(hi):
            v = max_below(hi)
            return v, count_ge(v)

        def pending(cv):
            return jnp.max(jnp.where(cv < kvec, 1.0, 0.0))

        def refine(state):
            hi, c_hi, v, cv, _ = state
            short = cv < kvec
            hi = jnp.where(short, v, hi)
            c_hi = jnp.where(short, cv, c_hi)
            v, cv = snap(hi)
            return hi, c_hi, v, cv, pending(cv)

        v, cv = snap(hi)
        _, c_hi, v, cv, _ = lax.while_loop(lambda st: st[4] > 0.0, refine,
                                           (hi, c_hi, v, cv, pending(cv)))
        need = kvec - c_hi
        surplus = jnp.max(jnp.where(cv > kvec, 1.0, 0.0)) > 0.0
        vb = spread(v)

        @pl.when(jnp.logical_not(surplus))
        def _():
            def to_bias(i, carry):
                blk = sc_ref[rows_of(i), :].reshape(slabs, SUBLANES, tq)
                sc_ref[rows_of(i), :] = jnp.where(blk >= vb[None], 0.0, neg).reshape(kb, tq)
                return carry
            lax.fori_loop(0, nblk, to_bias, 0)

        @pl.when(surplus)
        def _():
            def key_idx(i):
                return (lax.broadcasted_iota(jnp.int32, (slabs, SUBLANES, tq), 0) * SUBLANES
                        + lax.broadcasted_iota(jnp.int32, (slabs, SUBLANES, tq), 1) + i * kb)

            def idx_step(_, carry):
                lo_i, hi_i = carry
                mid_i = (lo_i + hi_i) >> 1
                midb = jnp.broadcast_to(mid_i, (SUBLANES, tq))
                acc = sweep(lambda i, blk, a: a + jnp.sum(jnp.where(
                    (blk.reshape(slabs, SUBLANES, tq) == vb[None]) & (key_idx(i) <= midb[None]),
                    1.0, 0.0), axis=0), jnp.zeros((SUBLANES, tq), F32))
                ok = jnp.sum(acc, axis=0, keepdims=True) >= need
                return jnp.where(ok, lo_i, mid_i), jnp.where(ok, mid_i, hi_i)

            _, istar = lax.fori_loop(0, index_steps, idx_step,
                                     (jnp.full((1, tq), -1, jnp.int32),
                                      jnp.full((1, tq), nblk * kb - 1, jnp.int32)))
            istarb = jnp.broadcast_to(istar, (SUBLANES, tq))

            def to_bias(i, carry):
                blk = sc_ref[rows_of(i), :].reshape(slabs, SUBLANES, tq)
                keep = (blk > vb[None]) | ((blk == vb[None]) & (key_idx(i) <= istarb[None]))
                sc_ref[rows_of(i), :] = jnp.where(keep, 0.0, neg).reshape(kb, tq)
                return carry
            lax.fori_loop(0, nblk, to_bias, 0)

    def fold(x, op):
        return op(x.reshape(x.shape[0] // SUBLANES, SUBLANES, tq), axis=0)

    cap_val = float(jnp.finfo(BF16).max)

    def max_pass(i, maxima):
        rows = rows_of(i)
        bias = sc_ref[rows, :]
        cap_ref[rows, :] = jnp.where(bias == 0.0, cap_val, 0.0).astype(BF16)
        return tuple(jnp.maximum(maxima[h], fold(lg_ref[h, rows, :] + bias, jnp.max))
                     for h in range(heads))

    maxima = lax.fori_loop(0, nblk, max_pass, (full(neg),) * heads, unroll=min(nblk, 2))

    vrows = vt_ref.shape[1] // kv_heads
    for h in range(heads):
        g = h // group
        m = jnp.max(maxima[h], axis=0, keepdims=True)
        for c in range(ke // tq):
            rows = slice(c * tq, (c + 1) * tq)
            e = jnp.exp2(lg_ref[h, rows, :] - m).astype(BF16)
            e_ref[h, rows, :] = jnp.minimum(e, cap_ref[rows, :])
        out_t = _dot(vt_ref[0, g * vrows:(g + 1) * vrows, :], e_ref[h])
        o_ref[0, :, h * dh:(h + 1) * dh] = (out_t[:dh] / out_t[dh:dh + 1]).T.astype(BF16)


def _sparse_attention_mixer(x2d, b, s, attn_gain, w_in, q_gain, k_gain, kidx_gain, tabs):
    t, d = x2d.shape
    heads, kv_heads, idx_heads, di = ATT_HEADS, ATT_KV_HEADS, IDX_HEADS, IDX_DH
    dh = d // heads
    c64, s64, c32, s32 = tabs
    tm = min(ROW_TILE, s)
    per_b = s // tm
    nq, nkv, nqi = heads * dh, kv_heads * dh, idx_heads * di
    nvt = kv_heads * (dh + 2 * SUBLANES)
    n_main = nq + 2 * nkv + nqi + di
    zeros = lambda n: jnp.zeros((d, n), w_in.dtype)
    wt = jnp.concatenate([w_in[:, :n_main], zeros(LANES - di), w_in[:, n_main:n_main + idx_heads],
                          zeros(2 * SUBLANES - idx_heads)], axis=1).T.astype(BF16)
    qg = (q_gain * (dh ** -0.5 * math.log2(math.e))).reshape(dh, 1)
    row = lambda width: pl.BlockSpec((tm, width), lambda i: (i, 0))
    col = lambda n: pl.BlockSpec((n, tm), lambda i: (0, i))
    seq_t = lambda n: pl.BlockSpec((1, n, tm), lambda i: (i // per_b, 0, i % per_b))
    shp = lambda width, dt: jax.ShapeDtypeStruct((t, width), dt)
    shp_t = lambda n, dt: jax.ShapeDtypeStruct((b, n, s), dt)
    qt, k, vt, qit, ki, wit = pl.pallas_call(
        functools.partial(_dsa_inproj_kernel, heads=heads, kv_heads=kv_heads, dh=dh,
                          idx_heads=idx_heads, di=di, wscale=idx_heads ** -0.5 * di ** -0.5),
        grid=(t // tm,),
        in_specs=[row(d), _resident((1, d)), _resident(wt.shape), _resident((dh, 1)),
                  _resident((dh, 1)), _resident((di, 1)), col(dh // 2), col(dh // 2),
                  col(di // 2), col(di // 2)],
        out_specs=[seq_t(nq), row(nkv), seq_t(nvt), seq_t(nqi), row(di), seq_t(idx_heads)],
        out_shape=[shp_t(nq, BF16), shp(nkv, BF16), shp_t(nvt, BF16), shp_t(nqi, BF16),
                   shp(di, BF16), shp_t(idx_heads, F32)],
        compiler_params=_params("parallel"),
        name="dsa_inproj",
    )(x2d, attn_gain.reshape(1, d), wt, qg, k_gain.reshape(dh, 1), kidx_gain.reshape(di, 1),
      c64, s64, c32, s32)

    topk = min(TOPK_MAX, s // 4)
    tq = min(Q_TILE, s)
    k, ki = k.reshape(b, s, nkv), ki.reshape(b, s, di)
    out = None
    for tile in range(s // tq):
        ke = (tile + 1) * tq
        operands = [qt, k, vt, qit, ki, wit]
        in_specs = [pl.BlockSpec((1, nq, tq), lambda bi, tile=tile: (bi, 0, tile)),
                    pl.BlockSpec((1, ke, nkv), lambda bi: (bi, 0, 0)),
                    pl.BlockSpec((1, nvt, ke), lambda bi: (bi, 0, 0)),
                    pl.BlockSpec((1, nqi, tq), lambda bi, tile=tile: (bi, 0, tile)),
                    pl.BlockSpec((1, ke, di), lambda bi: (bi, 0, 0)),
                    pl.BlockSpec((1, idx_heads, tq), lambda bi, tile=tile: (bi, 0, tile))]
        aliases = {}
        if out is not None:
            aliases = {len(operands): 0}
            operands.append(out)
            in_specs.append(pl.BlockSpec(memory_space=pl.ANY))
        out = pl.pallas_call(
            functools.partial(_dsa_attn_kernel, tile=tile, topk=topk, heads=heads,
                              kv_heads=kv_heads, dh=dh, idx_heads=idx_heads,
                              index_steps=(ke - 1).bit_length() + 1),
            grid=(b,),
            in_specs=in_specs,
            out_specs=pl.BlockSpec((1, tq, nq), lambda bi, tile=tile: (bi, tile, 0)),
            out_shape=jax.ShapeDtypeStruct((b, s, nq), BF16),
            scratch_shapes=[pltpu.VMEM((ke, tq), F32), pltpu.VMEM((heads, ke, tq), F32),
                            pltpu.VMEM((heads, ke, tq), BF16), pltpu.VMEM((ke, tq), BF16)],
            input_output_aliases=aliases,
            compiler_params=_params("parallel"),
            name=f"dsa_attn_{tile}",
        )(*operands)
    return out.reshape(t, nq)


def kernel(x, positions, attn_norm, ret_w_in, ret_out_norm, ret_w_out, dsa_w_in, dsa_q_norm,
           dsa_k_norm, dsa_kidx_norm, dsa_w_out, mlp_norm, mlp_w_up, mlp_w_down):
    b, s, d = x.shape
    depth = attn_norm.shape[0]
    c128, s128, c64, s64, c32, s32 = _rope_tables(positions)
    x2d = x.reshape(b * s, d)
    for i in range(depth):
        j = i // 2
        if i % 2 == 0:
            a = _retention_mixer(x2d, b, s, attn_norm[i], ret_w_in[j], c128, s128)
            w_out = ret_out_norm[j].reshape(-1, 1) * ret_w_out[j]
        else:
            a = _sparse_attention_mixer(x2d, b, s, attn_norm[i], dsa_w_in[j], dsa_q_norm[j],
                                        dsa_k_norm[j], dsa_kidx_norm[j], (c64, s64, c32, s32))
            w_out = dsa_w_out[j]
        x2d = _mixer_out_and_mlp(a, x2d, w_out, mlp_norm[i], mlp_w_up[i], mlp_w_down[i])
    return x2d.reshape(b, s, d)
```

```python
import functools
import math

import jax
import jax.numpy as jnp
from jax import lax
from jax.experimental import pallas as pl
from jax.experimental.pallas import tpu as pltpu

F32 = jnp.float32
BF16 = jnp.bfloat16

EPS = 1e-6
ROPE_THETA = 10000.0
RET_HEADS = 4
RET_CHUNK = 128
ATT_HEADS = 8
ATT_KV_HEADS = 2
IDX_HEADS = 8
IDX_DH = 64
TOPK_MAX = 256
LANES = 128
SUBLANES = 8

ROW_TILE = 512
RET_ROW_BLOCK = 1024
Q_TILE = 256
KEY_BLOCK = 128
SWEEP_UNROLL = 8
VMEM_LIMIT = 56 * 1024 * 1024
BISECT_STEPS = 13


def _params(*sem):
    return pltpu.CompilerParams(dimension_semantics=sem, vmem_limit_bytes=VMEM_LIMIT)


def _resident(shape):
    nd = len(shape)
    return pl.BlockSpec(shape, lambda *_: (0,) * nd, pipeline_mode=pl.Buffered(1))


def _rms(x, gain):
    return x * lax.rsqrt(jnp.mean(x * x, axis=-1, keepdims=True) + EPS) * gain


def _dot(a, b):
    return jnp.dot(a, b, preferred_element_type=F32)


def _dot_nt(a, b):
    return lax.dot_general(a, b, (((1,), (1,)), ((), ())), preferred_element_type=F32)


def _dot_tn(a, b):
    return lax.dot_general(a, b, (((0,), (0,)), ((), ())), preferred_element_type=F32)


def _rope_table_kernel(pos_ref, inv_ref, c128_ref, s128_ref, c64t_ref, s64t_ref, c32t_ref,
                       s32t_ref, ct_ref, st_ref):
    ang_t = inv_ref[...] * pos_ref[...].astype(F32)
    cos_t = jnp.cos(ang_t)
    sin_t = jnp.sin(ang_t)
    c128_ref[...] = cos_t.T
    s128_ref[...] = sin_t.T
    for c in range(ct_ref.shape[0]):
        lanes = slice(c * LANES, (c + 1) * LANES)
        ct_ref[c] = cos_t[:, lanes]
        st_ref[c] = sin_t[:, lanes]
        for step, c_ref, s_ref in ((2, c64t_ref, s64t_ref), (4, c32t_ref, s32t_ref)):
            rows = pl.ds(0, LANES // step, stride=step)
            c_ref[:, lanes] = ct_ref[c, rows, :]
            s_ref[:, lanes] = st_ref[c, rows, :]


def _rope_tables(positions):
    t = positions.size
    tm = min(ROW_TILE, t)
    inv = ROPE_THETA ** (-jnp.arange(LANES, dtype=F32) / LANES)
    row = pl.BlockSpec((tm, LANES), lambda i: (i, 0))
    col = lambda n: pl.BlockSpec((n, tm), lambda i: (0, i))
    tab = lambda n: jax.ShapeDtypeStruct((n, t), F32)
    return pl.pallas_call(
        _rope_table_kernel,
        grid=(t // tm,),
        in_specs=[pl.BlockSpec((1, tm), lambda i: (0, i)), _resident((LANES, 1))],
        out_specs=[row, row, col(64), col(64), col(32), col(32)],
        out_shape=[jax.ShapeDtypeStruct((t, LANES), F32)] * 2 + [tab(64), tab(64), tab(32),
                                                                  tab(32)],
        scratch_shapes=[pltpu.VMEM((tm // LANES, LANES, LANES), F32)] * 2,
        compiler_params=_params("parallel"),
        name="rope_tables",
    )(positions.reshape(1, t), inv.reshape(LANES, 1))


def _ret_inproj_kernel(x_ref, gain_ref, w_ref, cos_ref, sin_ref, q_ref, k_ref, v_ref, g_ref,
                       *, heads, dk, dv, kscale):
    hn = _rms(x_ref[...], gain_ref[...]).astype(BF16)
    cos = cos_ref[...]
    sin = sin_ref[...]
    half = dk // 2
    width = heads * dv
    for off, out_ref, swish in ((2 * heads * dk + width, g_ref, True),
                                (2 * heads * dk, v_ref, False)):
        for c in range(width // dv):
            z = _dot(hn, w_ref[:, off + c * dv:off + (c + 1) * dv])
            if swish:
                z = z * jax.nn.sigmoid(z)
            out_ref[:, c * dv:(c + 1) * dv] = z.astype(BF16)
    for h in range(heads):
        for off, out_ref, scale in ((0, q_ref, None), (heads * dk, k_ref, kscale)):
            z = _dot(hn, w_ref[:, off + h * dk:off + (h + 1) * dk])
            x1, x2 = z[:, :half], z[:, half:]
            o1 = x1 * cos - x2 * sin
            o2 = x2 * cos + x1 * sin
            if scale is not None:
                o1, o2 = o1 * scale, o2 * scale
            out_ref[:, h * dk:h * dk + half] = o1.astype(BF16)
            out_ref[:, h * dk + half:(h + 1) * dk] = o2.astype(BF16)


def _ret_core_kernel(q_ref, k_ref, v_ref, g_ref, dm_ref, qd_ref, kd_ref, cd_ref,
                     o_ref, state_ref, *, heads, chunk):
    @pl.when(pl.program_id(1) == 0)
    def _():
        state_ref[...] = jnp.zeros_like(state_ref)

    dk = q_ref.shape[2] // heads
    dv = v_ref.shape[2] // heads
    hs = range(heads)
    states = [state_ref[h] for h in hs]
    for c in range(q_ref.shape[1] // chunk):
        rows = slice(c * chunk, (c + 1) * chunk)
        q = [q_ref[0, rows, h * dk:(h + 1) * dk] for h in hs]
        k = [k_ref[0, rows, h * dk:(h + 1) * dk] for h in hs]
        v = [v_ref[0, rows, h * dv:(h + 1) * dv] for h in hs]
        scores = [_dot_nt(q[h], k[h]) for h in hs]
        kv = [_dot_tn((k[h].astype(F32) * kd_ref[h]).astype(BF16), v[h]) for h in hs]
        cross = [_dot(q[h], states[h].astype(BF16)) for h in hs]
        intra = [_dot((scores[h] * dm_ref[h]).astype(BF16), v[h]) for h in hs]
        for h in hs:
            o = intra[h] + cross[h] * qd_ref[h]
            y = o * lax.rsqrt(jnp.mean(o * o, axis=-1, keepdims=True) + EPS)
            gate = g_ref[0, rows, h * dv:(h + 1) * dv].astype(F32)
            o_ref[0, rows, h * dv:(h + 1) * dv] = (y * gate).astype(BF16)
            states[h] = states[h] * cd_ref[h] + kv[h]
    for h in hs:
        state_ref[h] = states[h]


def _retention_mixer(x2d, b, s, attn_gain, w_in, cos, sin):
    t, d = x2d.shape
    heads = RET_HEADS
    dk = d // heads
    dv = 2 * dk
    chunk = RET_CHUNK
    tm = min(ROW_TILE, t)
    w = w_in.astype(BF16)
    row = lambda width: pl.BlockSpec((tm, width), lambda i: (i, 0))
    q, k, v, g = pl.pallas_call(
        functools.partial(_ret_inproj_kernel, heads=heads, dk=dk, dv=dv, kscale=dk ** -0.5),
        grid=(t // tm,),
        in_specs=[row(d), _resident((1, d)), _resident(w.shape), row(LANES), row(LANES)],
        out_specs=[row(heads * dk), row(heads * dk), row(heads * dv), row(heads * dv)],
        out_shape=[jax.ShapeDtypeStruct((t, heads * dk), BF16),
                   jax.ShapeDtypeStruct((t, heads * dk), BF16),
                   jax.ShapeDtypeStruct((t, heads * dv), BF16),
                   jax.ShapeDtypeStruct((t, heads * dv), BF16)],
        compiler_params=_params("parallel"),
        name="ret_inproj",
    )(x2d, attn_gain.reshape(1, d), w, cos, sin)

    log_gamma = jnp.log1p(-(2.0 ** (-5.0 - jnp.arange(heads, dtype=F32))))
    i = jnp.arange(chunk, dtype=F32)
    diff = i[:, None] - i[None, :]
    dm = jnp.where(diff >= 0, jnp.exp(log_gamma[:, None, None] * jnp.maximum(diff, 0.0)), 0.0)
    qd = jnp.exp(log_gamma[:, None] * (i + 1.0))
    kd = jnp.exp(log_gamma[:, None] * (chunk - 1.0 - i))
    cd = jnp.exp(log_gamma * chunk)
    qd = jnp.broadcast_to(qd[:, :, None], (heads, chunk, dv))
    kd = jnp.broadcast_to(kd[:, :, None], (heads, chunk, dk))
    cd = jnp.broadcast_to(cd[:, None, None], (heads, 1, dv))

    rb = min(RET_ROW_BLOCK, s)
    seq = lambda width: pl.BlockSpec((1, rb, width), lambda bi, r: (bi, r, 0))
    o = pl.pallas_call(
        functools.partial(_ret_core_kernel, heads=heads, chunk=chunk),
        grid=(b, s // rb),
        in_specs=[seq(heads * dk), seq(heads * dk), seq(heads * dv), seq(heads * dv),
                  _resident(dm.shape), _resident(qd.shape), _resident(kd.shape),
                  _resident(cd.shape)],
        out_specs=seq(heads * dv),
        out_shape=jax.ShapeDtypeStruct((b, s, heads * dv), BF16),
        scratch_shapes=[pltpu.VMEM((heads, dk, dv), F32)],
        compiler_params=_params("parallel", "arbitrary"),
        name="ret_core",
    )(q.reshape(b, s, -1), k.reshape(b, s, -1), v.reshape(b, s, -1), g.reshape(b, s, -1),
      dm, qd, kd, cd)
    return o.reshape(t, heads * dv)


def _tail_kernel(a_ref, x_ref, wo_ref, gain_ref, wup_ref, wdn_ref, o_ref, *, ff_chunk):
    x1 = x_ref[...] + _dot(a_ref[...], wo_ref[...])
    hn = _rms(x1, gain_ref[...]).astype(BF16)
    acc = x1
    for c in range(wup_ref.shape[1] // ff_chunk):
        cols = slice(c * ff_chunk, (c + 1) * ff_chunk)
        u = jnp.maximum(_dot(hn, wup_ref[:, cols]), 0.0)
        acc = acc + _dot((u * u).astype(BF16), wdn_ref[cols, :])
    o_ref[...] = acc


def _mixer_out_and_mlp(a2d, x2d, w_out, mlp_gain, w_up, w_down):
    t, d = x2d.shape
    tm = min(ROW_TILE, t)
    wo, wu, wd = w_out.astype(BF16), w_up.astype(BF16), w_down.astype(BF16)
    row = lambda width: pl.BlockSpec((tm, width), lambda i: (i, 0))
    return pl.pallas_call(
        functools.partial(_tail_kernel, ff_chunk=1024),
        grid=(t // tm,),
        in_specs=[row(a2d.shape[1]), row(d), _resident(wo.shape), _resident((1, d)),
                  _resident(wu.shape), _resident(wd.shape)],
        out_specs=row(d),
        out_shape=jax.ShapeDtypeStruct((t, d), F32),
        compiler_params=_params("parallel"),
        name="outproj_mlp",
    )(a2d, x2d, wo, mlp_gain.reshape(1, d), wu, wd)


def _norm_rope_rows(z, gain, cos, sin):
    if gain is not None:
        z = z * lax.rsqrt(jnp.mean(z * z, axis=0, keepdims=True) + EPS) * gain
    half = z.shape[0] // 2
    x1, x2 = z[:half], z[half:]
    return x1 * cos - x2 * sin, x2 * cos + x1 * sin


def _dsa_inproj_kernel(x_ref, gain_ref, wt_ref, qg_ref, kg_ref, kig_ref, c64_ref, s64_ref,
                       c32_ref, s32_ref, qt_ref, k_ref, vt_ref, qit_ref, ki_ref, wit_ref,
                       *, heads, kv_heads, dh, idx_heads, di, wscale):
    hn = _rms(x_ref[...], gain_ref[...]).astype(BF16)
    tm = hn.shape[0]
    proj = _dot_nt(wt_ref[...], hn)
    c64, s64 = c64_ref[...], s64_ref[...]
    c32, s32 = c32_ref[...], s32_ref[...]
    qg = jnp.broadcast_to(qg_ref[...], (dh, tm))
    kg = jnp.broadcast_to(kg_ref[...], (dh, tm))
    for h in range(heads):
        o1, o2 = _norm_rope_rows(proj[h * dh:(h + 1) * dh], qg, c64, s64)
        qt_ref[0, h * dh:h * dh + dh // 2, :] = o1.astype(BF16)
        qt_ref[0, h * dh + dh // 2:(h + 1) * dh, :] = o2.astype(BF16)
    off = heads * dh
    for h in range(kv_heads):
        o1, o2 = _norm_rope_rows(proj[off + h * dh:off + (h + 1) * dh], kg, c64, s64)
        k_ref[:, h * dh:(h + 1) * dh] = jnp.concatenate([o1, o2], axis=0).T.astype(BF16)
    off += kv_heads * dh
    vrows = vt_ref.shape[1] // kv_heads
    for g in range(kv_heads):
        vt_ref[0, g * vrows:g * vrows + dh, :] = proj[off + g * dh:off + (g + 1) * dh].astype(BF16)
        vt_ref[0, g * vrows + dh:(g + 1) * vrows, :] = jnp.ones((vrows - dh, tm), BF16)
    off += kv_heads * dh
    for h in range(idx_heads):
        o1, o2 = _norm_rope_rows(proj[off + h * di:off + (h + 1) * di], None, c32, s32)
        qit_ref[0, h * di:h * di + di // 2, :] = o1.astype(BF16)
        qit_ref[0, h * di + di // 2:(h + 1) * di, :] = o2.astype(BF16)
    off += idx_heads * di
    kig = jnp.broadcast_to(kig_ref[...], (di, tm))
    o1, o2 = _norm_rope_rows(proj[off:off + di], kig, c32, s32)
    ki_t = jnp.concatenate([o1, o2, proj[off + di:off + LANES]], axis=0)
    ki_ref[...] = ki_t.T[:, :di].astype(BF16)
    off += LANES
    wit_ref[0] = proj[off:off + idx_heads] * wscale


def _dsa_attn_kernel(*refs, tile, topk, heads, kv_heads, dh, idx_heads, index_steps):
    qt_ref, k_ref, vt_ref, qit_ref, ki_ref, wit_ref = refs[:6]
    o_ref, sc_ref, lg_ref, e_ref, cap_ref = refs[-5:]
    tq = qt_ref.shape[2]
    ke = k_ref.shape[1]
    di = qit_ref.shape[1] // idx_heads
    kb = KEY_BLOCK
    slabs = kb // SUBLANES
    j = tile
    nblk = ke // kb
    neg = float(jnp.finfo(F32).min)
    big = float(jnp.finfo(F32).max)
    diag = (lax.broadcasted_iota(jnp.int32, (tq, tq), 0)
            <= lax.broadcasted_iota(jnp.int32, (tq, tq), 1))

    def rows_of(i):
        return pl.ds(pl.multiple_of(i * kb, kb), kb)

    def full(val):
        return jnp.full((SUBLANES, tq), val, F32)

    def spread(row):
        return jnp.broadcast_to(row, (SUBLANES, tq))

    group = heads // kv_heads

    def raw_logits(head, i):
        g = head // group
        lg_ref[head, rows_of(i), :] = _dot(k_ref[0, rows_of(i), g * dh:(g + 1) * dh],
                                           qt_ref[0, head * dh:(head + 1) * dh, :])

    def sweep(fn, init, head=None, unrolled=False):
        unroll = nblk if unrolled else max(u for u in range(1, SWEEP_UNROLL + 1) if nblk % u == 0)
        acc = lax.fori_loop(0, nblk, lambda i, a: fn(i, sc_ref[rows_of(i), :], a), init,
                            unroll=unroll)
        if head is not None:
            g = head // group
            lg = _dot(k_ref[0, :, g * dh:(g + 1) * dh], qt_ref[0, head * dh:(head + 1) * dh, :])
            lg_ref[head] = lg
            acc = acc + jnp.minimum(jnp.maximum(lg[ke - SUBLANES:, :], 0.0), 0.0)
        return acc

    if ke <= topk:
        sc_ref[...] = jnp.where(diag, 0.0, neg)
        for h in range(heads):
            lax.fori_loop(0, nblk, lambda i, c, h=h: (raw_logits(h, i), c)[1], 0, unroll=True)
    else:
        wit = wit_ref[0]
        ki = ki_ref[0]
        score = jnp.zeros((ke, tq), F32)
        for h in range(idx_heads):
            rel = jnp.maximum(_dot(ki, qit_ref[0, h * di:(h + 1) * di, :]), 0.0)
            score = score + rel * wit[h:h + 1, :]
        last = score[ke - tq:, :]
        row_max = jnp.max(jnp.where(diag, last, neg), axis=0, keepdims=True)
        row_min = jnp.min(jnp.where(diag, last, big), axis=0, keepdims=True)
        if ke > tq:
            row_max = jnp.maximum(row_max, jnp.max(score[:ke - tq, :], axis=0, keepdims=True))
            row_min = jnp.minimum(row_min, jnp.min(score[:ke - tq, :], axis=0, keepdims=True))
            sc_ref[:ke - tq, :] = score[:ke - tq, :]
        sc_ref[ke - tq:, :] = jnp.where(diag, last, neg)
        q_pos = j * tq + lax.broadcasted_iota(jnp.int32, (1, tq), 1)
        kvec = jnp.minimum(q_pos + 1, topk).astype(F32)

        def count_ge(th, head=None, unrolled=False):
            thb = spread(th)[None]
            acc = sweep(lambda i, blk, a: a + jnp.sum(
                jnp.where(blk.reshape(slabs, SUBLANES, tq) >= thb, 1.0, 0.0), axis=0),
                jnp.zeros((SUBLANES, tq), F32), head, unrolled)
            return jnp.sum(acc, axis=0, keepdims=True)

        def max_below(hi):
            hib = spread(hi)[None]

            def step(i, blk, a):
                blk = blk.reshape(slabs, SUBLANES, tq)
                return jnp.maximum(a, jnp.max(jnp.where(blk < hib, blk, neg), axis=0))

            return jnp.max(sweep(step, full(neg)), axis=0, keepdims=True)

        steps = max(BISECT_STEPS, heads)
        carried = {(h * steps) // heads: h for h in range(heads)}
        lo, hi, c_hi = row_min, row_max + jnp.abs(row_max) * 1e-3 + 1e-3, jnp.zeros((1, tq), F32)
        for step in range(steps):
            mid = lo * 0.5 + hi * 0.5
            c = count_ge(mid, carried.get(step), unrolled=True)
            ge = c >= kvec
            lo, hi, c_hi = jnp.where(ge, mid, lo), jnp.where(ge, hi, mid), jnp.where(ge, c_hi, c)

        def snap(hi):
            v = max_below(hi)
            return v, count_ge(v)

        def pending(cv):
            return jnp.max(jnp.where(cv < kvec, 1.0, 0.0))

        def refine(state):
            hi, c_hi, v, cv, _ = state
            short = cv < kvec
            hi = jnp.where(short, v, hi)
            c_hi = jnp.where(short, cv, c_hi)
            v, cv = snap(hi)
            return hi, c_hi, v, cv, pending(cv)

        v, cv = snap(hi)
        _, c_hi, v, cv, _ = lax.while_loop(lambda st: st[4] > 0.0, refine,
                                           (hi, c_hi, v, cv, pending(cv)))
        need = kvec - c_hi
        surplus = jnp.max(jnp.where(cv > kvec, 1.0, 0.0)) > 0.0
        vb = spread(v)

        @pl.when(jnp.logical_not(surplus))
        def _():
            def to_bias(i, carry):
                blk = sc_ref[rows_of(i), :].reshape(slabs, SUBLANES, tq)
                sc_ref[rows_of(i), :] = jnp.where(blk >= vb[None], 0.0, neg).reshape(kb, tq)
                return carry
            lax.fori_loop(0, nblk, to_bias, 0)

        @pl.when(surplus)
        def _():
            def key_idx(i):
                return (lax.broadcasted_iota(jnp.int32, (slabs, SUBLANES, tq), 0) * SUBLANES
                        + lax.broadcasted_iota(jnp.int32, (slabs, SUBLANES, tq), 1) + i * kb)

            def idx_step(_, carry):
                lo_i, hi_i = carry
                mid_i = (lo_i + hi_i) >> 1
                midb = jnp.broadcast_to(mid_i, (SUBLANES, tq))
                acc = sweep(lambda i, blk, a: a + jnp.sum(jnp.where(
                    (blk.reshape(slabs, SUBLANES, tq) == vb[None]) & (key_idx(i) <= midb[None]),
                    1.0, 0.0), axis=0), jnp.zeros((SUBLANES, tq), F32))
                ok = jnp.sum(acc, axis=0, keepdims=True) >= need
                return jnp.where(ok, lo_i, mid_i), jnp.where(ok, mid_i, hi_i)

            _, istar = lax.fori_loop(0, index_steps, idx_step,
                                     (jnp.full((1, tq), -1, jnp.int32),
                                      jnp.full((1, tq), nblk * kb - 1, jnp.int32)))
            istarb = jnp.broadcast_to(istar, (SUBLANES, tq))

            def to_bias(i, carry):
                blk = sc_ref[rows_of(i), :].reshape(slabs, SUBLANES, tq)
                keep = (blk > vb[None]) | ((blk == vb[None]) & (key_idx(i) <= istarb[None]))
                sc_ref[rows_of(i), :] = jnp.where(keep, 0.0, neg).reshape(kb, tq)
                return carry
            lax.fori_loop(0, nblk, to_bias, 0)

    def fold(x, op):
        return op(x.reshape(x.shape[0] // SUBLANES, SUBLANES, tq), axis=0)

    cap_val = float(jnp.finfo(BF16).max)

    def max_pass(i, maxima):
        rows = rows_of(i)
        bias = sc_ref[rows, :]
        cap_ref[rows, :] = jnp.where(bias == 0.0, cap_val, 0.0).astype(BF16)
        return tuple(jnp.maximum(maxima[h], fold(lg_ref[h, rows, :] + bias, jnp.max))
                     for h in range(heads))

    maxima = lax.fori_loop(0, nblk, max_pass, (full(neg),) * heads, unroll=min(nblk, 2))

    vrows = vt_ref.shape[1] // kv_heads
    for h in range(heads):
        g = h // group
        m = jnp.max(maxima[h], axis=0, keepdims=True)
        for c in range(ke // tq):
            rows = slice(c * tq, (c + 1) * tq)
            e = jnp.exp2(lg_ref[h, rows, :] - m).astype(BF16)
            e_ref[h, rows, :] = jnp.minimum(e, cap_ref[rows, :])
        out_t = _dot(vt_ref[0, g * vrows:(g + 1) * vrows, :], e_ref[h])
        o_ref[0, :, h * dh:(h + 1) * dh] = (out_t[:dh] / out_t[dh:dh + 1]).T.astype(BF16)


def _sparse_attention_mixer(x2d, b, s, attn_gain, w_in, q_gain, k_gain, kidx_gain, tabs):
    t, d = x2d.shape
    heads, kv_heads, idx_heads, di = ATT_HEADS, ATT_KV_HEADS, IDX_HEADS, IDX_DH
    dh = d // heads
    c64, s64, c32, s32 = tabs
    tm = min(ROW_TILE, s)
    per_b = s // tm
    nq, nkv, nqi = heads * dh, kv_heads * dh, idx_heads * di
    nvt = kv_heads * (dh + 2 * SUBLANES)
    n_main = nq + 2 * nkv + nqi + di
    zeros = lambda n: jnp.zeros((d, n), w_in.dtype)
    wt = jnp.concatenate([w_in[:, :n_main], zeros(LANES - di), w_in[:, n_main:n_main + idx_heads],
                          zeros(2 * SUBLANES - idx_heads)], axis=1).T.astype(BF16)
    qg = (q_gain * (dh ** -0.5 * math.log2(math.e))).reshape(dh, 1)
    row = lambda width: pl.BlockSpec((tm, width), lambda i: (i, 0))
    col = lambda n: pl.BlockSpec((n, tm), lambda i: (0, i))
    seq_t = lambda n: pl.BlockSpec((1, n, tm), lambda i: (i // per_b, 0, i % per_b))
    shp = lambda width, dt: jax.ShapeDtypeStruct((t, width), dt)
    shp_t = lambda n, dt: jax.ShapeDtypeStruct((b, n, s), dt)
    qt, k, vt, qit, ki, wit = pl.pallas_call(
        functools.partial(_dsa_inproj_kernel, heads=heads, kv_heads=kv_heads, dh=dh,
                          idx_heads=idx_heads, di=di, wscale=idx_heads ** -0.5 * di ** -0.5),
        grid=(t // tm,),
        in_specs=[row(d), _resident((1, d)), _resident(wt.shape), _resident((dh, 1)),
                  _resident((dh, 1)), _resident((di, 1)), col(dh // 2), col(dh // 2),
                  col(di // 2), col(di // 2)],
        out_specs=[seq_t(nq), row(nkv), seq_t(nvt), seq_t(nqi), row(di), seq_t(idx_heads)],
        out_shape=[shp_t(nq, BF16), shp(nkv, BF16), shp_t(nvt, BF16), shp_t(nqi, BF16),
                   shp(di, BF16), shp_t(idx_heads, F32)],
        compiler_params=_params("parallel"),
        name="dsa_inproj",
    )(x2d, attn_gain.reshape(1, d), wt, qg, k_gain.reshape(dh, 1), kidx_gain.reshape(di, 1),
      c64, s64, c32, s32)

    topk = min(TOPK_MAX, s // 4)
    tq = min(Q_TILE, s)
    k, ki = k.reshape(b, s, nkv), ki.reshape(b, s, di)
    out = None
    for tile in range(s // tq):
        ke = (tile + 1) * tq
        operands = [qt, k, vt, qit, ki, wit]
        in_specs = [pl.BlockSpec((1, nq, tq), lambda bi, tile=tile: (bi, 0, tile)),
                    pl.BlockSpec((1, ke, nkv), lambda bi: (bi, 0, 0)),
                    pl.BlockSpec((1, nvt, ke), lambda bi: (bi, 0, 0)),
                    pl.BlockSpec((1, nqi, tq), lambda bi, tile=tile: (bi, 0, tile)),
                    pl.BlockSpec((1, ke, di), lambda bi: (bi, 0, 0)),
                    pl.BlockSpec((1, idx_heads, tq), lambda bi, tile=tile: (bi, 0, tile))]
        aliases = {}
        if out is not None:
            aliases = {len(operands): 0}
            operands.append(out)
            in_specs.append(pl.BlockSpec(memory_space=pl.ANY))
        out = pl.pallas_call(
            functools.partial(_dsa_attn_kernel, tile=tile, topk=topk, heads=heads,
                              kv_heads=kv_heads, dh=dh, idx_heads=idx_heads,
                              index_steps=(ke - 1).bit_length() + 1),
            grid=(b,),
            in_specs=in_specs,
            out_specs=pl.BlockSpec((1, tq, nq), lambda bi, tile=tile: (bi, tile, 0)),
            out_shape=jax.ShapeDtypeStruct((b, s, nq), BF16),
            scratch_shapes=[pltpu.VMEM((ke, tq), F32), pltpu.VMEM((heads, ke, tq), F32),
                            pltpu.VMEM((heads, ke, tq), BF16), pltpu.VMEM((ke, tq), BF16)],
            input_output_aliases=aliases,
            compiler_params=_params("parallel"),
            name=f"dsa_attn_{tile}",
        )(*operands)
    return out.reshape(t, nq)


def kernel(x, positions, attn_norm, ret_w_in, ret_out_norm, ret_w_out, dsa_w_in, dsa_q_norm,
           dsa_k_norm, dsa_kidx_norm, dsa_w_out, mlp_norm, mlp_w_up, mlp_w_down):
    b, s, d = x.shape
    depth = attn_norm.shape[0]
    c128, s128, c64, s64, c32, s32 = _rope_tables(positions)
    x2d = x.reshape(b * s, d)
    for i in range(depth):
        j = i // 2
        if i % 2 == 0:
            a = _retention_mixer(x2d, b, s, attn_norm[i], ret_w_in[j], c128, s128)
            w_out = ret_out_norm[j].reshape(-1, 1) * ret_w_out[j]
        else:
            a = _sparse_attention_mixer(x2d, b, s, attn_norm[i], dsa_w_in[j], dsa_q_norm[j],
                                        dsa_k_norm[j], dsa_kidx_norm[j], (c64, s64, c32, s32))
            w_out = dsa_w_out[j]
        x2d = _mixer_out_and_mlp(a, x2d, w_out, mlp_norm[i], mlp_w_up[i], mlp_w_down[i])
    return x2d.reshape(b, s, d)
```

```python
import functools
import math

import jax
import jax.numpy as jnp
from jax import lax
from jax.experimental import pallas as pl
from jax.experimental.pallas import tpu as pltpu

F32 = jnp.float32
BF16 = jnp.bfloat16

EPS = 1e-6
ROPE_THETA = 10000.0
RET_HEADS = 4
RET_CHUNK = 128
ATT_HEADS = 8
ATT_KV_HEADS = 2
IDX_HEADS = 8
IDX_DH = 64
TOPK_MAX = 256
LANES = 128
SUBLANES = 8

ROW_TILE = 512
RET_ROW_BLOCK = 1024
Q_TILE = 256
KEY_BLOCK = 128
SWEEP_UNROLL = 8
VMEM_LIMIT = 56 * 1024 * 1024
BISECT_STEPS = 13


def _params(*sem):
    return pltpu.CompilerParams(dimension_semantics=sem, vmem_limit_bytes=VMEM_LIMIT)


def _resident(shape):
    nd = len(shape)
    return pl.BlockSpec(shape, lambda *_: (0,) * nd, pipeline_mode=pl.Buffered(1))


def _rms(x, gain):
    return x * lax.rsqrt(jnp.mean(x * x, axis=-1, keepdims=True) + EPS) * gain


def _dot(a, b):
    return jnp.dot(a, b, preferred_element_type=F32)


def _dot_nt(a, b):
    return lax.dot_general(a, b, (((1,), (1,)), ((), ())), preferred_element_type=F32)


def _dot_tn(a, b):
    return lax.dot_general(a, b, (((0,), (0,)), ((), ())), preferred_element_type=F32)


def _rope_table_kernel(pos_ref, inv_ref, c128_ref, s128_ref, c64t_ref, s64t_ref, c32t_ref,
                       s32t_ref, ct_ref, st_ref):
    ang_t = inv_ref[...] * pos_ref[...].astype(F32)
    cos_t = jnp.cos(ang_t)
    sin_t = jnp.sin(ang_t)
    c128_ref[...] = cos_t.T
    s128_ref[...] = sin_t.T
    for c in range(ct_ref.shape[0]):
        lanes = slice(c * LANES, (c + 1) * LANES)
        ct_ref[c] = cos_t[:, lanes]
        st_ref[c] = sin_t[:, lanes]
        for step, c_ref, s_ref in ((2, c64t_ref, s64t_ref), (4, c32t_ref, s32t_ref)):
            rows = pl.ds(0, LANES // step, stride=step)
            c_ref[:, lanes] = ct_ref[c, rows, :]
            s_ref[:, lanes] = st_ref[c, rows, :]


def _rope_tables(positions):
    t = positions.size
    tm = min(ROW_TILE, t)
    inv = ROPE_THETA ** (-jnp.arange(LANES, dtype=F32) / LANES)
    row = pl.BlockSpec((tm, LANES), lambda i: (i, 0))
    col = lambda n: pl.BlockSpec((n, tm), lambda i: (0, i))
    tab = lambda n: jax.ShapeDtypeStruct((n, t), F32)
    return pl.pallas_call(
        _rope_table_kernel,
        grid=(t // tm,),
        in_specs=[pl.BlockSpec((1, tm), lambda i: (0, i)), _resident((LANES, 1))],
        out_specs=[row, row, col(64), col(64), col(32), col(32)],
        out_shape=[jax.ShapeDtypeStruct((t, LANES), F32)] * 2 + [tab(64), tab(64), tab(32),
                                                                  tab(32)],
        scratch_shapes=[pltpu.VMEM((tm // LANES, LANES, LANES), F32)] * 2,
        compiler_params=_params("parallel"),
        name="rope_tables",
    )(positions.reshape(1, t), inv.reshape(LANES, 1))


def _ret_inproj_kernel(x_ref, gain_ref, w_ref, cos_ref, sin_ref, q_ref, k_ref, v_ref, g_ref,
                       *, heads, dk, dv, kscale):
    hn = _rms(x_ref[...], gain_ref[...]).astype(BF16)
    cos = cos_ref[...]
    sin = sin_ref[...]
    half = dk // 2
    width = heads * dv
    for off, out_ref, swish in ((2 * heads * dk + width, g_ref, True),
                                (2 * heads * dk, v_ref, False)):
        for c in range(width // dv):
            z = _dot(hn, w_ref[:, off + c * dv:off + (c + 1) * dv])
            if swish:
                z = z * (0.5 * jnp.tanh(0.5 * z) + 0.5)
            out_ref[:, c * dv:(c + 1) * dv] = z.astype(BF16)
    for h in range(heads):
        for off, out_ref, scale in ((0, q_ref, None), (heads * dk, k_ref, kscale)):
            z = _dot(hn, w_ref[:, off + h * dk:off + (h + 1) * dk])
            x1, x2 = z[:, :half], z[:, half:]
            o1 = x1 * cos - x2 * sin
            o2 = x2 * cos + x1 * sin
            if scale is not None:
                o1, o2 = o1 * scale, o2 * scale
            out_ref[:, h * dk:h * dk + half] = o1.astype(BF16)
            out_ref[:, h * dk + half:(h + 1) * dk] = o2.astype(BF16)


def _ret_core_kernel(q_ref, k_ref, v_ref, g_ref, dm_ref, qd_ref, kd_ref, cd_ref,
                     o_ref, state_ref, *, heads, chunk):
    @pl.when(pl.program_id(1) == 0)
    def _():
        state_ref[...] = jnp.zeros_like(state_ref)

    dk = q_ref.shape[2] // heads
    dv = v_ref.shape[2] // heads
    hs = range(heads)
    states = [state_ref[h] for h in hs]
    for c in range(q_ref.shape[1] // chunk):
        rows = slice(c * chunk, (c + 1) * chunk)
        q = [q_ref[0, rows, h * dk:(h + 1) * dk] for h in hs]
        k = [k_ref[0, rows, h * dk:(h + 1) * dk] for h in hs]
        v = [v_ref[0, rows, h * dv:(h + 1) * dv] for h in hs]
        scores = [_dot_nt(q[h], k[h]) for h in hs]
        kv = [_dot_tn((k[h].astype(F32) * kd_ref[h]).astype(BF16), v[h]) for h in hs]
        cross = [_dot(q[h], states[h].astype(BF16)) for h in hs]
        intra = [_dot((scores[h] * dm_ref[h]).astype(BF16), v[h]) for h in hs]
        for h in hs:
            o = intra[h] + cross[h] * qd_ref[h]
            y = o * lax.rsqrt(jnp.mean(o * o, axis=-1, keepdims=True) + EPS)
            gate = g_ref[0, rows, h * dv:(h + 1) * dv].astype(F32)
            o_ref[0, rows, h * dv:(h + 1) * dv] = (y * gate).astype(BF16)
            states[h] = states[h] * cd_ref[h] + kv[h]
    for h in hs:
        state_ref[h] = states[h]


def _retention_mixer(x2d, b, s, attn_gain, w_in, cos, sin):
    t, d = x2d.shape
    heads = RET_HEADS
    dk = d // heads
    dv = 2 * dk
    chunk = RET_CHUNK
    tm = min(ROW_TILE, t)
    w = w_in.astype(BF16)
    row = lambda width: pl.BlockSpec((tm, width), lambda i: (i, 0))
    q, k, v, g = pl.pallas_call(
        functools.partial(_ret_inproj_kernel, heads=heads, dk=dk, dv=dv, kscale=dk ** -0.5),
        grid=(t // tm,),
        in_specs=[row(d), _resident((1, d)), _resident(w.shape), row(LANES), row(LANES)],
        out_specs=[row(heads * dk), row(heads * dk), row(heads * dv), row(heads * dv)],
        out_shape=[jax.ShapeDtypeStruct((t, heads * dk), BF16),
                   jax.ShapeDtypeStruct((t, heads * dk), BF16),
                   jax.ShapeDtypeStruct((t, heads * dv), BF16),
                   jax.ShapeDtypeStruct((t, heads * dv), BF16)],
        compiler_params=_params("parallel"),
        name="ret_inproj",
    )(x2d, attn_gain.reshape(1, d), w, cos, sin)

    log_gamma = jnp.log1p(-(2.0 ** (-5.0 - jnp.arange(heads, dtype=F32))))
    i = jnp.arange(chunk, dtype=F32)
    diff = i[:, None] - i[None, :]
    dm = jnp.where(diff >= 0, jnp.exp(log_gamma[:, None, None] * jnp.maximum(diff, 0.0)), 0.0)
    qd = jnp.exp(log_gamma[:, None] * (i + 1.0))
    kd = jnp.exp(log_gamma[:, None] * (chunk - 1.0 - i))
    cd = jnp.exp(log_gamma * chunk)
    qd = jnp.broadcast_to(qd[:, :, None], (heads, chunk, dv))
    kd = jnp.broadcast_to(kd[:, :, None], (heads, chunk, dk))
    cd = jnp.broadcast_to(cd[:, None, None], (heads, 1, dv))

    rb = min(RET_ROW_BLOCK, s)
    seq = lambda width: pl.BlockSpec((1, rb, width), lambda bi, r: (bi, r, 0))
    o = pl.pallas_call(
        functools.partial(_ret_core_kernel, heads=heads, chunk=chunk),
        grid=(b, s // rb),
        in_specs=[seq(heads * dk), seq(heads * dk), seq(heads * dv), seq(heads * dv),
                  _resident(dm.shape), _resident(qd.shape), _resident(kd.shape),
                  _resident(cd.shape)],
        out_specs=seq(heads * dv),
        out_shape=jax.ShapeDtypeStruct((b, s, heads * dv), BF16),
        scratch_shapes=[pltpu.VMEM((heads, dk, dv), F32)],
        compiler_params=_params("parallel", "arbitrary"),
        name="ret_core",
    )(q.reshape(b, s, -1), k.reshape(b, s, -1), v.reshape(b, s, -1), g.reshape(b, s, -1),
      dm, qd, kd, cd)
    return o.reshape(t, heads * dv)


def _tail_kernel(a_ref, x_ref, wo_ref, gain_ref, wup_ref, wdn_ref, o_ref, *, ff_chunk):
    x1 = x_ref[...] + _dot(a_ref[...], wo_ref[...])
    hn = _rms(x1, gain_ref[...]).astype(BF16)
    acc = x1
    for c in range(wup_ref.shape[1] // ff_chunk):
        cols = slice(c * ff_chunk, (c + 1) * ff_chunk)
        u = jnp.maximum(_dot(hn, wup_ref[:, cols]), 0.0)
        acc = acc + _dot((u * u).astype(BF16), wdn_ref[cols, :])
    o_ref[...] = acc


def _mixer_out_and_mlp(a2d, x2d, w_out, mlp_gain, w_up, w_down):
    t, d = x2d.shape
    tm = min(ROW_TILE, t)
    wo, wu, wd = w_out.astype(BF16), w_up.astype(BF16), w_down.astype(BF16)
    row = lambda width: pl.BlockSpec((tm, width), lambda i: (i, 0))
    return pl.pallas_call(
        functools.partial(_tail_kernel, ff_chunk=1024),
        grid=(t // tm,),
        in_specs=[row(a2d.shape[1]), row(d), _resident(wo.shape), _resident((1, d)),
                  _resident(wu.shape), _resident(wd.shape)],
        out_specs=row(d),
        out_shape=jax.ShapeDtypeStruct((t, d), F32),
        compiler_params=_params("parallel"),
        name="outproj_mlp",
    )(a2d, x2d, wo, mlp_gain.reshape(1, d), wu, wd)


def _norm_rope_rows(z, gain, cos, sin):
    if gain is not None:
        z = z * lax.rsqrt(jnp.mean(z * z, axis=0, keepdims=True) + EPS) * gain
    half = z.shape[0] // 2
    x1, x2 = z[:half], z[half:]
    return x1 * cos - x2 * sin, x2 * cos + x1 * sin


def _dsa_inproj_kernel(x_ref, gain_ref, wt_ref, qg_ref, kg_ref, kig_ref, c64_ref, s64_ref,
                       c32_ref, s32_ref, qt_ref, k_ref, vt_ref, qit_ref, ki_ref, wit_ref, buf_ref,
                       *, heads, kv_heads, dh, idx_heads, di, wscale):
    hn = _rms(x_ref[...], gain_ref[...]).astype(BF16)
    tm = hn.shape[0]
    proj = _dot_nt(wt_ref[...], hn)
    c64, s64 = c64_ref[...], s64_ref[...]
    c32, s32 = c32_ref[...], s32_ref[...]
    qg = jnp.broadcast_to(qg_ref[...], (dh, tm))
    kg = jnp.broadcast_to(kg_ref[...], (dh, tm))
    for h in range(heads):
        o1, o2 = _norm_rope_rows(proj[h * dh:(h + 1) * dh], qg, c64, s64)
        qt_ref[0, h * dh:h * dh + dh // 2, :] = o1.astype(BF16)
        qt_ref[0, h * dh + dh // 2:(h + 1) * dh, :] = o2.astype(BF16)
    off = heads * dh
    for h in range(kv_heads):
        o1, o2 = _norm_rope_rows(proj[off + h * dh:off + (h + 1) * dh], kg, c64, s64)
        k_ref[:, h * dh:(h + 1) * dh] = jnp.concatenate([o1, o2], axis=0).T.astype(BF16)
    off += kv_heads * dh
    vrows = vt_ref.shape[1] // kv_heads
    for g in range(kv_heads):
        vt_ref[0, g * vrows:g * vrows + dh, :] = proj[off + g * dh:off + (g + 1) * dh].astype(BF16)
        vt_ref[0, g * vrows + dh:(g + 1) * vrows, :] = jnp.ones((vrows - dh, tm), BF16)
    off += kv_heads * dh
    for h in range(idx_heads):
        o1, o2 = _norm_rope_rows(proj[off + h * di:off + (h + 1) * di], None, c32, s32)
        qit_ref[0, h * di:h * di + di // 2, :] = o1.astype(BF16)
        qit_ref[0, h * di + di // 2:(h + 1) * di, :] = o2.astype(BF16)
    off += idx_heads * di
    kig = jnp.broadcast_to(kig_ref[...], (di, tm))
    o1, o2 = _norm_rope_rows(proj[off:off + di], kig, c32, s32)
    ki_t = jnp.concatenate([o1, o2, proj[off + di:off + LANES]], axis=0)
    ki_ref[...] = ki_t.T[:, :di].astype(BF16)
    off += LANES
    wit_ref[0] = proj[off:off + idx_heads] * wscale
    buf_ref[...] = jnp.zeros_like(buf_ref)


def _dsa_attn_kernel(*refs, tile, topk, heads, kv_heads, dh, idx_heads, index_steps):
    qt_ref, k_ref, vt_ref, qit_ref, ki_ref, wit_ref = refs[:6]
    o_ref, sc_ref, lg_ref, e_ref, cap_ref = refs[-5:]
    tq = qt_ref.shape[2]
    ke = k_ref.shape[1]
    di = qit_ref.shape[1] // idx_heads
    kb = KEY_BLOCK
    slabs = kb // SUBLANES
    j = tile
    nblk = ke // kb
    neg = float(jnp.finfo(F32).min)
    big = float(jnp.finfo(F32).max)
    diag = (lax.broadcasted_iota(jnp.int32, (tq, tq), 0)
            <= lax.broadcasted_iota(jnp.int32, (tq, tq), 1))

    def rows_of(i):
        return pl.ds(pl.multiple_of(i * kb, kb), kb)

    def full(val):
        return jnp.full((SUBLANES, tq), val, F32)

    def spread(row):
        return jnp.broadcast_to(row, (SUBLANES, tq))

    group = heads // kv_heads

    def raw_logits(head, i):
        g = head // group
        lg_ref[head, rows_of(i), :] = _dot(k_ref[0, rows_of(i), g * dh:(g + 1) * dh],
                                           qt_ref[0, head * dh:(head + 1) * dh, :])

    def sweep(fn, init, head=None, unrolled=False):
        unroll = nblk if unrolled else max(u for u in range(1, SWEEP_UNROLL + 1) if nblk % u == 0)
        acc = lax.fori_loop(0, nblk, lambda i, a: fn(i, sc_ref[rows_of(i), :], a), init,
                            unroll=unroll)
        if head is not None:
            g = head // group
            lg = _dot(k_ref[0, :, g * dh:(g + 1) * dh], qt_ref[0, head * dh:(head + 1) * dh, :])
            lg_ref[head] = lg
            acc = acc + jnp.minimum(jnp.maximum(lg[ke - SUBLANES:, :], 0.0), 0.0)
        return acc

    if ke <= topk:
        sc_ref[...] = jnp.where(diag, 0.0, neg)
        for h in range(heads):
            lax.fori_loop(0, nblk, lambda i, c, h=h: (raw_logits(h, i), c)[1], 0, unroll=True)
    else:
        wit = wit_ref[0]
        ki = ki_ref[0]
        score = jnp.zeros((ke, tq), F32)
        for h in range(idx_heads):
            rel = jnp.maximum(_dot(ki, qit_ref[0, h * di:(h + 1) * di, :]), 0.0)
            score = score + rel * wit[h:h + 1, :]
        last = score[ke - tq:, :]
        row_max = jnp.max(jnp.where(diag, last, neg), axis=0, keepdims=True)
        row_min = jnp.min(jnp.where(diag, last, big), axis=0, keepdims=True)
        if ke > tq:
            row_max = jnp.maximum(row_max, jnp.max(score[:ke - tq, :], axis=0, keepdims=True))
            row_min = jnp.minimum(row_min, jnp.min(score[:ke - tq, :], axis=0, keepdims=True))
            sc_ref[:ke - tq, :] = score[:ke - tq, :]
        sc_ref[ke - tq:, :] = jnp.where(diag, last, neg)
        q_pos = j * tq + lax.broadcasted_iota(jnp.int32, (1, tq), 1)
        kvec = jnp.minimum(q_pos + 1, topk).astype(F32)

        def count_ge(th, head=None, unrolled=False):
            thb = spread(th)[None]
            acc = sweep(lambda i, blk, a: a + jnp.sum(
                jnp.where(blk.reshape(slabs, SUBLANES, tq) >= thb, 1.0, 0.0), axis=0),
                jnp.zeros((SUBLANES, tq), F32), head, unrolled)
            return jnp.sum(acc, axis=0, keepdims=True)

        def max_below(hi):
            hib = spread(hi)[None]

            def step(i, blk, a):
                blk = blk.reshape(slabs, SUBLANES, tq)
                return jnp.maximum(a, jnp.max(jnp.where(blk < hib, blk, neg), axis=0))

            return jnp.max(sweep(step, full(neg)), axis=0, keepdims=True)

        steps = max(BISECT_STEPS, heads)
        carried = {(h * steps) // heads: h for h in range(heads)}
        lo, hi, c_hi = row_min, row_max + jnp.abs(row_max) * 1e-3 + 1e-3, jnp.zeros((1, tq), F32)
        for step in range(steps):
            mid = lo * 0.5 + hi * 0.5
            c = count_ge(mid, carried.get(step), unrolled=True)
            ge = c >= kvec
            lo, hi, c_hi = jnp.where(ge, mid, lo), jnp.where(ge, hi, mid), jnp.where(ge, c_hi, c)

        def snap(hi):
            v = max_below(hi)
            return v, count_ge(v)

        def pending(cv):
            return jnp.max(jnp.where(cv < kvec, 1.0, 0.0))

        def refine(state):
            hi, c_hi, v, cv, _ = state
            short = cv < kvec
            hi = jnp.where(short, v, hi)
            c_hi = jnp.where(short, cv, c_hi)
            v, cv = snap(hi)
            return hi, c_hi, v, cv, pending(cv)

        v, cv = snap(hi)
        _, c_hi, v, cv, _ = lax.while_loop(lambda st: st[4] > 0.0, refine,
                                           (hi, c_hi, v, cv, pending(cv)))
        need = kvec - c_hi
        surplus = jnp.max(jnp.where(cv > kvec, 1.0, 0.0)) > 0.0
        vb = spread(v)

        @pl.when(jnp.logical_not(surplus))
        def _():
            def to_bias(i, carry):
                blk = sc_ref[rows_of(i), :].reshape(slabs, SUBLANES, tq)
                sc_ref[rows_of(i), :] = jnp.where(blk >= vb[None], 0.0, neg).reshape(kb, tq)
                return carry
            lax.fori_loop(0, nblk, to_bias, 0)

        @pl.when(surplus)
        def _():
            def key_idx(i):
                return (lax.broadcasted_iota(jnp.int32, (slabs, SUBLANES, tq), 0) * SUBLANES
                        + lax.broadcasted_iota(jnp.int32, (slabs, SUBLANES, tq), 1) + i * kb)

            def idx_step(_, carry):
                lo_i, hi_i = carry
                mid_i = (lo_i + hi_i) >> 1
                midb = jnp.broadcast_to(mid_i, (SUBLANES, tq))
                acc = sweep(lambda i, blk, a: a + jnp.sum(jnp.where(
                    (blk.reshape(slabs, SUBLANES, tq) == vb[None]) & (key_idx(i) <= midb[None]),
                    1.0, 0.0), axis=0), jnp.zeros((SUBLANES, tq), F32))
                ok = jnp.sum(acc, axis=0, keepdims=True) >= need
                return jnp.where(ok, lo_i, mid_i), jnp.where(ok, mid_i, hi_i)

            _, istar = lax.fori_loop(0, index_steps, idx_step,
                                     (jnp.full((1, tq), -1, jnp.int32),
                                      jnp.full((1, tq), nblk * kb - 1, jnp.int32)))
            istarb = jnp.broadcast_to(istar, (SUBLANES, tq))

            def to_bias(i, carry):
                blk = sc_ref[rows_of(i), :].reshape(slabs, SUBLANES, tq)
                keep = (blk > vb[None]) | ((blk == vb[None]) & (key_idx(i) <= istarb[None]))
                sc_ref[rows_of(i), :] = jnp.where(keep, 0.0, neg).reshape(kb, tq)
                return carry
            lax.fori_loop(0, nblk, to_bias, 0)

    def fold(x, op):
        return op(x.reshape(x.shape[0] // SUBLANES, SUBLANES, tq), axis=0)

    cap_val = float(jnp.finfo(BF16).max)

    def max_pass(i, maxima):
        rows = rows_of(i)
        bias = sc_ref[rows, :]
        cap_ref[rows, :] = jnp.where(bias == 0.0, cap_val, 0.0).astype(BF16)
        return tuple(jnp.maximum(maxima[h], fold(lg_ref[h, rows, :] + bias, jnp.max))
                     for h in range(heads))

    maxima = lax.fori_loop(0, nblk, max_pass, (full(neg),) * heads, unroll=min(nblk, 2))

    vrows = vt_ref.shape[1] // kv_heads
    for h in range(heads):
        g = h // group
        m = jnp.max(maxima[h], axis=0, keepdims=True)
        for c in range(ke // tq):
            rows = slice(c * tq, (c + 1) * tq)
            e = jnp.exp2(lg_ref[h, rows, :] - m).astype(BF16)
            e_ref[h, rows, :] = jnp.minimum(e, cap_ref[rows, :])
        out_t = _dot(vt_ref[0, g * vrows:(g + 1) * vrows, :], e_ref[h])
        o_ref[0, :, h * dh:(h + 1) * dh] = (out_t[:dh] / out_t[dh:dh + 1]).T.astype(BF16)


def _sparse_attention_mixer(x2d, b, s, attn_gain, w_in, q_gain, k_gain, kidx_gain, tabs):
    t, d = x2d.shape
    heads, kv_heads, idx_heads, di = ATT_HEADS, ATT_KV_HEADS, IDX_HEADS, IDX_DH
    dh = d // heads
    c64, s64, c32, s32 = tabs
    tm = min(ROW_TILE, s)
    per_b = s // tm
    nq, nkv, nqi = heads * dh, kv_heads * dh, idx_heads * di
    nvt = kv_heads * (dh + 2 * SUBLANES)
    n_main = nq + 2 * nkv + nqi + di
    zeros = lambda n: jnp.zeros((d, n), w_in.dtype)
    wt = jnp.concatenate([w_in[:, :n_main], zeros(LANES - di), w_in[:, n_main:n_main + idx_heads],
                          zeros(2 * SUBLANES - idx_heads)], axis=1).T.astype(BF16)
    qg = (q_gain * (dh ** -0.5 * math.log2(math.e))).reshape(dh, 1)
    row = lambda width: pl.BlockSpec((tm, width), lambda i: (i, 0))
    col = lambda n: pl.BlockSpec((n, tm), lambda i: (0, i))
    seq_t = lambda n: pl.BlockSpec((1, n, tm), lambda i: (i // per_b, 0, i % per_b))
    shp = lambda width, dt: jax.ShapeDtypeStruct((t, width), dt)
    shp_t = lambda n, dt: jax.ShapeDtypeStruct((b, n, s), dt)
    qt, k, vt, qit, ki, wit, out = pl.pallas_call(
        functools.partial(_dsa_inproj_kernel, heads=heads, kv_heads=kv_heads, dh=dh,
                          idx_heads=idx_heads, di=di, wscale=idx_heads ** -0.5 * di ** -0.5),
        grid=(t // tm,),
        in_specs=[row(d), _resident((1, d)), _resident(wt.shape), _resident((dh, 1)),
                  _resident((dh, 1)), _resident((di, 1)), col(dh // 2), col(dh // 2),
                  col(di // 2), col(di // 2)],
        out_specs=[seq_t(nq), row(nkv), seq_t(nvt), seq_t(nqi), row(di), seq_t(idx_heads),
                   row(nq)],
        out_shape=[shp_t(nq, BF16), shp(nkv, BF16), shp_t(nvt, BF16), shp_t(nqi, BF16),
                   shp(di, BF16), shp_t(idx_heads, F32), shp(nq, BF16)],
        compiler_params=_params("parallel"),
        name="dsa_inproj",
    )(x2d, attn_gain.reshape(1, d), wt, qg, k_gain.reshape(dh, 1), kidx_gain.reshape(di, 1),
      c64, s64, c32, s32)

    topk = min(TOPK_MAX, s // 4)
    tq = min(Q_TILE, s)
    k, ki = k.reshape(b, s, nkv), ki.reshape(b, s, di)
    out = out.reshape(b, s, nq)
    for tile in range(s // tq):
        ke = (tile + 1) * tq
        operands = [qt, k, vt, qit, ki, wit]
        in_specs = [pl.BlockSpec((1, nq, tq), lambda bi, tile=tile: (bi, 0, tile)),
                    pl.BlockSpec((1, ke, nkv), lambda bi: (bi, 0, 0)),
                    pl.BlockSpec((1, nvt, ke), lambda bi: (bi, 0, 0)),
                    pl.BlockSpec((1, nqi, tq), lambda bi, tile=tile: (bi, 0, tile)),
                    pl.BlockSpec((1, ke, di), lambda bi: (bi, 0, 0)),
                    pl.BlockSpec((1, idx_heads, tq), lambda bi, tile=tile: (bi, 0, tile))]
        operands.append(out)
        in_specs.append(pl.BlockSpec(memory_space=pl.ANY))
        out = pl.pallas_call(
            functools.partial(_dsa_attn_kernel, tile=tile, topk=topk, heads=heads,
                              kv_heads=kv_heads, dh=dh, idx_heads=idx_heads,
                              index_steps=(ke - 1).bit_length() + 1),
            grid=(b,),
            in_specs=in_specs,
            out_specs=pl.BlockSpec((1, tq, nq), lambda bi, tile=tile: (bi, tile, 0)),
            out_shape=jax.ShapeDtypeStruct((b, s, nq), BF16),
            scratch_shapes=[pltpu.VMEM((ke, tq), F32), pltpu.VMEM((heads, ke, tq), F32),
                            pltpu.VMEM((heads, ke, tq), BF16), pltpu.VMEM((ke, tq), BF16)],
            input_output_aliases={len(operands) - 1: 0},
            compiler_params=_params("parallel"),
            name=f"dsa_attn_{tile}",
        )(*operands)
    return out.reshape(t, nq)


def kernel(x, positions, attn_norm, ret_w_in, ret_out_norm, ret_w_out, dsa_w_in, dsa_q_norm,
           dsa_k_norm, dsa_kidx_norm, dsa_w_out, mlp_norm, mlp_w_up, mlp_w_down):
    b, s, d = x.shape
    depth = attn_norm.shape[0]
    c128, s128, c64, s64, c32, s32 = _rope_tables(positions)
    x2d = x.reshape(b * s, d)
    for i in range(depth):
        j = i // 2
        if i % 2 == 0:
            a = _retention_mixer(x2d, b, s, attn_norm[i], ret_w_in[j], c128, s128)
            w_out = ret_out_norm[j].reshape(-1, 1) * ret_w_out[j]
        else:
            a = _sparse_attention_mixer(x2d, b, s, attn_norm[i], dsa_w_in[j], dsa_q_norm[j],
                                        dsa_k_norm[j], dsa_kidx_norm[j], (c64, s64, c32, s32))
            w_out = dsa_w_out[j]
        x2d = _mixer_out_and_mlp(a, x2d, w_out, mlp_norm[i], mlp_w_up[i], mlp_w_down[i])
    return x2d.reshape(b, s, d)
```

```python
import functools
import math

import jax
import jax.numpy as jnp
from jax import lax
from jax.experimental import pallas as pl
from jax.experimental.pallas import tpu as pltpu

F32 = jnp.float32
BF16 = jnp.bfloat16

EPS = 1e-6
ROPE_THETA = 10000.0
RET_HEADS = 4
RET_CHUNK = 256
ATT_HEADS = 8
ATT_KV_HEADS = 2
IDX_HEADS = 8
IDX_DH = 64
TOPK_MAX = 256
LANES = 128
SUBLANES = 8

ROW_TILE = 512
RET_ROW_BLOCK = 1024
Q_TILE = 256
KEY_BLOCK = 128
SWEEP_UNROLL = 8
VMEM_LIMIT = 56 * 1024 * 1024
BISECT_STEPS = 13


def _params(*sem):
    return pltpu.CompilerParams(dimension_semantics=sem, vmem_limit_bytes=VMEM_LIMIT)


def _resident(shape):
    nd = len(shape)
    return pl.BlockSpec(shape, lambda *_: (0,) * nd, pipeline_mode=pl.Buffered(1))


def _rms(x, gain):
    return x * lax.rsqrt(jnp.mean(x * x, axis=-1, keepdims=True) + EPS) * gain


def _dot(a, b):
    return jnp.dot(a, b, preferred_element_type=F32)


def _dot_nt(a, b):
    return lax.dot_general(a, b, (((1,), (1,)), ((), ())), preferred_element_type=F32)


def _dot_tn(a, b):
    return lax.dot_general(a, b, (((0,), (0,)), ((), ())), preferred_element_type=F32)


def _rope_table_kernel(pos_ref, inv_ref, c128_ref, s128_ref, c64t_ref, s64t_ref, c32t_ref,
                       s32t_ref, ct_ref, st_ref):
    ang_t = inv_ref[...] * pos_ref[...].astype(F32)
    cos_t = jnp.cos(ang_t)
    sin_t = jnp.sin(ang_t)
    c128_ref[...] = cos_t.T
    s128_ref[...] = sin_t.T
    for c in range(ct_ref.shape[0]):
        lanes = slice(c * LANES, (c + 1) * LANES)
        ct_ref[c] = cos_t[:, lanes]
        st_ref[c] = sin_t[:, lanes]
        for step, c_ref, s_ref in ((2, c64t_ref, s64t_ref), (4, c32t_ref, s32t_ref)):
            rows = pl.ds(0, LANES // step, stride=step)
            c_ref[:, lanes] = ct_ref[c, rows, :]
            s_ref[:, lanes] = st_ref[c, rows, :]


def _rope_tables(positions):
    t = positions.size
    tm = min(ROW_TILE, t)
    inv = ROPE_THETA ** (-jnp.arange(LANES, dtype=F32) / LANES)
    row = pl.BlockSpec((tm, LANES), lambda i: (i, 0))
    col = lambda n: pl.BlockSpec((n, tm), lambda i: (0, i))
    tab = lambda n: jax.ShapeDtypeStruct((n, t), F32)
    return pl.pallas_call(
        _rope_table_kernel,
        grid=(t // tm,),
        in_specs=[pl.BlockSpec((1, tm), lambda i: (0, i)), _resident((LANES, 1))],
        out_specs=[row, row, col(64), col(64), col(32), col(32)],
        out_shape=[jax.ShapeDtypeStruct((t, LANES), F32)] * 2 + [tab(64), tab(64), tab(32),
                                                                  tab(32)],
        scratch_shapes=[pltpu.VMEM((tm // LANES, LANES, LANES), F32)] * 2,
        compiler_params=_params("parallel"),
        name="rope_tables",
    )(positions.reshape(1, t), inv.reshape(LANES, 1))


def _ret_inproj_kernel(x_ref, gain_ref, w_ref, cos_ref, sin_ref, q_ref, k_ref, v_ref, g_ref,
                       *, heads, dk, dv, kscale):
    hn = _rms(x_ref[...], gain_ref[...]).astype(BF16)
    cos = cos_ref[...]
    sin = sin_ref[...]
    half = dk // 2
    width = heads * dv
    for off, out_ref, swish in ((2 * heads * dk + width, g_ref, True),
                                (2 * heads * dk, v_ref, False)):
        for c in range(width // dv):
            z = _dot(hn, w_ref[:, off + c * dv:off + (c + 1) * dv].astype(BF16))
            if swish:
                z = z * (0.5 * jnp.tanh(0.5 * z) + 0.5)
            out_ref[:, c * dv:(c + 1) * dv] = z.astype(BF16)
    for h in range(heads):
        for off, out_ref, scale in ((0, q_ref, None), (heads * dk, k_ref, kscale)):
            z = _dot(hn, w_ref[:, off + h * dk:off + (h + 1) * dk].astype(BF16))
            x1, x2 = z[:, :half], z[:, half:]
            o1 = x1 * cos - x2 * sin
            o2 = x2 * cos + x1 * sin
            if scale is not None:
                o1, o2 = o1 * scale, o2 * scale
            out_ref[:, h * dk:h * dk + half] = o1.astype(BF16)
            out_ref[:, h * dk + half:(h + 1) * dk] = o2.astype(BF16)


def _ret_core_kernel(q_ref, k_ref, v_ref, g_ref, dm_ref, qd_ref, kd_ref, cd_ref,
                     o_ref, state_ref, *, heads, chunk):
    @pl.when(pl.program_id(1) == 0)
    def _():
        state_ref[...] = jnp.zeros_like(state_ref)

    dk = q_ref.shape[2] // heads
    dv = v_ref.shape[2] // heads
    hs = range(heads)
    states = [state_ref[h] for h in hs]
    for c in range(q_ref.shape[1] // chunk):
        rows = slice(c * chunk, (c + 1) * chunk)
        q = [q_ref[0, rows, h * dk:(h + 1) * dk] for h in hs]
        k = [k_ref[0, rows, h * dk:(h + 1) * dk] for h in hs]
        v = [v_ref[0, rows, h * dv:(h + 1) * dv] for h in hs]
        scores = [_dot_nt(q[h], k[h]) for h in hs]
        kv = [_dot_tn((k[h].astype(F32) * kd_ref[h]).astype(BF16), v[h]) for h in hs]
        cross = [_dot(q[h], states[h].astype(BF16)) for h in hs]
        intra = [_dot((scores[h] * dm_ref[h]).astype(BF16), v[h]) for h in hs]
        for h in hs:
            o = intra[h] + cross[h] * qd_ref[h]
            y = o * lax.rsqrt(jnp.mean(o * o, axis=-1, keepdims=True) + EPS)
            gate = g_ref[0, rows, h * dv:(h + 1) * dv].astype(F32)
            o_ref[0, rows, h * dv:(h + 1) * dv] = (y * gate).astype(BF16)
            states[h] = states[h] * cd_ref[h] + kv[h]
    for h in hs:
        state_ref[h] = states[h]


def _retention_mixer(x2d, b, s, attn_gain, w_in, cos, sin):
    t, d = x2d.shape
    heads = RET_HEADS
    dk = d // heads
    dv = 2 * dk
    chunk = RET_CHUNK
    tm = min(ROW_TILE, t)
    w = w_in
    row = lambda width: pl.BlockSpec((tm, width), lambda i: (i, 0))
    q, k, v, g = pl.pallas_call(
        functools.partial(_ret_inproj_kernel, heads=heads, dk=dk, dv=dv, kscale=dk ** -0.5),
        grid=(t // tm,),
        in_specs=[row(d), _resident((1, d)), _resident(w.shape), row(LANES), row(LANES)],
        out_specs=[row(heads * dk), row(heads * dk), row(heads * dv), row(heads * dv)],
        out_shape=[jax.ShapeDtypeStruct((t, heads * dk), BF16),
                   jax.ShapeDtypeStruct((t, heads * dk), BF16),
                   jax.ShapeDtypeStruct((t, heads * dv), BF16),
                   jax.ShapeDtypeStruct((t, heads * dv), BF16)],
        compiler_params=_params("parallel"),
        name="ret_inproj",
    )(x2d, attn_gain.reshape(1, d), w, cos, sin)

    log_gamma = jnp.log1p(-(2.0 ** (-5.0 - jnp.arange(heads, dtype=F32))))
    i = jnp.arange(chunk, dtype=F32)
    diff = i[:, None] - i[None, :]
    dm = jnp.where(diff >= 0, jnp.exp(log_gamma[:, None, None] * jnp.maximum(diff, 0.0)), 0.0)
    qd = jnp.exp(log_gamma[:, None] * (i + 1.0))
    kd = jnp.exp(log_gamma[:, None] * (chunk - 1.0 - i))
    cd = jnp.exp(log_gamma * chunk)
    qd = jnp.broadcast_to(qd[:, :, None], (heads, chunk, dv))
    kd = jnp.broadcast_to(kd[:, :, None], (heads, chunk, dk))
    cd = jnp.broadcast_to(cd[:, None, None], (heads, 1, dv))

    rb = min(RET_ROW_BLOCK, s)
    seq = lambda width: pl.BlockSpec((1, rb, width), lambda bi, r: (bi, r, 0))
    o = pl.pallas_call(
        functools.partial(_ret_core_kernel, heads=heads, chunk=chunk),
        grid=(b, s // rb),
        in_specs=[seq(heads * dk), seq(heads * dk), seq(heads * dv), seq(heads * dv),
                  _resident(dm.shape), _resident(qd.shape), _resident(kd.shape),
                  _resident(cd.shape)],
        out_specs=seq(heads * dv),
        out_shape=jax.ShapeDtypeStruct((b, s, heads * dv), BF16),
        scratch_shapes=[pltpu.VMEM((heads, dk, dv), F32)],
        compiler_params=_params("parallel", "arbitrary"),
        name="ret_core",
    )(q.reshape(b, s, -1), k.reshape(b, s, -1), v.reshape(b, s, -1), g.reshape(b, s, -1),
      dm, qd, kd, cd)
    return o.reshape(t, heads * dv)


def _tail_kernel(a_ref, x_ref, wo_ref, gain_ref, wup_ref, wdn_ref, o_ref, *, ff_chunk):
    x1 = x_ref[...] + _dot(a_ref[...], wo_ref[...])
    hn = _rms(x1, gain_ref[...]).astype(BF16)
    acc = x1
    for c in range(wup_ref.shape[1] // ff_chunk):
        cols = slice(c * ff_chunk, (c + 1) * ff_chunk)
        u = jnp.maximum(_dot(hn, wup_ref[:, cols]), 0.0)
        acc = acc + _dot((u * u).astype(BF16), wdn_ref[cols, :])
    o_ref[...] = acc


def _mixer_out_and_mlp(a2d, x2d, w_out, mlp_gain, w_up, w_down):
    t, d = x2d.shape
    tm = min(ROW_TILE, t)
    wo, wu, wd = w_out.astype(BF16), w_up.astype(BF16), w_down.astype(BF16)
    row = lambda width: pl.BlockSpec((tm, width), lambda i: (i, 0))
    return pl.pallas_call(
        functools.partial(_tail_kernel, ff_chunk=1024),
        grid=(t // tm,),
        in_specs=[row(a2d.shape[1]), row(d), _resident(wo.shape), _resident((1, d)),
                  _resident(wu.shape), _resident(wd.shape)],
        out_specs=row(d),
        out_shape=jax.ShapeDtypeStruct((t, d), F32),
        compiler_params=_params("parallel"),
        name="outproj_mlp",
    )(a2d, x2d, wo, mlp_gain.reshape(1, d), wu, wd)


def _norm_rope_rows(z, gain, cos, sin):
    if gain is not None:
        z = z * lax.rsqrt(jnp.mean(z * z, axis=0, keepdims=True) + EPS) * gain
    half = z.shape[0] // 2
    x1, x2 = z[:half], z[half:]
    return x1 * cos - x2 * sin, x2 * cos + x1 * sin


def _dsa_inproj_kernel(x_ref, gain_ref, wt_ref, qg_ref, kg_ref, kig_ref, c64_ref, s64_ref,
                       c32_ref, s32_ref, qt_ref, k_ref, vt_ref, qit_ref, ki_ref, wit_ref, buf_ref,
                       *, heads, kv_heads, dh, idx_heads, di, wscale):
    hn = _rms(x_ref[...], gain_ref[...]).astype(BF16)
    tm = hn.shape[0]
    proj = _dot_nt(wt_ref[...], hn)
    c64, s64 = c64_ref[...], s64_ref[...]
    c32, s32 = c32_ref[...], s32_ref[...]
    qg = jnp.broadcast_to(qg_ref[...], (dh, tm))
    kg = jnp.broadcast_to(kg_ref[...], (dh, tm))
    for h in range(heads):
        o1, o2 = _norm_rope_rows(proj[h * dh:(h + 1) * dh], qg, c64, s64)
        qt_ref[0, h * dh:h * dh + dh // 2, :] = o1.astype(BF16)
        qt_ref[0, h * dh + dh // 2:(h + 1) * dh, :] = o2.astype(BF16)
    off = heads * dh
    for h in range(kv_heads):
        o1, o2 = _norm_rope_rows(proj[off + h * dh:off + (h + 1) * dh], kg, c64, s64)
        k_ref[:, h * dh:(h + 1) * dh] = jnp.concatenate([o1, o2], axis=0).T.astype(BF16)
    off += kv_heads * dh
    vrows = vt_ref.shape[1] // kv_heads
    for g in range(kv_heads):
        vt_ref[0, g * vrows:g * vrows + dh, :] = proj[off + g * dh:off + (g + 1) * dh].astype(BF16)
        vt_ref[0, g * vrows + dh:(g + 1) * vrows, :] = jnp.ones((vrows - dh, tm), BF16)
    off += kv_heads * dh
    for h in range(idx_heads):
        o1, o2 = _norm_rope_rows(proj[off + h * di:off + (h + 1) * di], None, c32, s32)
        qit_ref[0, h * di:h * di + di // 2, :] = o1.astype(BF16)
        qit_ref[0, h * di + di // 2:(h + 1) * di, :] = o2.astype(BF16)
    off += idx_heads * di
    kig = jnp.broadcast_to(kig_ref[...], (di, tm))
    o1, o2 = _norm_rope_rows(proj[off:off + di], kig, c32, s32)
    ki_t = jnp.concatenate([o1, o2, proj[off + di:off + LANES]], axis=0)
    ki_ref[...] = ki_t.T[:, :di].astype(BF16)
    off += LANES
    wit_ref[0] = proj[off:off + idx_heads] * wscale
    buf_ref[...] = jnp.zeros_like(buf_ref)


def _dsa_attn_kernel(*refs, tile, topk, heads, kv_heads, dh, idx_heads, index_steps):
    qt_ref, k_ref, vt_ref, qit_ref, ki_ref, wit_ref = refs[:6]
    o_ref, sc_ref, lg_ref, e_ref, cap_ref = refs[-5:]
    tq = qt_ref.shape[2]
    ke = k_ref.shape[1]
    di = qit_ref.shape[1] // idx_heads
    kb = KEY_BLOCK
    slabs = kb // SUBLANES
    j = tile
    nblk = ke // kb
    neg = float(jnp.finfo(F32).min)
    big = float(jnp.finfo(F32).max)
    diag = (lax.broadcasted_iota(jnp.int32, (tq, tq), 0)
            <= lax.broadcasted_iota(jnp.int32, (tq, tq), 1))

    def rows_of(i):
        return pl.ds(pl.multiple_of(i * kb, kb), kb)

    def full(val):
        return jnp.full((SUBLANES, tq), val, F32)

    def spread(row):
        return jnp.broadcast_to(row, (SUBLANES, tq))

    group = heads // kv_heads

    def raw_logits(head, i):
        g = head // group
        lg_ref[head, rows_of(i), :] = _dot(k_ref[0, rows_of(i), g * dh:(g + 1) * dh],
                                           qt_ref[0, head * dh:(head + 1) * dh, :])

    def sweep(fn, init, head=None, unrolled=False):
        unroll = nblk if unrolled else max(u for u in range(1, SWEEP_UNROLL + 1) if nblk % u == 0)
        acc = lax.fori_loop(0, nblk, lambda i, a: fn(i, sc_ref[rows_of(i), :], a), init,
                            unroll=unroll)
        if head is not None:
            g = head // group
            lg = _dot(k_ref[0, :, g * dh:(g + 1) * dh], qt_ref[0, head * dh:(head + 1) * dh, :])
            lg_ref[head] = lg
            acc = acc + jnp.minimum(jnp.maximum(lg[ke - SUBLANES:, :], 0.0), 0.0)
        return acc

    if ke <= topk:
        sc_ref[...] = jnp.where(diag, 0.0, neg)
        for h in range(heads):
            lax.fori_loop(0, nblk, lambda i, c, h=h: (raw_logits(h, i), c)[1], 0, unroll=True)
    else:
        wit = wit_ref[0]
        ki = ki_ref[0]
        score = jnp.zeros((ke, tq), F32)
        for h in range(idx_heads):
            rel = jnp.maximum(_dot(ki, qit_ref[0, h * di:(h + 1) * di, :]), 0.0)
            score = score + rel * wit[h:h + 1, :]
        last = score[ke - tq:, :]
        row_max = jnp.max(jnp.where(diag, last, neg), axis=0, keepdims=True)
        row_min = jnp.min(jnp.where(diag, last, big), axis=0, keepdims=True)
        if ke > tq:
            row_max = jnp.maximum(row_max, jnp.max(score[:ke - tq, :], axis=0, keepdims=True))
            row_min = jnp.minimum(row_min, jnp.min(score[:ke - tq, :], axis=0, keepdims=True))
            sc_ref[:ke - tq, :] = score[:ke - tq, :]
        sc_ref[ke - tq:, :] = jnp.where(diag, last, neg)
        q_pos = j * tq + lax.broadcasted_iota(jnp.int32, (1, tq), 1)
        kvec = jnp.minimum(q_pos + 1, topk).astype(F32)

        def count_ge(th, head=None, unrolled=False):
            thb = spread(th)[None]
            acc = sweep(lambda i, blk, a: a + jnp.sum(
                jnp.where(blk.reshape(slabs, SUBLANES, tq) >= thb, 1.0, 0.0), axis=0),
                jnp.zeros((SUBLANES, tq), F32), head, unrolled)
            return jnp.sum(acc, axis=0, keepdims=True)

        def max_below(hi):
            hib = spread(hi)[None]

            def step(i, blk, a):
                blk = blk.reshape(slabs, SUBLANES, tq)
                return jnp.maximum(a, jnp.max(jnp.where(blk < hib, blk, neg), axis=0))

            return jnp.max(sweep(step, full(neg)), axis=0, keepdims=True)

        steps = max(BISECT_STEPS, heads)
        carried = {(h * steps) // heads: h for h in range(heads)}
        lo, hi, c_hi = row_min, row_max + jnp.abs(row_max) * 1e-3 + 1e-3, jnp.zeros((1, tq), F32)
        for step in range(steps):
            mid = lo * 0.5 + hi * 0.5
            c = count_ge(mid, carried.get(step), unrolled=True)
            ge = c >= kvec
            lo, hi, c_hi = jnp.where(ge, mid, lo), jnp.where(ge, hi, mid), jnp.where(ge, c_hi, c)

        def snap(hi):
            v = max_below(hi)
            return v, count_ge(v)

        def pending(cv):
            return jnp.max(jnp.where(cv < kvec, 1.0, 0.0))

        def refine(state):
            hi, c_hi, v, cv, _ = state
            short = cv < kvec
            hi = jnp.where(short, v, hi)
            c_hi = jnp.where(short, cv, c_hi)
            v, cv = snap(hi)
            return hi, c_hi, v, cv, pending(cv)

        v, cv = snap(hi)
        _, c_hi, v, cv, _ = lax.while_loop(lambda st: st[4] > 0.0, refine,
                                           (hi, c_hi, v, cv, pending(cv)))
        need = kvec - c_hi
        surplus = jnp.max(jnp.where(cv > kvec, 1.0, 0.0)) > 0.0
        vb = spread(v)

        @pl.when(jnp.logical_not(surplus))
        def _():
            def to_bias(i, carry):
                blk = sc_ref[rows_of(i), :].reshape(slabs, SUBLANES, tq)
                sc_ref[rows_of(i), :] = jnp.where(blk >= vb[None], 0.0, neg).reshape(kb, tq)
                return carry
            lax.fori_loop(0, nblk, to_bias, 0)

        @pl.when(surplus)
        def _():
            def key_idx(i):
                return (lax.broadcasted_iota(jnp.int32, (slabs, SUBLANES, tq), 0) * SUBLANES
                        + lax.broadcasted_iota(jnp.int32, (slabs, SUBLANES, tq), 1) + i * kb)

            def idx_step(_, carry):
                lo_i, hi_i = carry
                mid_i = (lo_i + hi_i) >> 1
                midb = jnp.broadcast_to(mid_i, (SUBLANES, tq))
                acc = sweep(lambda i, blk, a: a + jnp.sum(jnp.where(
                    (blk.reshape(slabs, SUBLANES, tq) == vb[None]) & (key_idx(i) <= midb[None]),
                    1.0, 0.0), axis=0), jnp.zeros((SUBLANES, tq), F32))
                ok = jnp.sum(acc, axis=0, keepdims=True) >= need
                return jnp.where(ok, lo_i, mid_i), jnp.where(ok, mid_i, hi_i)

            _, istar = lax.fori_loop(0, index_steps, idx_step,
                                     (jnp.full((1, tq), -1, jnp.int32),
                                      jnp.full((1, tq), nblk * kb - 1, jnp.int32)))
            istarb = jnp.broadcast_to(istar, (SUBLANES, tq))

            def to_bias(i, carry):
                blk = sc_ref[rows_of(i), :].reshape(slabs, SUBLANES, tq)
                keep = (blk > vb[None]) | ((blk == vb[None]) & (key_idx(i) <= istarb[None]))
                sc_ref[rows_of(i), :] = jnp.where(keep, 0.0, neg).reshape(kb, tq)
                return carry
            lax.fori_loop(0, nblk, to_bias, 0)

    def fold(x, op):
        return op(x.reshape(x.shape[0] // SUBLANES, SUBLANES, tq), axis=0)

    cap_val = float(jnp.finfo(BF16).max)

    def max_pass(i, maxima):
        rows = rows_of(i)
        bias = sc_ref[rows, :]
        cap_ref[rows, :] = jnp.where(bias == 0.0, cap_val, 0.0).astype(BF16)
        return tuple(jnp.maximum(maxima[h], fold(lg_ref[h, rows, :] + bias, jnp.max))
                     for h in range(heads))

    maxima = lax.fori_loop(0, nblk, max_pass, (full(neg),) * heads, unroll=min(nblk, 2))

    vrows = vt_ref.shape[1] // kv_heads
    for h in range(heads):
        g = h // group
        m = jnp.max(maxima[h], axis=0, keepdims=True)
        for c in range(ke // tq):
            rows = slice(c * tq, (c + 1) * tq)
            e = jnp.exp2(lg_ref[h, rows, :] - m).astype(BF16)
            e_ref[h, rows, :] = jnp.minimum(e, cap_ref[rows, :])
        out_t = _dot(vt_ref[0, g * vrows:(g + 1) * vrows, :], e_ref[h])
        o_ref[0, :, h * dh:(h + 1) * dh] = (out_t[:dh] / out_t[dh:dh + 1]).T.astype(BF16)


def _sparse_attention_mixer(x2d, b, s, attn_gain, w_in, q_gain, k_gain, kidx_gain, tabs):
    t, d = x2d.shape
    heads, kv_heads, idx_heads, di = ATT_HEADS, ATT_KV_HEADS, IDX_HEADS, IDX_DH
    dh = d // heads
    c64, s64, c32, s32 = tabs
    tm = min(ROW_TILE, s)
    per_b = s // tm
    nq, nkv, nqi = heads * dh, kv_heads * dh, idx_heads * di
    nvt = kv_heads * (dh + 2 * SUBLANES)
    n_main = nq + 2 * nkv + nqi + di
    zeros = lambda n: jnp.zeros((d, n), w_in.dtype)
    wt = jnp.concatenate([w_in[:, :n_main], zeros(LANES - di), w_in[:, n_main:n_main + idx_heads],
                          zeros(2 * SUBLANES - idx_heads)], axis=1).T.astype(BF16)
    qg = (q_gain * (dh ** -0.5 * math.log2(math.e))).reshape(dh, 1)
    row = lambda width: pl.BlockSpec((tm, width), lambda i: (i, 0))
    col = lambda n: pl.BlockSpec((n, tm), lambda i: (0, i))
    seq_t = lambda n: pl.BlockSpec((1, n, tm), lambda i: (i // per_b, 0, i % per_b))
    shp = lambda width, dt: jax.ShapeDtypeStruct((t, width), dt)
    shp_t = lambda n, dt: jax.ShapeDtypeStruct((b, n, s), dt)
    qt, k, vt, qit, ki, wit, out = pl.pallas_call(
        functools.partial(_dsa_inproj_kernel, heads=heads, kv_heads=kv_heads, dh=dh,
                          idx_heads=idx_heads, di=di, wscale=idx_heads ** -0.5 * di ** -0.5),
        grid=(t // tm,),
        in_specs=[row(d), _resident((1, d)), _resident(wt.shape), _resident((dh, 1)),
                  _resident((dh, 1)), _resident((di, 1)), col(dh // 2), col(dh // 2),
                  col(di // 2), col(di // 2)],
        out_specs=[seq_t(nq), row(nkv), seq_t(nvt), seq_t(nqi), row(di), seq_t(idx_heads),
                   row(nq)],
        out_shape=[shp_t(nq, BF16), shp(nkv, BF16), shp_t(nvt, BF16), shp_t(nqi, BF16),
                   shp(di, BF16), shp_t(idx_heads, F32), shp(nq, BF16)],
        compiler_params=_params("parallel"),
        name="dsa_inproj",
    )(x2d, attn_gain.reshape(1, d), wt, qg, k_gain.reshape(dh, 1), kidx_gain.reshape(di, 1),
      c64, s64, c32, s32)

    topk = min(TOPK_MAX, s // 4)
    tq = min(Q_TILE, s)
    k, ki = k.reshape(b, s, nkv), ki.reshape(b, s, di)
    out = out.reshape(b, s, nq)
    for tile in range(s // tq):
        ke = (tile + 1) * tq
        operands = [qt, k, vt, qit, ki, wit]
        in_specs = [pl.BlockSpec((1, nq, tq), lambda bi, tile=tile: (bi, 0, tile)),
                    pl.BlockSpec((1, ke, nkv), lambda bi: (bi, 0, 0)),
                    pl.BlockSpec((1, nvt, ke), lambda bi: (bi, 0, 0)),
                    pl.BlockSpec((1, nqi, tq), lambda bi, tile=tile: (bi, 0, tile)),
                    pl.BlockSpec((1, ke, di), lambda bi: (bi, 0, 0)),
                    pl.BlockSpec((1, idx_heads, tq), lambda bi, tile=tile: (bi, 0, tile))]
        operands.append(out)
        in_specs.append(pl.BlockSpec(memory_space=pl.ANY))
        out = pl.pallas_call(
            functools.partial(_dsa_attn_kernel, tile=tile, topk=topk, heads=heads,
                              kv_heads=kv_heads, dh=dh, idx_heads=idx_heads,
                              index_steps=(ke - 1).bit_length() + 1),
            grid=(b,),
            in_specs=in_specs,
            out_specs=pl.BlockSpec((1, tq, nq), lambda bi, tile=tile: (bi, tile, 0)),
            out_shape=jax.ShapeDtypeStruct((b, s, nq), BF16),
            scratch_shapes=[pltpu.VMEM((ke, tq), F32), pltpu.VMEM((heads, ke, tq), F32),
                            pltpu.VMEM((heads, ke, tq), BF16), pltpu.VMEM((ke, tq), BF16)],
            input_output_aliases={len(operands) - 1: 0},
            compiler_params=_params("parallel"),
            name=f"dsa_attn_{tile}",
        )(*operands)
    return out.reshape(t, nq)


def kernel(x, positions, attn_norm, ret_w_in, ret_out_norm, ret_w_out, dsa_w_in, dsa_q_norm,
           dsa_k_norm, dsa_kidx_norm, dsa_w_out, mlp_norm, mlp_w_up, mlp_w_down):
    b, s, d = x.shape
    depth = attn_norm.shape[0]
    c128, s128, c64, s64, c32, s32 = _rope_tables(positions)
    x2d = x.reshape(b * s, d)
    for i in range(depth):
        j = i // 2
        if i % 2 == 0:
            a = _retention_mixer(x2d, b, s, attn_norm[i], ret_w_in[j], c128, s128)
            w_out = ret_out_norm[j].reshape(-1, 1) * ret_w_out[j]
        else:
            a = _sparse_attention_mixer(x2d, b, s, attn_norm[i], dsa_w_in[j], dsa_q_norm[j],
                                        dsa_k_norm[j], dsa_kidx_norm[j], (c64, s64, c32, s32))
            w_out = dsa_w_out[j]
        x2d = _mixer_out_and_mlp(a, x2d, w_out, mlp_norm[i], mlp_w_up[i], mlp_w_down[i])
    return x2d.reshape(b, s, d)
```

```python
import functools
import math

import jax
import jax.numpy as jnp
from jax import lax
from jax.experimental import pallas as pl
from jax.experimental.pallas import tpu as pltpu

F32 = jnp.float32
BF16 = jnp.bfloat16

EPS = 1e-6
ROPE_THETA = 10000.0
RET_HEADS = 4
RET_CHUNK = 256
ATT_HEADS = 8
ATT_KV_HEADS = 2
IDX_HEADS = 8
IDX_DH = 64
TOPK_MAX = 256
LANES = 128
SUBLANES = 8

ROW_TILE = 512
RET_ROW_BLOCK = 1024
Q_TILE = 256
KEY_BLOCK = 128
SWEEP_UNROLL = 8
VMEM_LIMIT = 56 * 1024 * 1024
BISECT_STEPS = 13


def _params(*sem):
    return pltpu.CompilerParams(dimension_semantics=sem, vmem_limit_bytes=VMEM_LIMIT)


def _resident(shape):
    nd = len(shape)
    return pl.BlockSpec(shape, lambda *_: (0,) * nd, pipeline_mode=pl.Buffered(1))


def _rms(x, gain):
    return x * lax.rsqrt(jnp.mean(x * x, axis=-1, keepdims=True) + EPS) * gain


def _dot(a, b):
    return jnp.dot(a, b, preferred_element_type=F32)


def _dot_nt(a, b):
    return lax.dot_general(a, b, (((1,), (1,)), ((), ())), preferred_element_type=F32)


def _dot_tn(a, b):
    return lax.dot_general(a, b, (((0,), (0,)), ((), ())), preferred_element_type=F32)


def _rope_table_kernel(pos_ref, inv_ref, c128_ref, s128_ref, c64t_ref, s64t_ref, c32t_ref,
                       s32t_ref, ct_ref, st_ref):
    ang_t = inv_ref[...] * pos_ref[...].astype(F32)
    cos_t = jnp.cos(ang_t)
    sin_t = jnp.sin(ang_t)
    c128_ref[...] = cos_t.T
    s128_ref[...] = sin_t.T
    for c in range(ct_ref.shape[0]):
        lanes = slice(c * LANES, (c + 1) * LANES)
        ct_ref[c] = cos_t[:, lanes]
        st_ref[c] = sin_t[:, lanes]
        for step, c_ref, s_ref in ((2, c64t_ref, s64t_ref), (4, c32t_ref, s32t_ref)):
            rows = pl.ds(0, LANES // step, stride=step)
            c_ref[:, lanes] = ct_ref[c, rows, :]
            s_ref[:, lanes] = st_ref[c, rows, :]


def _rope_tables(positions):
    t = positions.size
    tm = min(ROW_TILE, t)
    inv = ROPE_THETA ** (-jnp.arange(LANES, dtype=F32) / LANES)
    row = pl.BlockSpec((tm, LANES), lambda i: (i, 0))
    col = lambda n: pl.BlockSpec((n, tm), lambda i: (0, i))
    tab = lambda n: jax.ShapeDtypeStruct((n, t), F32)
    return pl.pallas_call(
        _rope_table_kernel,
        grid=(t // tm,),
        in_specs=[pl.BlockSpec((1, tm), lambda i: (0, i)), _resident((LANES, 1))],
        out_specs=[row, row, col(64), col(64), col(32), col(32)],
        out_shape=[jax.ShapeDtypeStruct((t, LANES), F32)] * 2 + [tab(64), tab(64), tab(32),
                                                                  tab(32)],
        scratch_shapes=[pltpu.VMEM((tm // LANES, LANES, LANES), F32)] * 2,
        compiler_params=_params("parallel"),
        name="rope_tables",
    )(positions.reshape(1, t), inv.reshape(LANES, 1))


def _ret_inproj_kernel(x_ref, gain_ref, w_ref, cos_ref, sin_ref, q_ref, k_ref, v_ref, g_ref,
                       *, heads, dk, dv, kscale):
    hn = _rms(x_ref[...], gain_ref[...]).astype(BF16)
    cos = cos_ref[...]
    sin = sin_ref[...]
    half = dk // 2
    width = heads * dv
    for off, out_ref, swish in ((2 * heads * dk + width, g_ref, True),
                                (2 * heads * dk, v_ref, False)):
        for c in range(width // dv):
            z = _dot(hn, w_ref[:, off + c * dv:off + (c + 1) * dv].astype(BF16))
            if swish:
                z = z * (0.5 * jnp.tanh(0.5 * z) + 0.5)
            out_ref[:, c * dv:(c + 1) * dv] = z.astype(BF16)
    for h in range(heads):
        for off, out_ref, scale in ((0, q_ref, None), (heads * dk, k_ref, kscale)):
            z = _dot(hn, w_ref[:, off + h * dk:off + (h + 1) * dk].astype(BF16))
            x1, x2 = z[:, :half], z[:, half:]
            o1 = x1 * cos - x2 * sin
            o2 = x2 * cos + x1 * sin
            if scale is not None:
                o1, o2 = o1 * scale, o2 * scale
            out_ref[:, h * dk:h * dk + half] = o1.astype(BF16)
            out_ref[:, h * dk + half:(h + 1) * dk] = o2.astype(BF16)


def _ret_core_kernel(q_ref, k_ref, v_ref, g_ref, dm_ref, qd_ref, kd_ref, cd_ref,
                     o_ref, state_ref, *, heads, chunk):
    @pl.when(pl.program_id(1) == 0)
    def _():
        state_ref[...] = jnp.zeros_like(state_ref)

    dk = q_ref.shape[2] // heads
    dv = v_ref.shape[2] // heads
    hs = range(heads)
    states = [state_ref[h] for h in hs]
    for c in range(q_ref.shape[1] // chunk):
        rows = slice(c * chunk, (c + 1) * chunk)
        q = [q_ref[0, rows, h * dk:(h + 1) * dk] for h in hs]
        k = [k_ref[0, rows, h * dk:(h + 1) * dk] for h in hs]
        v = [v_ref[0, rows, h * dv:(h + 1) * dv] for h in hs]
        scores = [_dot_nt(q[h], k[h]) for h in hs]
        kv = [_dot_tn((k[h].astype(F32) * kd_ref[h]).astype(BF16), v[h]) for h in hs]
        cross = [_dot(q[h], states[h].astype(BF16)) for h in hs]
        intra = [_dot((scores[h] * dm_ref[h]).astype(BF16), v[h]) for h in hs]
        for h in hs:
            o = intra[h] + cross[h] * qd_ref[h]
            y = o * lax.rsqrt(jnp.mean(o * o, axis=-1, keepdims=True) + EPS)
            gate = g_ref[0, rows, h * dv:(h + 1) * dv].astype(F32)
            o_ref[0, rows, h * dv:(h + 1) * dv] = (y * gate).astype(BF16)
            states[h] = states[h] * cd_ref[h] + kv[h]
    for h in hs:
        state_ref[h] = states[h]


def _retention_mixer(x2d, b, s, attn_gain, w_in, cos, sin):
    t, d = x2d.shape
    heads = RET_HEADS
    dk = d // heads
    dv = 2 * dk
    chunk = RET_CHUNK
    tm = min(ROW_TILE, t)
    w = w_in
    row = lambda width: pl.BlockSpec((tm, width), lambda i: (i, 0))
    q, k, v, g = pl.pallas_call(
        functools.partial(_ret_inproj_kernel, heads=heads, dk=dk, dv=dv, kscale=dk ** -0.5),
        grid=(t // tm,),
        in_specs=[row(d), _resident((1, d)), _resident(w.shape), row(LANES), row(LANES)],
        out_specs=[row(heads * dk), row(heads * dk), row(heads * dv), row(heads * dv)],
        out_shape=[jax.ShapeDtypeStruct((t, heads * dk), BF16),
                   jax.ShapeDtypeStruct((t, heads * dk), BF16),
                   jax.ShapeDtypeStruct((t, heads * dv), BF16),
                   jax.ShapeDtypeStruct((t, heads * dv), BF16)],
        compiler_params=_params("parallel"),
        name="ret_inproj",
    )(x2d, attn_gain.reshape(1, d), w, cos, sin)

    log_gamma = jnp.log1p(-(2.0 ** (-5.0 - jnp.arange(heads, dtype=F32))))
    i = jnp.arange(chunk, dtype=F32)
    diff = i[:, None] - i[None, :]
    dm = jnp.where(diff >= 0, jnp.exp(log_gamma[:, None, None] * jnp.maximum(diff, 0.0)), 0.0)
    qd = jnp.exp(log_gamma[:, None] * (i + 1.0))
    kd = jnp.exp(log_gamma[:, None] * (chunk - 1.0 - i))
    cd = jnp.exp(log_gamma * chunk)
    qd = jnp.broadcast_to(qd[:, :, None], (heads, chunk, dv))
    kd = jnp.broadcast_to(kd[:, :, None], (heads, chunk, dk))
    cd = jnp.broadcast_to(cd[:, None, None], (heads, 1, dv))

    rb = min(RET_ROW_BLOCK, s)
    seq = lambda width: pl.BlockSpec((1, rb, width), lambda bi, r: (bi, r, 0))
    o = pl.pallas_call(
        functools.partial(_ret_core_kernel, heads=heads, chunk=chunk),
        grid=(b, s // rb),
        in_specs=[seq(heads * dk), seq(heads * dk), seq(heads * dv), seq(heads * dv),
                  _resident(dm.shape), _resident(qd.shape), _resident(kd.shape),
                  _resident(cd.shape)],
        out_specs=seq(heads * dv),
        out_shape=jax.ShapeDtypeStruct((b, s, heads * dv), BF16),
        scratch_shapes=[pltpu.VMEM((heads, dk, dv), F32)],
        compiler_params=_params("parallel", "arbitrary"),
        name="ret_core",
    )(q.reshape(b, s, -1), k.reshape(b, s, -1), v.reshape(b, s, -1), g.reshape(b, s, -1),
      dm, qd, kd, cd)
    return o.reshape(t, heads * dv)


def _tail_kernel(a_ref, x_ref, wo_ref, gain_ref, wup_ref, wdn_ref, o_ref, *, ff_chunk):
    x1 = x_ref[...] + _dot(a_ref[...], wo_ref[...])
    hn = _rms(x1, gain_ref[...]).astype(BF16)
    acc = x1
    for c in range(wup_ref.shape[1] // ff_chunk):
        cols = slice(c * ff_chunk, (c + 1) * ff_chunk)
        u = jnp.maximum(_dot(hn, wup_ref[:, cols].astype(BF16)), 0.0)
        acc = acc + _dot((u * u).astype(BF16), wdn_ref[cols, :])
    o_ref[...] = acc


def _mixer_out_and_mlp(a2d, x2d, w_out, mlp_gain, w_up, w_down):
    t, d = x2d.shape
    tm = min(ROW_TILE, t)
    wo, wu, wd = w_out.astype(BF16), w_up, w_down.astype(BF16)
    row = lambda width: pl.BlockSpec((tm, width), lambda i: (i, 0))
    return pl.pallas_call(
        functools.partial(_tail_kernel, ff_chunk=1024),
        grid=(t // tm,),
        in_specs=[row(a2d.shape[1]), row(d), _resident(wo.shape), _resident((1, d)),
                  _resident(wu.shape), _resident(wd.shape)],
        out_specs=row(d),
        out_shape=jax.ShapeDtypeStruct((t, d), F32),
        compiler_params=_params("parallel"),
        name="outproj_mlp",
    )(a2d, x2d, wo, mlp_gain.reshape(1, d), wu, wd)


def _norm_rope_rows(z, gain, cos, sin):
    if gain is not None:
        z = z * lax.rsqrt(jnp.mean(z * z, axis=0, keepdims=True) + EPS) * gain
    half = z.shape[0] // 2
    x1, x2 = z[:half], z[half:]
    return x1 * cos - x2 * sin, x2 * cos + x1 * sin


def _dsa_inproj_kernel(x_ref, gain_ref, wt_ref, qg_ref, kg_ref, kig_ref, c64_ref, s64_ref,
                       c32_ref, s32_ref, qt_ref, k_ref, vt_ref, qit_ref, ki_ref, wit_ref, buf_ref,
                       *, heads, kv_heads, dh, idx_heads, di, wscale):
    hn = _rms(x_ref[...], gain_ref[...]).astype(BF16)
    tm = hn.shape[0]
    proj = _dot_nt(wt_ref[...], hn)
    c64, s64 = c64_ref[...], s64_ref[...]
    c32, s32 = c32_ref[...], s32_ref[...]
    qg = jnp.broadcast_to(qg_ref[...], (dh, tm))
    kg = jnp.broadcast_to(kg_ref[...], (dh, tm))
    for h in range(heads):
        o1, o2 = _norm_rope_rows(proj[h * dh:(h + 1) * dh], qg, c64, s64)
        qt_ref[0, h * dh:h * dh + dh // 2, :] = o1.astype(BF16)
        qt_ref[0, h * dh + dh // 2:(h + 1) * dh, :] = o2.astype(BF16)
    off = heads * dh
    for h in range(kv_heads):
        o1, o2 = _norm_rope_rows(proj[off + h * dh:off + (h + 1) * dh], kg, c64, s64)
        k_ref[:, h * dh:(h + 1) * dh] = jnp.concatenate([o1, o2], axis=0).T.astype(BF16)
    off += kv_heads * dh
    vrows = vt_ref.shape[1] // kv_heads
    for g in range(kv_heads):
        vt_ref[0, g * vrows:g * vrows + dh, :] = proj[off + g * dh:off + (g + 1) * dh].astype(BF16)
        vt_ref[0, g * vrows + dh:(g + 1) * vrows, :] = jnp.ones((vrows - dh, tm), BF16)
    off += kv_heads * dh
    for h in range(idx_heads):
        o1, o2 = _norm_rope_rows(proj[off + h * di:off + (h + 1) * di], None, c32, s32)
        qit_ref[0, h * di:h * di + di // 2, :] = o1.astype(BF16)
        qit_ref[0, h * di + di // 2:(h + 1) * di, :] = o2.astype(BF16)
    off += idx_heads * di
    kig = jnp.broadcast_to(kig_ref[...], (di, tm))
    o1, o2 = _norm_rope_rows(proj[off:off + di], kig, c32, s32)
    ki_t = jnp.concatenate([o1, o2, proj[off + di:off + LANES]], axis=0)
    ki_ref[...] = ki_t.T[:, :di].astype(BF16)
    off += LANES
    wit_ref[0] = proj[off:off + idx_heads] * wscale
    buf_ref[...] = jnp.zeros_like(buf_ref)


def _dsa_attn_kernel(*refs, tile, topk, heads, kv_heads, dh, idx_heads, index_steps):
    qt_ref, k_ref, vt_ref, qit_ref, ki_ref, wit_ref = refs[:6]
    o_ref, sc_ref, lg_ref, e_ref, cap_ref = refs[-5:]
    tq = qt_ref.shape[2]
    ke = k_ref.shape[1]
    di = qit_ref.shape[1] // idx_heads
    kb = KEY_BLOCK
    slabs = kb // SUBLANES
    j = tile
    nblk = ke // kb
    neg = float(jnp.finfo(F32).min)
    big = float(jnp.finfo(F32).max)
    diag = (lax.broadcasted_iota(jnp.int32, (tq, tq), 0)
            <= lax.broadcasted_iota(jnp.int32, (tq, tq), 1))

    def rows_of(i):
        return pl.ds(pl.multiple_of(i * kb, kb), kb)

    def full(val):
        return jnp.full((SUBLANES, tq), val, F32)

    def spread(row):
        return jnp.broadcast_to(row, (SUBLANES, tq))

    group = heads // kv_heads

    def raw_logits(head, i):
        g = head // group
        lg_ref[head, rows_of(i), :] = _dot(k_ref[0, rows_of(i), g * dh:(g + 1) * dh],
                                           qt_ref[0, head * dh:(head + 1) * dh, :])

    def sweep(fn, init, head=None, unrolled=False):
        unroll = nblk if unrolled else max(u for u in range(1, SWEEP_UNROLL + 1) if nblk % u == 0)
        acc = lax.fori_loop(0, nblk, lambda i, a: fn(i, sc_ref[rows_of(i), :], a), init,
                            unroll=unroll)
        if head is not None:
            g = head // group
            lg = _dot(k_ref[0, :, g * dh:(g + 1) * dh], qt_ref[0, head * dh:(head + 1) * dh, :])
            lg_ref[head] = lg
            acc = acc + jnp.minimum(jnp.maximum(lg[ke - SUBLANES:, :], 0.0), 0.0)
        return acc

    if ke <= topk:
        sc_ref[...] = jnp.where(diag, 0.0, neg)
        for h in range(heads):
            lax.fori_loop(0, nblk, lambda i, c, h=h: (raw_logits(h, i), c)[1], 0, unroll=True)
    else:
        wit = wit_ref[0]
        ki = ki_ref[0]
        score = jnp.zeros((ke, tq), F32)
        for h in range(idx_heads):
            rel = jnp.maximum(_dot(ki, qit_ref[0, h * di:(h + 1) * di, :]), 0.0)
            score = score + rel * wit[h:h + 1, :]
        last = score[ke - tq:, :]
        row_max = jnp.max(jnp.where(diag, last, neg), axis=0, keepdims=True)
        row_min = jnp.min(jnp.where(diag, last, big), axis=0, keepdims=True)
        if ke > tq:
            row_max = jnp.maximum(row_max, jnp.max(score[:ke - tq, :], axis=0, keepdims=True))
            row_min = jnp.minimum(row_min, jnp.min(score[:ke - tq, :], axis=0, keepdims=True))
            sc_ref[:ke - tq, :] = score[:ke - tq, :]
        sc_ref[ke - tq:, :] = jnp.where(diag, last, neg)
        q_pos = j * tq + lax.broadcasted_iota(jnp.int32, (1, tq), 1)
        kvec = jnp.minimum(q_pos + 1, topk).astype(F32)

        def count_ge(th, head=None, unrolled=False):
            thb = spread(th)[None]
            acc = sweep(lambda i, blk, a: a + jnp.sum(
                jnp.where(blk.reshape(slabs, SUBLANES, tq) >= thb, 1.0, 0.0), axis=0),
                jnp.zeros((SUBLANES, tq), F32), head, unrolled)
            return jnp.sum(acc, axis=0, keepdims=True)

        def max_below(hi):
            hib = spread(hi)[None]

            def step(i, blk, a):
                blk = blk.reshape(slabs, SUBLANES, tq)
                return jnp.maximum(a, jnp.max(jnp.where(blk < hib, blk, neg), axis=0))

            return jnp.max(sweep(step, full(neg)), axis=0, keepdims=True)

        steps = max(BISECT_STEPS, heads)
        carried = {(h * steps) // heads: h for h in range(heads)}
        lo, hi, c_hi = row_min, row_max + jnp.abs(row_max) * 1e-3 + 1e-3, jnp.zeros((1, tq), F32)
        for step in range(steps):
            mid = lo * 0.5 + hi * 0.5
            c = count_ge(mid, carried.get(step), unrolled=True)
            ge = c >= kvec
            lo, hi, c_hi = jnp.where(ge, mid, lo), jnp.where(ge, hi, mid), jnp.where(ge, c_hi, c)

        def count_and_below(v):
            vb = spread(v)[None]

            def step(i, blk, a):
                blk = blk.reshape(slabs, SUBLANES, tq)
                ge = blk >= vb
                return (a[0] + jnp.sum(jnp.where(ge, 1.0, 0.0), axis=0),
                        jnp.maximum(a[1], jnp.max(jnp.where(ge, neg, blk), axis=0)))

            cnt, below = sweep(step, (jnp.zeros((SUBLANES, tq), F32), full(neg)))
            return jnp.sum(cnt, axis=0, keepdims=True), jnp.max(below, axis=0, keepdims=True)

        def pending(cv):
            return jnp.max(jnp.where(cv < kvec, 1.0, 0.0))

        def refine(state):
            c_hi, v, cv, below, _ = state
            short = cv < kvec
            c_hi = jnp.where(short, cv, c_hi)
            v = jnp.where(short, below, v)
            cv, below = count_and_below(v)
            return c_hi, v, cv, below, pending(cv)

        v = max_below(hi)
        cv, below = count_and_below(v)
        c_hi, v, cv, _, _ = lax.while_loop(lambda st: st[4] > 0.0, refine,
                                           (c_hi, v, cv, below, pending(cv)))
        need = kvec - c_hi
        surplus = jnp.max(jnp.where(cv > kvec, 1.0, 0.0)) > 0.0
        vb = spread(v)

        @pl.when(jnp.logical_not(surplus))
        def _():
            def to_bias(i, carry):
                blk = sc_ref[rows_of(i), :].reshape(slabs, SUBLANES, tq)
                sc_ref[rows_of(i), :] = jnp.where(blk >= vb[None], 0.0, neg).reshape(kb, tq)
                return carry
            lax.fori_loop(0, nblk, to_bias, 0)

        @pl.when(surplus)
        def _():
            def key_idx(i):
                return (lax.broadcasted_iota(jnp.int32, (slabs, SUBLANES, tq), 0) * SUBLANES
                        + lax.broadcasted_iota(jnp.int32, (slabs, SUBLANES, tq), 1) + i * kb)

            def idx_step(_, carry):
                lo_i, hi_i = carry
                mid_i = (lo_i + hi_i) >> 1
                midb = jnp.broadcast_to(mid_i, (SUBLANES, tq))
                acc = sweep(lambda i, blk, a: a + jnp.sum(jnp.where(
                    (blk.reshape(slabs, SUBLANES, tq) == vb[None]) & (key_idx(i) <= midb[None]),
                    1.0, 0.0), axis=0), jnp.zeros((SUBLANES, tq), F32))
                ok = jnp.sum(acc, axis=0, keepdims=True) >= need
                return jnp.where(ok, lo_i, mid_i), jnp.where(ok, mid_i, hi_i)

            _, istar = lax.fori_loop(0, index_steps, idx_step,
                                     (jnp.full((1, tq), -1, jnp.int32),
                                      jnp.full((1, tq), nblk * kb - 1, jnp.int32)))
            istarb = jnp.broadcast_to(istar, (SUBLANES, tq))

            def to_bias(i, carry):
                blk = sc_ref[rows_of(i), :].reshape(slabs, SUBLANES, tq)
                keep = (blk > vb[None]) | ((blk == vb[None]) & (key_idx(i) <= istarb[None]))
                sc_ref[rows_of(i), :] = jnp.where(keep, 0.0, neg).reshape(kb, tq)
                return carry
            lax.fori_loop(0, nblk, to_bias, 0)

    cap_val = float(jnp.finfo(BF16).max)

    packed = 2 * SUBLANES

    def max_pass(i, maxima):
        rows = rows_of(i)
        selected = sc_ref[rows, :] == 0.0
        cap_ref[rows, :] = jnp.where(selected, cap_val, 0.0).astype(BF16)
        bias = jnp.where(selected, 0.0, -cap_val).astype(BF16)
        new = []
        for h in range(heads):
            lg = lg_ref[h, rows, :].astype(BF16) + bias
            new.append(jnp.maximum(maxima[h], jnp.max(lg.reshape(kb // packed, packed, tq), axis=0)))
        return tuple(new)

    maxima = lax.fori_loop(0, nblk, max_pass, (jnp.full((packed, tq), -cap_val, BF16),) * heads,
                           unroll=min(nblk, 2))
    maxima = [m16.astype(F32) for m16 in maxima]

    vrows = vt_ref.shape[1] // kv_heads
    for h in range(heads):
        g = h // group
        m = jnp.max(maxima[h], axis=0, keepdims=True)
        for c in range(ke // tq):
            rows = slice(c * tq, (c + 1) * tq)
            e = jnp.exp2(lg_ref[h, rows, :] - m).astype(BF16)
            e_ref[h, rows, :] = jnp.minimum(e, cap_ref[rows, :])
        out_t = _dot(vt_ref[0, g * vrows:(g + 1) * vrows, :], e_ref[h])
        o_ref[0, :, h * dh:(h + 1) * dh] = (out_t[:dh] / out_t[dh:dh + 1]).T.astype(BF16)


def _sparse_attention_mixer(x2d, b, s, attn_gain, w_in, q_gain, k_gain, kidx_gain, tabs):
    t, d = x2d.shape
    heads, kv_heads, idx_heads, di = ATT_HEADS, ATT_KV_HEADS, IDX_HEADS, IDX_DH
    dh = d // heads
    c64, s64, c32, s32 = tabs
    tm = min(ROW_TILE, s)
    per_b = s // tm
    nq, nkv, nqi = heads * dh, kv_heads * dh, idx_heads * di
    nvt = kv_heads * (dh + 2 * SUBLANES)
    n_main = nq + 2 * nkv + nqi + di
    zeros = lambda n: jnp.zeros((d, n), w_in.dtype)
    wt = jnp.concatenate([w_in[:, :n_main], zeros(LANES - di), w_in[:, n_main:n_main + idx_heads],
                          zeros(2 * SUBLANES - idx_heads)], axis=1).T.astype(BF16)
    qg = (q_gain * (dh ** -0.5 * math.log2(math.e))).reshape(dh, 1)
    row = lambda width: pl.BlockSpec((tm, width), lambda i: (i, 0))
    col = lambda n: pl.BlockSpec((n, tm), lambda i: (0, i))
    seq_t = lambda n: pl.BlockSpec((1, n, tm), lambda i: (i // per_b, 0, i % per_b))
    shp = lambda width, dt: jax.ShapeDtypeStruct((t, width), dt)
    shp_t = lambda n, dt: jax.ShapeDtypeStruct((b, n, s), dt)
    qt, k, vt, qit, ki, wit, out = pl.pallas_call(
        functools.partial(_dsa_inproj_kernel, heads=heads, kv_heads=kv_heads, dh=dh,
                          idx_heads=idx_heads, di=di, wscale=idx_heads ** -0.5 * di ** -0.5),
        grid=(t // tm,),
        in_specs=[row(d), _resident((1, d)), _resident(wt.shape), _resident((dh, 1)),
                  _resident((dh, 1)), _resident((di, 1)), col(dh // 2), col(dh // 2),
                  col(di // 2), col(di // 2)],
        out_specs=[seq_t(nq), row(nkv), seq_t(nvt), seq_t(nqi), row(di), seq_t(idx_heads),
                   row(nq)],
        out_shape=[shp_t(nq, BF16), shp(nkv, BF16), shp_t(nvt, BF16), shp_t(nqi, BF16),
                   shp(di, BF16), shp_t(idx_heads, F32), shp(nq, BF16)],
        compiler_params=_params("parallel"),
        name="dsa_inproj",
    )(x2d, attn_gain.reshape(1, d), wt, qg, k_gain.reshape(dh, 1), kidx_gain.reshape(di, 1),
      c64, s64, c32, s32)

    topk = min(TOPK_MAX, s // 4)
    tq = min(Q_TILE, s)
    k, ki = k.reshape(b, s, nkv), ki.reshape(b, s, di)
    out = out.reshape(b, s, nq)
    for tile in range(s // tq):
        ke = (tile + 1) * tq
        operands = [qt, k, vt, qit, ki, wit]
        in_specs = [pl.BlockSpec((1, nq, tq), lambda bi, tile=tile: (bi, 0, tile)),
                    pl.BlockSpec((1, ke, nkv), lambda bi: (bi, 0, 0)),
                    pl.BlockSpec((1, nvt, ke), lambda bi: (bi, 0, 0)),
                    pl.BlockSpec((1, nqi, tq), lambda bi, tile=tile: (bi, 0, tile)),
                    pl.BlockSpec((1, ke, di), lambda bi: (bi, 0, 0)),
                    pl.BlockSpec((1, idx_heads, tq), lambda bi, tile=tile: (bi, 0, tile))]
        operands.append(out)
        in_specs.append(pl.BlockSpec(memory_space=pl.ANY))
        out = pl.pallas_call(
            functools.partial(_dsa_attn_kernel, tile=tile, topk=topk, heads=heads,
                              kv_heads=kv_heads, dh=dh, idx_heads=idx_heads,
                              index_steps=(ke - 1).bit_length() + 1),
            grid=(b,),
            in_specs=in_specs,
            out_specs=pl.BlockSpec((1, tq, nq), lambda bi, tile=tile: (bi, tile, 0)),
            out_shape=jax.ShapeDtypeStruct((b, s, nq), BF16),
            scratch_shapes=[pltpu.VMEM((ke, tq), F32), pltpu.VMEM((heads, ke, tq), F32),
                            pltpu.VMEM((heads, ke, tq), BF16), pltpu.VMEM((ke, tq), BF16)],
            input_output_aliases={len(operands) - 1: 0},
            compiler_params=_params("parallel"),
            name=f"dsa_attn_{tile}",
        )(*operands)
    return out.reshape(t, nq)


def kernel(x, positions, attn_norm, ret_w_in, ret_out_norm, ret_w_out, dsa_w_in, dsa_q_norm,
           dsa_k_norm, dsa_kidx_norm, dsa_w_out, mlp_norm, mlp_w_up, mlp_w_down):
    b, s, d = x.shape
    depth = attn_norm.shape[0]
    c128, s128, c64, s64, c32, s32 = _rope_tables(positions)
    x2d = x.reshape(b * s, d)
    for i in range(depth):
        j = i // 2
        if i % 2 == 0:
            a = _retention_mixer(x2d, b, s, attn_norm[i], ret_w_in[j], c128, s128)
            w_out = ret_out_norm[j].reshape(-1, 1) * ret_w_out[j]
        else:
            a = _sparse_attention_mixer(x2d, b, s, attn_norm[i], dsa_w_in[j], dsa_q_norm[j],
                                        dsa_k_norm[j], dsa_kidx_norm[j], (c64, s64, c32, s32))
            w_out = dsa_w_out[j]
        x2d = _mixer_out_and_mlp(a, x2d, w_out, mlp_norm[i], mlp_w_up[i], mlp_w_down[i])
    return x2d.reshape(b, s, d)
```

```python
import functools
import math

import jax
import jax.numpy as jnp
from jax import lax
from jax.experimental import pallas as pl
from jax.experimental.pallas import tpu as pltpu

F32 = jnp.float32
BF16 = jnp.bfloat16

EPS = 1e-6
ROPE_THETA = 10000.0
RET_HEADS = 4
RET_CHUNK = 256
ATT_HEADS = 8
ATT_KV_HEADS = 2
IDX_HEADS = 8
IDX_DH = 64
TOPK_MAX = 256
LANES = 128
SUBLANES = 8

ROW_TILE = 512
RET_ROW_BLOCK = 1024
Q_TILE = 256
KEY_BLOCK = 128
SWEEP_UNROLL = 8
VMEM_LIMIT = 56 * 1024 * 1024
BISECT_STEPS = 13


def _params(*sem):
    return pltpu.CompilerParams(dimension_semantics=sem, vmem_limit_bytes=VMEM_LIMIT)


def _resident(shape):
    nd = len(shape)
    return pl.BlockSpec(shape, lambda *_: (0,) * nd, pipeline_mode=pl.Buffered(1))


def _rms(x, gain):
    return x * lax.rsqrt(jnp.mean(x * x, axis=-1, keepdims=True) + EPS) * gain


def _dot(a, b):
    return jnp.dot(a, b, preferred_element_type=F32)


def _dot_nt(a, b):
    return lax.dot_general(a, b, (((1,), (1,)), ((), ())), preferred_element_type=F32)


def _dot_tn(a, b):
    return lax.dot_general(a, b, (((0,), (0,)), ((), ())), preferred_element_type=F32)


def _rope_table_kernel(pos_ref, inv_ref, c128_ref, s128_ref, c64t_ref, s64t_ref, c32t_ref,
                       s32t_ref, ct_ref, st_ref):
    ang_t = inv_ref[...] * pos_ref[...].astype(F32)
    cos_t = jnp.cos(ang_t)
    sin_t = jnp.sin(ang_t)
    c128_ref[...] = cos_t.T
    s128_ref[...] = sin_t.T
    for c in range(ct_ref.shape[0]):
        lanes = slice(c * LANES, (c + 1) * LANES)
        ct_ref[c] = cos_t[:, lanes]
        st_ref[c] = sin_t[:, lanes]
        for step, c_ref, s_ref in ((2, c64t_ref, s64t_ref), (4, c32t_ref, s32t_ref)):
            rows = pl.ds(0, LANES // step, stride=step)
            c_ref[:, lanes] = ct_ref[c, rows, :]
            s_ref[:, lanes] = st_ref[c, rows, :]


def _rope_tables(positions):
    t = positions.size
    tm = min(ROW_TILE, t)
    inv = ROPE_THETA ** (-jnp.arange(LANES, dtype=F32) / LANES)
    row = pl.BlockSpec((tm, LANES), lambda i: (i, 0))
    col = lambda n: pl.BlockSpec((n, tm), lambda i: (0, i))
    tab = lambda n: jax.ShapeDtypeStruct((n, t), F32)
    return pl.pallas_call(
        _rope_table_kernel,
        grid=(t // tm,),
        in_specs=[pl.BlockSpec((1, tm), lambda i: (0, i)), _resident((LANES, 1))],
        out_specs=[row, row, col(64), col(64), col(32), col(32)],
        out_shape=[jax.ShapeDtypeStruct((t, LANES), F32)] * 2 + [tab(64), tab(64), tab(32),
                                                                  tab(32)],
        scratch_shapes=[pltpu.VMEM((tm // LANES, LANES, LANES), F32)] * 2,
        compiler_params=_params("parallel"),
        name="rope_tables",
    )(positions.reshape(1, t), inv.reshape(LANES, 1))


def _ret_inproj_kernel(x_ref, gain_ref, w_ref, cos_ref, sin_ref, q_ref, k_ref, v_ref, g_ref,
                       *, heads, dk, dv, kscale):
    hn = _rms(x_ref[...], gain_ref[...]).astype(BF16)
    cos = cos_ref[...]
    sin = sin_ref[...]
    half = dk // 2
    width = heads * dv
    for off, out_ref, swish in ((2 * heads * dk + width, g_ref, True),
                                (2 * heads * dk, v_ref, False)):
        for c in range(width // dv):
            z = _dot(hn, w_ref[:, off + c * dv:off + (c + 1) * dv].astype(BF16))
            if swish:
                z = z * (0.5 * jnp.tanh(0.5 * z) + 0.5)
            out_ref[:, c * dv:(c + 1) * dv] = z.astype(BF16)
    for h in range(heads):
        for off, out_ref, scale in ((0, q_ref, None), (heads * dk, k_ref, kscale)):
            z = _dot(hn, w_ref[:, off + h * dk:off + (h + 1) * dk].astype(BF16))
            x1, x2 = z[:, :half], z[:, half:]
            o1 = x1 * cos - x2 * sin
            o2 = x2 * cos + x1 * sin
            if scale is not None:
                o1, o2 = o1 * scale, o2 * scale
            out_ref[:, h * dk:h * dk + half] = o1.astype(BF16)
            out_ref[:, h * dk + half:(h + 1) * dk] = o2.astype(BF16)


def _ret_core_kernel(q_ref, k_ref, v_ref, g_ref, dm_ref, qd_ref, kd_ref, cd_ref,
                     o_ref, state_ref, *, heads, chunk):
    @pl.when(pl.program_id(1) == 0)
    def _():
        state_ref[...] = jnp.zeros_like(state_ref)

    dk = q_ref.shape[2] // heads
    dv = v_ref.shape[2] // heads
    hs = range(heads)
    states = [state_ref[h] for h in hs]
    for c in range(q_ref.shape[1] // chunk):
        rows = slice(c * chunk, (c + 1) * chunk)
        q = [q_ref[0, rows, h * dk:(h + 1) * dk] for h in hs]
        k = [k_ref[0, rows, h * dk:(h + 1) * dk] for h in hs]
        v = [v_ref[0, rows, h * dv:(h + 1) * dv] for h in hs]
        scores = [_dot_nt(q[h], k[h]) for h in hs]
        kv = [_dot_tn((k[h].astype(F32) * kd_ref[h]).astype(BF16), v[h]) for h in hs]
        cross = [_dot(q[h], states[h].astype(BF16)) for h in hs]
        intra = [_dot((scores[h] * dm_ref[h]).astype(BF16), v[h]) for h in hs]
        for h in hs:
            o = intra[h] + cross[h] * qd_ref[h]
            y = o * lax.rsqrt(jnp.mean(o * o, axis=-1, keepdims=True) + EPS)
            gate = g_ref[0, rows, h * dv:(h + 1) * dv].astype(F32)
            o_ref[0, rows, h * dv:(h + 1) * dv] = (y * gate).astype(BF16)
            states[h] = states[h] * cd_ref[h] + kv[h]
    for h in hs:
        state_ref[h] = states[h]


def _retention_mixer(x2d, b, s, attn_gain, w_in, cos, sin):
    t, d = x2d.shape
    heads = RET_HEADS
    dk = d // heads
    dv = 2 * dk
    chunk = RET_CHUNK
    tm = min(ROW_TILE, t)
    w = w_in
    row = lambda width: pl.BlockSpec((tm, width), lambda i: (i, 0))
    q, k, v, g = pl.pallas_call(
        functools.partial(_ret_inproj_kernel, heads=heads, dk=dk, dv=dv, kscale=dk ** -0.5),
        grid=(t // tm,),
        in_specs=[row(d), _resident((1, d)), _resident(w.shape), row(LANES), row(LANES)],
        out_specs=[row(heads * dk), row(heads * dk), row(heads * dv), row(heads * dv)],
        out_shape=[jax.ShapeDtypeStruct((t, heads * dk), BF16),
                   jax.ShapeDtypeStruct((t, heads * dk), BF16),
                   jax.ShapeDtypeStruct((t, heads * dv), BF16),
                   jax.ShapeDtypeStruct((t, heads * dv), BF16)],
        compiler_params=_params("parallel"),
        name="ret_inproj",
    )(x2d, attn_gain.reshape(1, d), w, cos, sin)

    log_gamma = jnp.log1p(-(2.0 ** (-5.0 - jnp.arange(heads, dtype=F32))))
    i = jnp.arange(chunk, dtype=F32)
    diff = i[:, None] - i[None, :]
    dm = jnp.where(diff >= 0, jnp.exp(log_gamma[:, None, None] * jnp.maximum(diff, 0.0)), 0.0)
    qd = jnp.exp(log_gamma[:, None] * (i + 1.0))
    kd = jnp.exp(log_gamma[:, None] * (chunk - 1.0 - i))
    cd = jnp.exp(log_gamma * chunk)
    qd = jnp.broadcast_to(qd[:, :, None], (heads, chunk, dv))
    kd = jnp.broadcast_to(kd[:, :, None], (heads, chunk, dk))
    cd = jnp.broadcast_to(cd[:, None, None], (heads, 1, dv))

    rb = min(RET_ROW_BLOCK, s)
    seq = lambda width: pl.BlockSpec((1, rb, width), lambda bi, r: (bi, r, 0))
    o = pl.pallas_call(
        functools.partial(_ret_core_kernel, heads=heads, chunk=chunk),
        grid=(b, s // rb),
        in_specs=[seq(heads * dk), seq(heads * dk), seq(heads * dv), seq(heads * dv),
                  _resident(dm.shape), _resident(qd.shape), _resident(kd.shape),
                  _resident(cd.shape)],
        out_specs=seq(heads * dv),
        out_shape=jax.ShapeDtypeStruct((b, s, heads * dv), BF16),
        scratch_shapes=[pltpu.VMEM((heads, dk, dv), F32)],
        compiler_params=_params("parallel", "arbitrary"),
        name="ret_core",
    )(q.reshape(b, s, -1), k.reshape(b, s, -1), v.reshape(b, s, -1), g.reshape(b, s, -1),
      dm, qd, kd, cd)
    return o.reshape(t, heads * dv)


def _tail_kernel(a_ref, x_ref, wo_ref, gain_ref, wup_ref, wdn_ref, o_ref, *, ff_chunk):
    x1 = x_ref[...] + _dot(a_ref[...], wo_ref[...])
    hn = _rms(x1, gain_ref[...]).astype(BF16)
    acc = x1
    for c in range(wup_ref.shape[2] // ff_chunk):
        cols = slice(c * ff_chunk, (c + 1) * ff_chunk)
        u = jnp.maximum(_dot(hn, wup_ref[0, :, cols].astype(BF16)), 0.0)
        acc = acc + _dot((u * u).astype(BF16), wdn_ref[0, cols, :].astype(BF16))
    o_ref[...] = acc


def _mixer_out_and_mlp(a2d, x2d, w_out, mlp_gain, w_up_all, w_down_all, layer):
    t, d = x2d.shape
    tm = min(ROW_TILE, t)
    wo = w_out.astype(BF16)
    row = lambda width: pl.BlockSpec((tm, width), lambda i: (i, 0))
    of_layer = lambda w: pl.BlockSpec((1,) + w.shape[1:], lambda i: (layer, 0, 0),
                                      pipeline_mode=pl.Buffered(1))
    return pl.pallas_call(
        functools.partial(_tail_kernel, ff_chunk=1024),
        grid=(t // tm,),
        in_specs=[row(a2d.shape[1]), row(d), _resident(wo.shape), _resident((1, d)),
                  of_layer(w_up_all), of_layer(w_down_all)],
        out_specs=row(d),
        out_shape=jax.ShapeDtypeStruct((t, d), F32),
        compiler_params=_params("parallel"),
        name="outproj_mlp",
    )(a2d, x2d, wo, mlp_gain.reshape(1, d), w_up_all, w_down_all)


def _norm_rope_rows(z, gain, cos, sin):
    if gain is not None:
        z = z * lax.rsqrt(jnp.mean(z * z, axis=0, keepdims=True) + EPS) * gain
    half = z.shape[0] // 2
    x1, x2 = z[:half], z[half:]
    return x1 * cos - x2 * sin, x2 * cos + x1 * sin


def _dsa_inproj_kernel(x_ref, gain_ref, wt_ref, qg_ref, kg_ref, kig_ref, c64_ref, s64_ref,
                       c32_ref, s32_ref, qt_ref, k_ref, vt_ref, qit_ref, ki_ref, wit_ref, buf_ref,
                       *, heads, kv_heads, dh, idx_heads, di, wscale):
    hn = _rms(x_ref[...], gain_ref[...]).astype(BF16)
    tm = hn.shape[0]
    proj = _dot_nt(wt_ref[...], hn)
    c64, s64 = c64_ref[...], s64_ref[...]
    c32, s32 = c32_ref[...], s32_ref[...]
    qg = jnp.broadcast_to(qg_ref[...], (dh, tm))
    kg = jnp.broadcast_to(kg_ref[...], (dh, tm))
    for h in range(heads):
        o1, o2 = _norm_rope_rows(proj[h * dh:(h + 1) * dh], qg, c64, s64)
        qt_ref[0, h * dh:h * dh + dh // 2, :] = o1.astype(BF16)
        qt_ref[0, h * dh + dh // 2:(h + 1) * dh, :] = o2.astype(BF16)
    off = heads * dh
    for h in range(kv_heads):
        o1, o2 = _norm_rope_rows(proj[off + h * dh:off + (h + 1) * dh], kg, c64, s64)
        k_ref[:, h * dh:(h + 1) * dh] = jnp.concatenate([o1, o2], axis=0).T.astype(BF16)
    off += kv_heads * dh
    vrows = vt_ref.shape[1] // kv_heads
    for g in range(kv_heads):
        vt_ref[0, g * vrows:g * vrows + dh, :] = proj[off + g * dh:off + (g + 1) * dh].astype(BF16)
        vt_ref[0, g * vrows + dh:(g + 1) * vrows, :] = jnp.ones((vrows - dh, tm), BF16)
    off += kv_heads * dh
    for h in range(idx_heads):
        o1, o2 = _norm_rope_rows(proj[off + h * di:off + (h + 1) * di], None, c32, s32)
        qit_ref[0, h * di:h * di + di // 2, :] = o1.astype(BF16)
        qit_ref[0, h * di + di // 2:(h + 1) * di, :] = o2.astype(BF16)
    off += idx_heads * di
    kig = jnp.broadcast_to(kig_ref[...], (di, tm))
    o1, o2 = _norm_rope_rows(proj[off:off + di], kig, c32, s32)
    ki_t = jnp.concatenate([o1, o2, proj[off + di:off + LANES]], axis=0)
    ki_ref[...] = ki_t.T[:, :di].astype(BF16)
    off += LANES
    wit_ref[0] = proj[off:off + idx_heads] * wscale
    buf_ref[...] = jnp.zeros_like(buf_ref)


def _dsa_attn_kernel(*refs, tile, topk, heads, kv_heads, dh, idx_heads, index_steps):
    qt_ref, k_ref, vt_ref, qit_ref, ki_ref, wit_ref = refs[:6]
    o_ref, sc_ref, lg_ref, e_ref, cap_ref = refs[-5:]
    tq = qt_ref.shape[2]
    ke = k_ref.shape[1]
    di = qit_ref.shape[1] // idx_heads
    kb = KEY_BLOCK
    slabs = kb // SUBLANES
    j = tile
    nblk = ke // kb
    neg = float(jnp.finfo(F32).min)
    big = float(jnp.finfo(F32).max)
    diag = (lax.broadcasted_iota(jnp.int32, (tq, tq), 0)
            <= lax.broadcasted_iota(jnp.int32, (tq, tq), 1))

    def rows_of(i):
        return pl.ds(pl.multiple_of(i * kb, kb), kb)

    def full(val):
        return jnp.full((SUBLANES, tq), val, F32)

    def spread(row):
        return jnp.broadcast_to(row, (SUBLANES, tq))

    group = heads // kv_heads

    def raw_logits(head, i):
        g = head // group
        lg_ref[head, rows_of(i), :] = _dot(k_ref[0, rows_of(i), g * dh:(g + 1) * dh],
                                           qt_ref[0, head * dh:(head + 1) * dh, :])

    def sweep(fn, init, head=None, unrolled=False):
        unroll = nblk if unrolled else max(u for u in range(1, SWEEP_UNROLL + 1) if nblk % u == 0)
        acc = lax.fori_loop(0, nblk, lambda i, a: fn(i, sc_ref[rows_of(i), :], a), init,
                            unroll=unroll)
        if head is not None:
            g = head // group
            lg = _dot(k_ref[0, :, g * dh:(g + 1) * dh], qt_ref[0, head * dh:(head + 1) * dh, :])
            lg_ref[head] = lg
            acc = acc + jnp.minimum(jnp.maximum(lg[ke - SUBLANES:, :], 0.0), 0.0)
        return acc

    if ke <= topk:
        sc_ref[...] = jnp.where(diag, 0.0, neg)
        for h in range(heads):
            lax.fori_loop(0, nblk, lambda i, c, h=h: (raw_logits(h, i), c)[1], 0, unroll=True)
    else:
        wit = wit_ref[0]
        ki = ki_ref[0]
        score = jnp.zeros((ke, tq), F32)
        for h in range(idx_heads):
            rel = jnp.maximum(_dot(ki, qit_ref[0, h * di:(h + 1) * di, :]), 0.0)
            score = score + rel * wit[h:h + 1, :]
        last = score[ke - tq:, :]
        row_max = jnp.max(jnp.where(diag, last, neg), axis=0, keepdims=True)
        row_min = jnp.min(jnp.where(diag, last, big), axis=0, keepdims=True)
        if ke > tq:
            row_max = jnp.maximum(row_max, jnp.max(score[:ke - tq, :], axis=0, keepdims=True))
            row_min = jnp.minimum(row_min, jnp.min(score[:ke - tq, :], axis=0, keepdims=True))
            sc_ref[:ke - tq, :] = score[:ke - tq, :]
        sc_ref[ke - tq:, :] = jnp.where(diag, last, neg)
        q_pos = j * tq + lax.broadcasted_iota(jnp.int32, (1, tq), 1)
        kvec = jnp.minimum(q_pos + 1, topk).astype(F32)

        def count_ge(th, head=None, unrolled=False):
            thb = spread(th)[None]
            acc = sweep(lambda i, blk, a: a + jnp.sum(
                jnp.where(blk.reshape(slabs, SUBLANES, tq) >= thb, 1.0, 0.0), axis=0),
                jnp.zeros((SUBLANES, tq), F32), head, unrolled)
            return jnp.sum(acc, axis=0, keepdims=True)

        def max_below(hi):
            hib = spread(hi)[None]

            def step(i, blk, a):
                blk = blk.reshape(slabs, SUBLANES, tq)
                return jnp.maximum(a, jnp.max(jnp.where(blk < hib, blk, neg), axis=0))

            return jnp.max(sweep(step, full(neg)), axis=0, keepdims=True)

        steps = max(BISECT_STEPS, heads)
        carried = {(h * steps) // heads: h for h in range(heads)}
        lo, hi, c_hi = row_min, row_max + jnp.abs(row_max) * 1e-3 + 1e-3, jnp.zeros((1, tq), F32)
        for step in range(steps):
            mid = lo * 0.5 + hi * 0.5
            c = count_ge(mid, carried.get(step), unrolled=True)
            ge = c >= kvec
            lo, hi, c_hi = jnp.where(ge, mid, lo), jnp.where(ge, hi, mid), jnp.where(ge, c_hi, c)

        def count_and_below(v):
            vb = spread(v)[None]

            def step(i, blk, a):
                blk = blk.reshape(slabs, SUBLANES, tq)
                ge = blk >= vb
                return (a[0] + jnp.sum(jnp.where(ge, 1.0, 0.0), axis=0),
                        jnp.maximum(a[1], jnp.max(jnp.where(ge, neg, blk), axis=0)))

            cnt, below = sweep(step, (jnp.zeros((SUBLANES, tq), F32), full(neg)))
            return jnp.sum(cnt, axis=0, keepdims=True), jnp.max(below, axis=0, keepdims=True)

        def pending(cv):
            return jnp.max(jnp.where(cv < kvec, 1.0, 0.0))

        def refine(state):
            c_hi, v, cv, below, _ = state
            short = cv < kvec
            c_hi = jnp.where(short, cv, c_hi)
            v = jnp.where(short, below, v)
            cv, below = count_and_below(v)
            return c_hi, v, cv, below, pending(cv)

        v = max_below(hi)
        cv, below = count_and_below(v)
        c_hi, v, cv, _, _ = lax.while_loop(lambda st: st[4] > 0.0, refine,
                                           (c_hi, v, cv, below, pending(cv)))
        need = kvec - c_hi
        surplus = jnp.max(jnp.where(cv > kvec, 1.0, 0.0)) > 0.0
        vb = spread(v)

        @pl.when(jnp.logical_not(surplus))
        def _():
            def to_bias(i, carry):
                blk = sc_ref[rows_of(i), :].reshape(slabs, SUBLANES, tq)
                sc_ref[rows_of(i), :] = jnp.where(blk >= vb[None], 0.0, neg).reshape(kb, tq)
                return carry
            lax.fori_loop(0, nblk, to_bias, 0)

        @pl.when(surplus)
        def _():
            def key_idx(i):
                return (lax.broadcasted_iota(jnp.int32, (slabs, SUBLANES, tq), 0) * SUBLANES
                        + lax.broadcasted_iota(jnp.int32, (slabs, SUBLANES, tq), 1) + i * kb)

            def idx_step(_, carry):
                lo_i, hi_i = carry
                mid_i = (lo_i + hi_i) >> 1
                midb = jnp.broadcast_to(mid_i, (SUBLANES, tq))
                acc = sweep(lambda i, blk, a: a + jnp.sum(jnp.where(
                    (blk.reshape(slabs, SUBLANES, tq) == vb[None]) & (key_idx(i) <= midb[None]),
                    1.0, 0.0), axis=0), jnp.zeros((SUBLANES, tq), F32))
                ok = jnp.sum(acc, axis=0, keepdims=True) >= need
                return jnp.where(ok, lo_i, mid_i), jnp.where(ok, mid_i, hi_i)

            _, istar = lax.fori_loop(0, index_steps, idx_step,
                                     (jnp.full((1, tq), -1, jnp.int32),
                                      jnp.full((1, tq), nblk * kb - 1, jnp.int32)))
            istarb = jnp.broadcast_to(istar, (SUBLANES, tq))

            def to_bias(i, carry):
                blk = sc_ref[rows_of(i), :].reshape(slabs, SUBLANES, tq)
                keep = (blk > vb[None]) | ((blk == vb[None]) & (key_idx(i) <= istarb[None]))
                sc_ref[rows_of(i), :] = jnp.where(keep, 0.0, neg).reshape(kb, tq)
                return carry
            lax.fori_loop(0, nblk, to_bias, 0)

    cap_val = float(jnp.finfo(BF16).max)

    packed = 2 * SUBLANES

    def max_pass(i, maxima):
        rows = rows_of(i)
        selected = sc_ref[rows, :] == 0.0
        cap_ref[rows, :] = jnp.where(selected, cap_val, 0.0).astype(BF16)
        bias = jnp.where(selected, 0.0, -cap_val).astype(BF16)
        new = []
        for h in range(heads):
            lg = lg_ref[h, rows, :].astype(BF16) + bias
            new.append(jnp.maximum(maxima[h], jnp.max(lg.reshape(kb // packed, packed, tq), axis=0)))
        return tuple(new)

    maxima = lax.fori_loop(0, nblk, max_pass, (jnp.full((packed, tq), -cap_val, BF16),) * heads,
                           unroll=min(nblk, 2))
    maxima = [m16.astype(F32) for m16 in maxima]

    vrows = vt_ref.shape[1] // kv_heads
    for h in range(heads):
        g = h // group
        m = jnp.max(maxima[h], axis=0, keepdims=True)
        for c in range(ke // tq):
            rows = slice(c * tq, (c + 1) * tq)
            e = jnp.exp2(lg_ref[h, rows, :] - m).astype(BF16)
            e_ref[h, rows, :] = jnp.minimum(e, cap_ref[rows, :])
        out_t = _dot(vt_ref[0, g * vrows:(g + 1) * vrows, :], e_ref[h])
        o_ref[0, :, h * dh:(h + 1) * dh] = (out_t[:dh] / out_t[dh:dh + 1]).T.astype(BF16)


def _sparse_attention_mixer(x2d, b, s, attn_gain, w_in, q_gain, k_gain, kidx_gain, tabs):
    t, d = x2d.shape
    heads, kv_heads, idx_heads, di = ATT_HEADS, ATT_KV_HEADS, IDX_HEADS, IDX_DH
    dh = d // heads
    c64, s64, c32, s32 = tabs
    tm = min(ROW_TILE, s)
    per_b = s // tm
    nq, nkv, nqi = heads * dh, kv_heads * dh, idx_heads * di
    nvt = kv_heads * (dh + 2 * SUBLANES)
    n_main = nq + 2 * nkv + nqi + di
    zeros = lambda n: jnp.zeros((d, n), w_in.dtype)
    wt = jnp.concatenate([w_in[:, :n_main], zeros(LANES - di), w_in[:, n_main:n_main + idx_heads],
                          zeros(2 * SUBLANES - idx_heads)], axis=1).T.astype(BF16)
    qg = (q_gain * (dh ** -0.5 * math.log2(math.e))).reshape(dh, 1)
    row = lambda width: pl.BlockSpec((tm, width), lambda i: (i, 0))
    col = lambda n: pl.BlockSpec((n, tm), lambda i: (0, i))
    seq_t = lambda n: pl.BlockSpec((1, n, tm), lambda i: (i // per_b, 0, i % per_b))
    shp = lambda width, dt: jax.ShapeDtypeStruct((t, width), dt)
    shp_t = lambda n, dt: jax.ShapeDtypeStruct((b, n, s), dt)
    qt, k, vt, qit, ki, wit, out = pl.pallas_call(
        functools.partial(_dsa_inproj_kernel, heads=heads, kv_heads=kv_heads, dh=dh,
                          idx_heads=idx_heads, di=di, wscale=idx_heads ** -0.5 * di ** -0.5),
        grid=(t // tm,),
        in_specs=[row(d), _resident((1, d)), _resident(wt.shape), _resident((dh, 1)),
                  _resident((dh, 1)), _resident((di, 1)), col(dh // 2), col(dh // 2),
                  col(di // 2), col(di // 2)],
        out_specs=[seq_t(nq), row(nkv), seq_t(nvt), seq_t(nqi), row(di), seq_t(idx_heads),
                   row(nq)],
        out_shape=[shp_t(nq, BF16), shp(nkv, BF16), shp_t(nvt, BF16), shp_t(nqi, BF16),
                   shp(di, BF16), shp_t(idx_heads, F32), shp(nq, BF16)],
        compiler_params=_params("parallel"),
        name="dsa_inproj",
    )(x2d, attn_gain.reshape(1, d), wt, qg, k_gain.reshape(dh, 1), kidx_gain.reshape(di, 1),
      c64, s64, c32, s32)

    topk = min(TOPK_MAX, s // 4)
    tq = min(Q_TILE, s)
    k, ki = k.reshape(b, s, nkv), ki.reshape(b, s, di)
    out = out.reshape(b, s, nq)
    for tile in range(s // tq):
        ke = (tile + 1) * tq
        operands = [qt, k, vt, qit, ki, wit]
        in_specs = [pl.BlockSpec((1, nq, tq), lambda bi, tile=tile: (bi, 0, tile)),
                    pl.BlockSpec((1, ke, nkv), lambda bi: (bi, 0, 0)),
                    pl.BlockSpec((1, nvt, ke), lambda bi: (bi, 0, 0)),
                    pl.BlockSpec((1, nqi, tq), lambda bi, tile=tile: (bi, 0, tile)),
                    pl.BlockSpec((1, ke, di), lambda bi: (bi, 0, 0)),
                    pl.BlockSpec((1, idx_heads, tq), lambda bi, tile=tile: (bi, 0, tile))]
        operands.append(out)
        in_specs.append(pl.BlockSpec(memory_space=pl.ANY))
        out = pl.pallas_call(
            functools.partial(_dsa_attn_kernel, tile=tile, topk=topk, heads=heads,
                              kv_heads=kv_heads, dh=dh, idx_heads=idx_heads,
                              index_steps=(ke - 1).bit_length() + 1),
            grid=(b,),
            in_specs=in_specs,
            out_specs=pl.BlockSpec((1, tq, nq), lambda bi, tile=tile: (bi, tile, 0)),
            out_shape=jax.ShapeDtypeStruct((b, s, nq), BF16),
            scratch_shapes=[pltpu.VMEM((ke, tq), F32), pltpu.VMEM((heads, ke, tq), F32),
                            pltpu.VMEM((heads, ke, tq), BF16), pltpu.VMEM((ke, tq), BF16)],
            input_output_aliases={len(operands) - 1: 0},
            compiler_params=_params("parallel"),
            name=f"dsa_attn_{tile}",
        )(*operands)
    return out.reshape(t, nq)


def kernel(x, positions, attn_norm, ret_w_in, ret_out_norm, ret_w_out, dsa_w_in, dsa_q_norm,
           dsa_k_norm, dsa_kidx_norm, dsa_w_out, mlp_norm, mlp_w_up, mlp_w_down):
    b, s, d = x.shape
    depth = attn_norm.shape[0]
    c128, s128, c64, s64, c32, s32 = _rope_tables(positions)
    x2d = x.reshape(b * s, d)
    for i in range(depth):
        j = i // 2
        if i % 2 == 0:
            a = _retention_mixer(x2d, b, s, attn_norm[i], ret_w_in[j], c128, s128)
            w_out = ret_out_norm[j].reshape(-1, 1) * ret_w_out[j]
        else:
            a = _sparse_attention_mixer(x2d, b, s, attn_norm[i], dsa_w_in[j], dsa_q_norm[j],
                                        dsa_k_norm[j], dsa_kidx_norm[j], (c64, s64, c32, s32))
            w_out = dsa_w_out[j]
        x2d = _mixer_out_and_mlp(a, x2d, w_out, mlp_norm[i], mlp_w_up, mlp_w_down, i)
    return x2d.reshape(b, s, d)
```

```python
import functools
import math

import jax
import jax.numpy as jnp
from jax import lax
from jax.experimental import pallas as pl
from jax.experimental.pallas import tpu as pltpu

F32 = jnp.float32
BF16 = jnp.bfloat16

EPS = 1e-6
ROPE_THETA = 10000.0
RET_HEADS = 4
RET_CHUNK = 256
ATT_HEADS = 8
ATT_KV_HEADS = 2
IDX_HEADS = 8
IDX_DH = 64
TOPK_MAX = 256
LANES = 128
SUBLANES = 8

ROW_TILE = 512
RET_ROW_BLOCK = 1024
Q_TILE = 256
KEY_BLOCK = 128
SWEEP_UNROLL = 8
VMEM_LIMIT = 56 * 1024 * 1024
BISECT_STEPS = 13


def _params(*sem):
    return pltpu.CompilerParams(dimension_semantics=sem, vmem_limit_bytes=VMEM_LIMIT)


def _resident(shape):
    nd = len(shape)
    return pl.BlockSpec(shape, lambda *_: (0,) * nd, pipeline_mode=pl.Buffered(1))


def _rms(x, gain):
    return x * lax.rsqrt(jnp.mean(x * x, axis=-1, keepdims=True) + EPS) * gain


def _dot(a, b):
    return jnp.dot(a, b, preferred_element_type=F32)


def _dot_nt(a, b):
    return lax.dot_general(a, b, (((1,), (1,)), ((), ())), preferred_element_type=F32)


def _dot_tn(a, b):
    return lax.dot_general(a, b, (((0,), (0,)), ((), ())), preferred_element_type=F32)


def _rope_slab_in_kernel(c, pos_ref, inv_ref, c64t_ref, s64t_ref, c32t_ref, s32t_ref, ct_ref,
                         st_ref):
    lanes = slice(c * LANES, (c + 1) * LANES)
    ang_t = inv_ref[...] * pos_ref[:, lanes].astype(F32)
    cos_t = jnp.cos(ang_t)
    sin_t = jnp.sin(ang_t)
    ct_ref[c] = cos_t
    st_ref[c] = sin_t
    for step, c_ref, s_ref in ((2, c64t_ref, s64t_ref), (4, c32t_ref, s32t_ref)):
        rows = pl.ds(0, LANES // step, stride=step)
        c_ref[:, lanes] = ct_ref[c, rows, :]
        s_ref[:, lanes] = st_ref[c, rows, :]
    return cos_t.T, sin_t.T


def _ret_inproj_kernel(x_ref, gain_ref, w_ref, pos_ref, inv_ref, q_ref, k_ref, v_ref, g_ref,
                       c64t_ref, s64t_ref, c32t_ref, s32t_ref, ct_ref, st_ref,
                       *, heads, dk, dv, kscale):
    hn = _rms(x_ref[...], gain_ref[...]).astype(BF16)
    half = dk // 2
    width = heads * dv
    for off, out_ref, swish in ((2 * heads * dk + width, g_ref, True),
                                (2 * heads * dk, v_ref, False)):
        for c in range(width // dv):
            z = _dot(hn, w_ref[:, off + c * dv:off + (c + 1) * dv].astype(BF16))
            if swish:
                z = z * (0.5 * jnp.tanh(0.5 * z) + 0.5)
            out_ref[:, c * dv:(c + 1) * dv] = z.astype(BF16)
    trig = [_rope_slab_in_kernel(c, pos_ref, inv_ref, c64t_ref, s64t_ref, c32t_ref, s32t_ref,
                                 ct_ref, st_ref) for c in range(ct_ref.shape[0])]
    cos = jnp.concatenate([t[0] for t in trig], axis=0)
    sin = jnp.concatenate([t[1] for t in trig], axis=0)
    for h in range(heads):
        for off, out_ref, scale in ((0, q_ref, None), (heads * dk, k_ref, kscale)):
            z = _dot(hn, w_ref[:, off + h * dk:off + (h + 1) * dk].astype(BF16))
            x1, x2 = z[:, :half], z[:, half:]
            o1 = x1 * cos - x2 * sin
            o2 = x2 * cos + x1 * sin
            if scale is not None:
                o1, o2 = o1 * scale, o2 * scale
            out_ref[:, h * dk:h * dk + half] = o1.astype(BF16)
            out_ref[:, h * dk + half:(h + 1) * dk] = o2.astype(BF16)


def _ret_core_kernel(q_ref, k_ref, v_ref, g_ref, dm_ref, qd_ref, kd_ref, cd_ref,
                     o_ref, state_ref, *, heads, chunk):
    @pl.when(pl.program_id(1) == 0)
    def _():
        state_ref[...] = jnp.zeros_like(state_ref)

    dk = q_ref.shape[2] // heads
    dv = v_ref.shape[2] // heads
    hs = range(heads)
    states = [state_ref[h] for h in hs]
    for c in range(q_ref.shape[1] // chunk):
        rows = slice(c * chunk, (c + 1) * chunk)
        q = [q_ref[0, rows, h * dk:(h + 1) * dk] for h in hs]
        k = [k_ref[0, rows, h * dk:(h + 1) * dk] for h in hs]
        v = [v_ref[0, rows, h * dv:(h + 1) * dv] for h in hs]
        scores = [_dot_nt(q[h], k[h]) for h in hs]
        kv = [_dot_tn((k[h].astype(F32) * kd_ref[h]).astype(BF16), v[h]) for h in hs]
        cross = [_dot(q[h], states[h].astype(BF16)) for h in hs]
        intra = [_dot((scores[h] * dm_ref[h]).astype(BF16), v[h]) for h in hs]
        for h in hs:
            o = intra[h] + cross[h] * qd_ref[h]
            y = o * lax.rsqrt(jnp.mean(o * o, axis=-1, keepdims=True) + EPS)
            gate = g_ref[0, rows, h * dv:(h + 1) * dv].astype(F32)
            o_ref[0, rows, h * dv:(h + 1) * dv] = (y * gate).astype(BF16)
            states[h] = states[h] * cd_ref[h] + kv[h]
    for h in hs:
        state_ref[h] = states[h]


def _retention_mixer(x2d, b, s, attn_gain, w_in, positions):
    t, d = x2d.shape
    heads = RET_HEADS
    dk = d // heads
    dv = 2 * dk
    chunk = RET_CHUNK
    tm = min(ROW_TILE, t)
    w = w_in
    row = lambda width: pl.BlockSpec((tm, width), lambda i: (i, 0))
    col = lambda n: pl.BlockSpec((n, tm), lambda i: (0, i))
    tab = lambda n: jax.ShapeDtypeStruct((n, t), F32)
    inv = ROPE_THETA ** (-jnp.arange(LANES, dtype=F32) / LANES)
    q, k, v, g, *tables = pl.pallas_call(
        functools.partial(_ret_inproj_kernel, heads=heads, dk=dk, dv=dv, kscale=dk ** -0.5),
        grid=(t // tm,),
        in_specs=[row(d), _resident((1, d)), _resident(w.shape),
                  pl.BlockSpec((1, tm), lambda i: (0, i)), _resident((LANES, 1))],
        out_specs=[row(heads * dk), row(heads * dk), row(heads * dv), row(heads * dv),
                   col(64), col(64), col(32), col(32)],
        out_shape=[jax.ShapeDtypeStruct((t, heads * dk), BF16),
                   jax.ShapeDtypeStruct((t, heads * dk), BF16),
                   jax.ShapeDtypeStruct((t, heads * dv), BF16),
                   jax.ShapeDtypeStruct((t, heads * dv), BF16),
                   tab(64), tab(64), tab(32), tab(32)],
        scratch_shapes=[pltpu.VMEM((tm // LANES, LANES, LANES), F32)] * 2,
        compiler_params=_params("parallel"),
        name="ret_inproj",
    )(x2d, attn_gain.reshape(1, d), w, positions.reshape(1, t), inv.reshape(LANES, 1))

    log_gamma = jnp.log1p(-(2.0 ** (-5.0 - jnp.arange(heads, dtype=F32))))
    i = jnp.arange(chunk, dtype=F32)
    diff = i[:, None] - i[None, :]
    dm = jnp.where(diff >= 0, jnp.exp(log_gamma[:, None, None] * jnp.maximum(diff, 0.0)), 0.0)
    qd = jnp.exp(log_gamma[:, None] * (i + 1.0))
    kd = jnp.exp(log_gamma[:, None] * (chunk - 1.0 - i))
    cd = jnp.exp(log_gamma * chunk)
    qd = jnp.broadcast_to(qd[:, :, None], (heads, chunk, dv))
    kd = jnp.broadcast_to(kd[:, :, None], (heads, chunk, dk))
    cd = jnp.broadcast_to(cd[:, None, None], (heads, 1, dv))

    rb = min(RET_ROW_BLOCK, s)
    seq = lambda width: pl.BlockSpec((1, rb, width), lambda bi, r: (bi, r, 0))
    o = pl.pallas_call(
        functools.partial(_ret_core_kernel, heads=heads, chunk=chunk),
        grid=(b, s // rb),
        in_specs=[seq(heads * dk), seq(heads * dk), seq(heads * dv), seq(heads * dv),
                  _resident(dm.shape), _resident(qd.shape), _resident(kd.shape),
                  _resident(cd.shape)],
        out_specs=seq(heads * dv),
        out_shape=jax.ShapeDtypeStruct((b, s, heads * dv), BF16),
        scratch_shapes=[pltpu.VMEM((heads, dk, dv), F32)],
        compiler_params=_params("parallel", "arbitrary"),
        name="ret_core",
    )(q.reshape(b, s, -1), k.reshape(b, s, -1), v.reshape(b, s, -1), g.reshape(b, s, -1),
      dm, qd, kd, cd)
    return o.reshape(t, heads * dv), tuple(tables)


def _tail_kernel(a_ref, x_ref, wo_ref, gain_ref, wup_ref, wdn_ref, o_ref, *, ff_chunk):
    x1 = x_ref[...] + _dot(a_ref[...], wo_ref[...])
    hn = _rms(x1, gain_ref[...]).astype(BF16)
    acc = x1
    for c in range(wup_ref.shape[2] // ff_chunk):
        cols = slice(c * ff_chunk, (c + 1) * ff_chunk)
        u = jnp.maximum(_dot(hn, wup_ref[0, :, cols].astype(BF16)), 0.0)
        acc = acc + _dot((u * u).astype(BF16), wdn_ref[0, cols, :].astype(BF16))
    o_ref[...] = acc


def _mixer_out_and_mlp(a2d, x2d, w_out, mlp_gain, w_up_all, w_down_all, layer):
    t, d = x2d.shape
    tm = min(ROW_TILE, t)
    wo = w_out.astype(BF16)
    row = lambda width: pl.BlockSpec((tm, width), lambda i: (i, 0))
    of_layer = lambda w: pl.BlockSpec((1,) + w.shape[1:], lambda i: (layer, 0, 0),
                                      pipeline_mode=pl.Buffered(1))
    return pl.pallas_call(
        functools.partial(_tail_kernel, ff_chunk=1024),
        grid=(t // tm,),
        in_specs=[row(a2d.shape[1]), row(d), _resident(wo.shape), _resident((1, d)),
                  of_layer(w_up_all), of_layer(w_down_all)],
        out_specs=row(d),
        out_shape=jax.ShapeDtypeStruct((t, d), F32),
        compiler_params=_params("parallel"),
        name="outproj_mlp",
    )(a2d, x2d, wo, mlp_gain.reshape(1, d), w_up_all, w_down_all)


def _norm_rope_rows(z, gain, cos, sin):
    if gain is not None:
        z = z * lax.rsqrt(jnp.mean(z * z, axis=0, keepdims=True) + EPS) * gain
    half = z.shape[0] // 2
    x1, x2 = z[:half], z[half:]
    return x1 * cos - x2 * sin, x2 * cos + x1 * sin


def _dsa_inproj_kernel(x_ref, gain_ref, wt_ref, qg_ref, kg_ref, kig_ref, c64_ref, s64_ref,
                       c32_ref, s32_ref, qt_ref, k_ref, vt_ref, qit_ref, ki_ref, wit_ref, buf_ref,
                       *, heads, kv_heads, dh, idx_heads, di, wscale):
    hn = _rms(x_ref[...], gain_ref[...]).astype(BF16)
    tm = hn.shape[0]
    proj = _dot_nt(wt_ref[...], hn)
    c64, s64 = c64_ref[...], s64_ref[...]
    c32, s32 = c32_ref[...], s32_ref[...]
    qg = jnp.broadcast_to(qg_ref[...], (dh, tm))
    kg = jnp.broadcast_to(kg_ref[...], (dh, tm))
    for h in range(heads):
        o1, o2 = _norm_rope_rows(proj[h * dh:(h + 1) * dh], qg, c64, s64)
        qt_ref[0, h * dh:h * dh + dh // 2, :] = o1.astype(BF16)
        qt_ref[0, h * dh + dh // 2:(h + 1) * dh, :] = o2.astype(BF16)
    off = heads * dh
    for h in range(kv_heads):
        o1, o2 = _norm_rope_rows(proj[off + h * dh:off + (h + 1) * dh], kg, c64, s64)
        k_ref[:, h * dh:(h + 1) * dh] = jnp.concatenate([o1, o2], axis=0).T.astype(BF16)
    off += kv_heads * dh
    vrows = vt_ref.shape[1] // kv_heads
    for g in range(kv_heads):
        vt_ref[0, g * vrows:g * vrows + dh, :] = proj[off + g * dh:off + (g + 1) * dh].astype(BF16)
        vt_ref[0, g * vrows + dh:(g + 1) * vrows, :] = jnp.ones((vrows - dh, tm), BF16)
    off += kv_heads * dh
    for h in range(idx_heads):
        o1, o2 = _norm_rope_rows(proj[off + h * di:off + (h + 1) * di], None, c32, s32)
        qit_ref[0, h * di:h * di + di // 2, :] = o1.astype(BF16)
        qit_ref[0, h * di + di // 2:(h + 1) * di, :] = o2.astype(BF16)
    off += idx_heads * di
    kig = jnp.broadcast_to(kig_ref[...], (di, tm))
    o1, o2 = _norm_rope_rows(proj[off:off + di], kig, c32, s32)
    ki_t = jnp.concatenate([o1, o2, proj[off + di:off + LANES]], axis=0)
    ki_ref[...] = ki_t.T[:, :di].astype(BF16)
    off += LANES
    wit_ref[0] = proj[off:off + idx_heads] * wscale
    buf_ref[...] = jnp.zeros_like(buf_ref)


def _dsa_attn_kernel(*refs, tile, topk, heads, kv_heads, dh, idx_heads, index_steps):
    qt_ref, k_ref, vt_ref, qit_ref, ki_ref, wit_ref = refs[:6]
    o_ref, sc_ref, lg_ref, e_ref, cap_ref = refs[-5:]
    tq = qt_ref.shape[2]
    ke = k_ref.shape[1]
    di = qit_ref.shape[1] // idx_heads
    kb = KEY_BLOCK
    slabs = kb // SUBLANES
    j = tile
    nblk = ke // kb
    neg = float(jnp.finfo(F32).min)
    big = float(jnp.finfo(F32).max)
    diag = (lax.broadcasted_iota(jnp.int32, (tq, tq), 0)
            <= lax.broadcasted_iota(jnp.int32, (tq, tq), 1))

    def rows_of(i):
        return pl.ds(pl.multiple_of(i * kb, kb), kb)

    def full(val):
        return jnp.full((SUBLANES, tq), val, F32)

    def spread(row):
        return jnp.broadcast_to(row, (SUBLANES, tq))

    group = heads // kv_heads

    def raw_logits(head, i):
        g = head // group
        lg_ref[head, rows_of(i), :] = _dot(k_ref[0, rows_of(i), g * dh:(g + 1) * dh],
                                           qt_ref[0, head * dh:(head + 1) * dh, :])

    def sweep(fn, init, head=None, unrolled=False):
        unroll = nblk if unrolled else max(u for u in range(1, SWEEP_UNROLL + 1) if nblk % u == 0)
        acc = lax.fori_loop(0, nblk, lambda i, a: fn(i, sc_ref[rows_of(i), :], a), init,
                            unroll=unroll)
        if head is not None:
            g = head // group
            lg = _dot(k_ref[0, :, g * dh:(g + 1) * dh], qt_ref[0, head * dh:(head + 1) * dh, :])
            lg_ref[head] = lg
            acc = acc + jnp.minimum(jnp.maximum(lg[ke - SUBLANES:, :], 0.0), 0.0)
        return acc

    if ke <= topk:
        sc_ref[...] = jnp.where(diag, 0.0, neg)
        for h in range(heads):
            lax.fori_loop(0, nblk, lambda i, c, h=h: (raw_logits(h, i), c)[1], 0, unroll=True)
    else:
        wit = wit_ref[0]
        ki = ki_ref[0]
        score = jnp.zeros((ke, tq), F32)
        for h in range(idx_heads):
            rel = jnp.maximum(_dot(ki, qit_ref[0, h * di:(h + 1) * di, :]), 0.0)
            score = score + rel * wit[h:h + 1, :]
        last = score[ke - tq:, :]
        row_max = jnp.max(jnp.where(diag, last, neg), axis=0, keepdims=True)
        row_min = jnp.min(jnp.where(diag, last, big), axis=0, keepdims=True)
        if ke > tq:
            row_max = jnp.maximum(row_max, jnp.max(score[:ke - tq, :], axis=0, keepdims=True))
            row_min = jnp.minimum(row_min, jnp.min(score[:ke - tq, :], axis=0, keepdims=True))
            sc_ref[:ke - tq, :] = score[:ke - tq, :]
        sc_ref[ke - tq:, :] = jnp.where(diag, last, neg)
        q_pos = j * tq + lax.broadcasted_iota(jnp.int32, (1, tq), 1)
        kvec = jnp.minimum(q_pos + 1, topk).astype(F32)

        def count_ge(th, head=None, unrolled=False):
            thb = spread(th)[None]
            acc = sweep(lambda i, blk, a: a + jnp.sum(
                jnp.where(blk.reshape(slabs, SUBLANES, tq) >= thb, 1.0, 0.0), axis=0),
                jnp.zeros((SUBLANES, tq), F32), head, unrolled)
            return jnp.sum(acc, axis=0, keepdims=True)

        def max_below(hi):
            hib = spread(hi)[None]

            def step(i, blk, a):
                blk = blk.reshape(slabs, SUBLANES, tq)
                return jnp.maximum(a, jnp.max(jnp.where(blk < hib, blk, neg), axis=0))

            return jnp.max(sweep(step, full(neg)), axis=0, keepdims=True)

        steps = max(BISECT_STEPS, heads)
        carried = {(h * steps) // heads: h for h in range(heads)}
        lo, hi, c_hi = row_min, row_max + jnp.abs(row_max) * 1e-3 + 1e-3, jnp.zeros((1, tq), F32)
        for step in range(steps):
            mid = lo * 0.5 + hi * 0.5
            c = count_ge(mid, carried.get(step), unrolled=True)
            ge = c >= kvec
            lo, hi, c_hi = jnp.where(ge, mid, lo), jnp.where(ge, hi, mid), jnp.where(ge, c_hi, c)

        def count_and_below(v):
            vb = spread(v)[None]

            def step(i, blk, a):
                blk = blk.reshape(slabs, SUBLANES, tq)
                ge = blk >= vb
                return (a[0] + jnp.sum(jnp.where(ge, 1.0, 0.0), axis=0),
                        jnp.maximum(a[1], jnp.max(jnp.where(ge, neg, blk), axis=0)))

            cnt, below = sweep(step, (jnp.zeros((SUBLANES, tq), F32), full(neg)))
            return jnp.sum(cnt, axis=0, keepdims=True), jnp.max(below, axis=0, keepdims=True)

        def pending(cv):
            return jnp.max(jnp.where(cv < kvec, 1.0, 0.0))

        def refine(state):
            c_hi, v, cv, below, _ = state
            short = cv < kvec
            c_hi = jnp.where(short, cv, c_hi)
            v = jnp.where(short, below, v)
            cv, below = count_and_below(v)
            return c_hi, v, cv, below, pending(cv)

        v = max_below(hi)
        cv, below = count_and_below(v)
        c_hi, v, cv, _, _ = lax.while_loop(lambda st: st[4] > 0.0, refine,
                                           (c_hi, v, cv, below, pending(cv)))
        need = kvec - c_hi
        surplus = jnp.max(jnp.where(cv > kvec, 1.0, 0.0)) > 0.0
        vb = spread(v)

        @pl.when(jnp.logical_not(surplus))
        def _():
            def to_bias(i, carry):
                blk = sc_ref[rows_of(i), :].reshape(slabs, SUBLANES, tq)
                sc_ref[rows_of(i), :] = jnp.where(blk >= vb[None], 0.0, neg).reshape(kb, tq)
                return carry
            lax.fori_loop(0, nblk, to_bias, 0)

        @pl.when(surplus)
        def _():
            def key_idx(i):
                return (lax.broadcasted_iota(jnp.int32, (slabs, SUBLANES, tq), 0) * SUBLANES
                        + lax.broadcasted_iota(jnp.int32, (slabs, SUBLANES, tq), 1) + i * kb)

            def idx_step(_, carry):
                lo_i, hi_i = carry
                mid_i = (lo_i + hi_i) >> 1
                midb = jnp.broadcast_to(mid_i, (SUBLANES, tq))
                acc = sweep(lambda i, blk, a: a + jnp.sum(jnp.where(
                    (blk.reshape(slabs, SUBLANES, tq) == vb[None]) & (key_idx(i) <= midb[None]),
                    1.0, 0.0), axis=0), jnp.zeros((SUBLANES, tq), F32))
                ok = jnp.sum(acc, axis=0, keepdims=True) >= need
                return jnp.where(ok, lo_i, mid_i), jnp.where(ok, mid_i, hi_i)

            _, istar = lax.fori_loop(0, index_steps, idx_step,
                                     (jnp.full((1, tq), -1, jnp.int32),
                                      jnp.full((1, tq), nblk * kb - 1, jnp.int32)))
            istarb = jnp.broadcast_to(istar, (SUBLANES, tq))

            def to_bias(i, carry):
                blk = sc_ref[rows_of(i), :].reshape(slabs, SUBLANES, tq)
                keep = (blk > vb[None]) | ((blk == vb[None]) & (key_idx(i) <= istarb[None]))
                sc_ref[rows_of(i), :] = jnp.where(keep, 0.0, neg).reshape(kb, tq)
                return carry
            lax.fori_loop(0, nblk, to_bias, 0)

    cap_val = float(jnp.finfo(BF16).max)

    packed = 2 * SUBLANES

    def max_pass(i, maxima):
        rows = rows_of(i)
        selected = sc_ref[rows, :] == 0.0
        cap_ref[rows, :] = jnp.where(selected, cap_val, 0.0).astype(BF16)
        bias = jnp.where(selected, 0.0, -cap_val).astype(BF16)
        new = []
        for h in range(heads):
            lg = lg_ref[h, rows, :].astype(BF16) + bias
            new.append(jnp.maximum(maxima[h], jnp.max(lg.reshape(kb // packed, packed, tq), axis=0)))
        return tuple(new)

    maxima = lax.fori_loop(0, nblk, max_pass, (jnp.full((packed, tq), -cap_val, BF16),) * heads,
                           unroll=min(nblk, 2))
    maxima = [m16.astype(F32) for m16 in maxima]

    vrows = vt_ref.shape[1] // kv_heads
    for h in range(heads):
        g = h // group
        m = jnp.max(maxima[h], axis=0, keepdims=True)
        for c in range(ke // tq):
            rows = slice(c * tq, (c + 1) * tq)
            e = jnp.exp2(lg_ref[h, rows, :] - m).astype(BF16)
            e_ref[h, rows, :] = jnp.minimum(e, cap_ref[rows, :])
        out_t = _dot(vt_ref[0, g * vrows:(g + 1) * vrows, :], e_ref[h])
        o_ref[0, :, h * dh:(h + 1) * dh] = (out_t[:dh] / out_t[dh:dh + 1]).T.astype(BF16)


def _sparse_attention_mixer(x2d, b, s, attn_gain, w_in, q_gain, k_gain, kidx_gain, tabs):
    t, d = x2d.shape
    heads, kv_heads, idx_heads, di = ATT_HEADS, ATT_KV_HEADS, IDX_HEADS, IDX_DH
    dh = d // heads
    c64, s64, c32, s32 = tabs
    tm = min(ROW_TILE, s)
    per_b = s // tm
    nq, nkv, nqi = heads * dh, kv_heads * dh, idx_heads * di
    nvt = kv_heads * (dh + 2 * SUBLANES)
    n_main = nq + 2 * nkv + nqi + di
    zeros = lambda n: jnp.zeros((d, n), w_in.dtype)
    wt = jnp.concatenate([w_in[:, :n_main], zeros(LANES - di), w_in[:, n_main:n_main + idx_heads],
                          zeros(2 * SUBLANES - idx_heads)], axis=1).T.astype(BF16)
    qg = (q_gain * (dh ** -0.5 * math.log2(math.e))).reshape(dh, 1)
    row = lambda width: pl.BlockSpec((tm, width), lambda i: (i, 0))
    col = lambda n: pl.BlockSpec((n, tm), lambda i: (0, i))
    seq_t = lambda n: pl.BlockSpec((1, n, tm), lambda i: (i // per_b, 0, i % per_b))
    shp = lambda width, dt: jax.ShapeDtypeStruct((t, width), dt)
    shp_t = lambda n, dt: jax.ShapeDtypeStruct((b, n, s), dt)
    qt, k, vt, qit, ki, wit, out = pl.pallas_call(
        functools.partial(_dsa_inproj_kernel, heads=heads, kv_heads=kv_heads, dh=dh,
                          idx_heads=idx_heads, di=di, wscale=idx_heads ** -0.5 * di ** -0.5),
        grid=(t // tm,),
        in_specs=[row(d), _resident((1, d)), _resident(wt.shape), _resident((dh, 1)),
                  _resident((dh, 1)), _resident((di, 1)), col(dh // 2), col(dh // 2),
                  col(di // 2), col(di // 2)],
        out_specs=[seq_t(nq), row(nkv), seq_t(nvt), seq_t(nqi), row(di), seq_t(idx_heads),
                   row(nq)],
        out_shape=[shp_t(nq, BF16), shp(nkv, BF16), shp_t(nvt, BF16), shp_t(nqi, BF16),
                   shp(di, BF16), shp_t(idx_heads, F32), shp(nq, BF16)],
        compiler_params=_params("parallel"),
        name="dsa_inproj",
    )(x2d, attn_gain.reshape(1, d), wt, qg, k_gain.reshape(dh, 1), kidx_gain.reshape(di, 1),
      c64, s64, c32, s32)

    topk = min(TOPK_MAX, s // 4)
    tq = min(Q_TILE, s)
    k, ki = k.reshape(b, s, nkv), ki.reshape(b, s, di)
    out = out.reshape(b, s, nq)
    for tile in range(s // tq):
        ke = (tile + 1) * tq
        operands = [qt, k, vt, qit, ki, wit]
        in_specs = [pl.BlockSpec((1, nq, tq), lambda bi, tile=tile: (bi, 0, tile)),
                    pl.BlockSpec((1, ke, nkv), lambda bi: (bi, 0, 0)),
                    pl.BlockSpec((1, nvt, ke), lambda bi: (bi, 0, 0)),
                    pl.BlockSpec((1, nqi, tq), lambda bi, tile=tile: (bi, 0, tile)),
                    pl.BlockSpec((1, ke, di), lambda bi: (bi, 0, 0)),
                    pl.BlockSpec((1, idx_heads, tq), lambda bi, tile=tile: (bi, 0, tile))]
        operands.append(out)
        in_specs.append(pl.BlockSpec(memory_space=pl.ANY))
        out = pl.pallas_call(
            functools.partial(_dsa_attn_kernel, tile=tile, topk=topk, heads=heads,
                              kv_heads=kv_heads, dh=dh, idx_heads=idx_heads,
                              index_steps=(ke - 1).bit_length() + 1),
            grid=(b,),
            in_specs=in_specs,
            out_specs=pl.BlockSpec((1, tq, nq), lambda bi, tile=tile: (bi, tile, 0)),
            out_shape=jax.ShapeDtypeStruct((b, s, nq), BF16),
            scratch_shapes=[pltpu.VMEM((ke, tq), F32), pltpu.VMEM((heads, ke, tq), F32),
                            pltpu.VMEM((heads, ke, tq), BF16), pltpu.VMEM((ke, tq), BF16)],
            input_output_aliases={len(operands) - 1: 0},
            compiler_params=_params("parallel"),
            name=f"dsa_attn_{tile}",
        )(*operands)
    return out.reshape(t, nq)


def kernel(x, positions, attn_norm, ret_w_in, ret_out_norm, ret_w_out, dsa_w_in, dsa_q_norm,
           dsa_k_norm, dsa_kidx_norm, dsa_w_out, mlp_norm, mlp_w_up, mlp_w_down):
    b, s, d = x.shape
    depth = attn_norm.shape[0]
    x2d = x.reshape(b * s, d)
    tables = None
    for i in range(depth):
        j = i // 2
        if i % 2 == 0:
            a, made = _retention_mixer(x2d, b, s, attn_norm[i], ret_w_in[j], positions)
            tables = made if tables is None else tables
            w_out = ret_out_norm[j].reshape(-1, 1) * ret_w_out[j]
        else:
            a = _sparse_attention_mixer(x2d, b, s, attn_norm[i], dsa_w_in[j], dsa_q_norm[j],
                                        dsa_k_norm[j], dsa_kidx_norm[j], tables)
            w_out = dsa_w_out[j]
        x2d = _mixer_out_and_mlp(a, x2d, w_out, mlp_norm[i], mlp_w_up, mlp_w_down, i)
    return x2d.reshape(b, s, d)
```

```python
import functools
import math

import jax
import jax.numpy as jnp
from jax import lax
from jax.experimental import pallas as pl
from jax.experimental.pallas import tpu as pltpu

F32 = jnp.float32
BF16 = jnp.bfloat16

EPS = 1e-6
ROPE_THETA = 10000.0
RET_HEADS = 4
RET_CHUNK = 256
ATT_HEADS = 8
ATT_KV_HEADS = 2
IDX_HEADS = 8
IDX_DH = 64
TOPK_MAX = 256
LANES = 128
SUBLANES = 8

ROW_TILE = 512
RET_ROW_BLOCK = 1024
Q_TILE = 256
KEY_BLOCK = 128
SWEEP_UNROLL = 8
VMEM_LIMIT = 56 * 1024 * 1024
BISECT_STEPS = 13


def _params(*sem):
    return pltpu.CompilerParams(dimension_semantics=sem, vmem_limit_bytes=VMEM_LIMIT)


def _resident(shape):
    nd = len(shape)
    return pl.BlockSpec(shape, lambda *_: (0,) * nd, pipeline_mode=pl.Buffered(1))


def _rms(x, gain):
    return x * lax.rsqrt(jnp.mean(x * x, axis=-1, keepdims=True) + EPS) * gain


def _dot(a, b):
    return jnp.dot(a, b, preferred_element_type=F32)


def _dot_nt(a, b):
    return lax.dot_general(a, b, (((1,), (1,)), ((), ())), preferred_element_type=F32)


def _dot_tn(a, b):
    return lax.dot_general(a, b, (((0,), (0,)), ((), ())), preferred_element_type=F32)


def _rope_slab_in_kernel(c, pos_ref, inv_ref, c64t_ref, s64t_ref, c32t_ref, s32t_ref, ct_ref,
                         st_ref):
    lanes = slice(c * LANES, (c + 1) * LANES)
    ang_t = inv_ref[...] * pos_ref[:, lanes].astype(F32)
    cos_t = jnp.cos(ang_t)
    sin_t = jnp.sin(ang_t)
    ct_ref[c] = cos_t
    st_ref[c] = sin_t
    for step, c_ref, s_ref in ((2, c64t_ref, s64t_ref), (4, c32t_ref, s32t_ref)):
        rows = pl.ds(0, LANES // step, stride=step)
        c_ref[:, lanes] = ct_ref[c, rows, :]
        s_ref[:, lanes] = st_ref[c, rows, :]
    return cos_t.T, sin_t.T


def _ret_inproj_kernel(x_ref, gain_ref, w_ref, pos_ref, inv_ref, q_ref, k_ref, v_ref, g_ref,
                       c64t_ref, s64t_ref, c32t_ref, s32t_ref, ct_ref, st_ref,
                       *, heads, dk, dv, kscale):
    hn = _rms(x_ref[...], gain_ref[...]).astype(BF16)
    half = dk // 2
    width = heads * dv
    for off, out_ref, swish in ((2 * heads * dk + width, g_ref, True),
                                (2 * heads * dk, v_ref, False)):
        for c in range(width // dv):
            z = _dot(hn, w_ref[:, off + c * dv:off + (c + 1) * dv].astype(BF16))
            if swish:
                z = z * (0.5 * jnp.tanh(0.5 * z) + 0.5)
            out_ref[:, c * dv:(c + 1) * dv] = z.astype(BF16)
    trig = [_rope_slab_in_kernel(c, pos_ref, inv_ref, c64t_ref, s64t_ref, c32t_ref, s32t_ref,
                                 ct_ref, st_ref) for c in range(ct_ref.shape[0])]
    cos = jnp.concatenate([t[0] for t in trig], axis=0)
    sin = jnp.concatenate([t[1] for t in trig], axis=0)
    for h in range(heads):
        for off, out_ref, scale in ((0, q_ref, None), (heads * dk, k_ref, kscale)):
            z = _dot(hn, w_ref[:, off + h * dk:off + (h + 1) * dk].astype(BF16))
            x1, x2 = z[:, :half], z[:, half:]
            o1 = x1 * cos - x2 * sin
            o2 = x2 * cos + x1 * sin
            if scale is not None:
                o1, o2 = o1 * scale, o2 * scale
            out_ref[:, h * dk:h * dk + half] = o1.astype(BF16)
            out_ref[:, h * dk + half:(h + 1) * dk] = o2.astype(BF16)


def _ret_core_kernel(q_ref, k_ref, v_ref, g_ref, dm_ref, qd_ref, kd_ref, cd_ref,
                     o_ref, state_ref, *, heads, chunk):
    @pl.when(pl.program_id(1) == 0)
    def _():
        state_ref[...] = jnp.zeros_like(state_ref)

    dk = q_ref.shape[2] // heads
    dv = v_ref.shape[2] // heads
    hs = range(heads)
    states = [state_ref[h] for h in hs]
    for c in range(q_ref.shape[1] // chunk):
        rows = slice(c * chunk, (c + 1) * chunk)
        q = [q_ref[0, rows, h * dk:(h + 1) * dk] for h in hs]
        k = [k_ref[0, rows, h * dk:(h + 1) * dk] for h in hs]
        v = [v_ref[0, rows, h * dv:(h + 1) * dv] for h in hs]
        scores = [_dot_nt(q[h], k[h]) for h in hs]
        kv = [_dot_tn((k[h].astype(F32) * kd_ref[h]).astype(BF16), v[h]) for h in hs]
        cross = [_dot(q[h], states[h].astype(BF16)) for h in hs]
        intra = [_dot((scores[h] * dm_ref[h]).astype(BF16), v[h]) for h in hs]
        for h in hs:
            o = intra[h] + cross[h] * qd_ref[h]
            y = o * lax.rsqrt(jnp.mean(o * o, axis=-1, keepdims=True) + EPS)
            gate = g_ref[0, rows, h * dv:(h + 1) * dv].astype(F32)
            o_ref[0, rows, h * dv:(h + 1) * dv] = (y * gate).astype(BF16)
            states[h] = states[h] * cd_ref[h] + kv[h]
    for h in hs:
        state_ref[h] = states[h]


def _retention_mixer(x2d, b, s, attn_gain, w_in, positions):
    t, d = x2d.shape
    heads = RET_HEADS
    dk = d // heads
    dv = 2 * dk
    chunk = RET_CHUNK
    tm = min(ROW_TILE, t)
    w = w_in
    row = lambda width: pl.BlockSpec((tm, width), lambda i: (i, 0))
    col = lambda n: pl.BlockSpec((n, tm), lambda i: (0, i))
    tab = lambda n: jax.ShapeDtypeStruct((n, t), F32)
    inv = ROPE_THETA ** (-jnp.arange(LANES, dtype=F32) / LANES)
    q, k, v, g, *tables = pl.pallas_call(
        functools.partial(_ret_inproj_kernel, heads=heads, dk=dk, dv=dv, kscale=dk ** -0.5),
        grid=(t // tm,),
        in_specs=[row(d), _resident((1, d)), _resident(w.shape),
                  pl.BlockSpec((1, tm), lambda i: (0, i)), _resident((LANES, 1))],
        out_specs=[row(heads * dk), row(heads * dk), row(heads * dv), row(heads * dv),
                   col(64), col(64), col(32), col(32)],
        out_shape=[jax.ShapeDtypeStruct((t, heads * dk), BF16),
                   jax.ShapeDtypeStruct((t, heads * dk), BF16),
                   jax.ShapeDtypeStruct((t, heads * dv), BF16),
                   jax.ShapeDtypeStruct((t, heads * dv), BF16),
                   tab(64), tab(64), tab(32), tab(32)],
        scratch_shapes=[pltpu.VMEM((tm // LANES, LANES, LANES), F32)] * 2,
        compiler_params=_params("parallel"),
        name="ret_inproj",
    )(x2d, attn_gain.reshape(1, d), w, positions.reshape(1, t), inv.reshape(LANES, 1))

    log_gamma = jnp.log1p(-(2.0 ** (-5.0 - jnp.arange(heads, dtype=F32))))
    i = jnp.arange(chunk, dtype=F32)
    diff = i[:, None] - i[None, :]
    dm = jnp.where(diff >= 0, jnp.exp(log_gamma[:, None, None] * jnp.maximum(diff, 0.0)), 0.0)
    qd = jnp.exp(log_gamma[:, None] * (i + 1.0))
    kd = jnp.exp(log_gamma[:, None] * (chunk - 1.0 - i))
    cd = jnp.exp(log_gamma * chunk)
    qd = jnp.broadcast_to(qd[:, :, None], (heads, chunk, dv))
    kd = jnp.broadcast_to(kd[:, :, None], (heads, chunk, dk))
    cd = jnp.broadcast_to(cd[:, None, None], (heads, 1, dv))

    rb = min(RET_ROW_BLOCK, s)
    seq = lambda width: pl.BlockSpec((1, rb, width), lambda bi, r: (bi, r, 0))
    o = pl.pallas_call(
        functools.partial(_ret_core_kernel, heads=heads, chunk=chunk),
        grid=(b, s // rb),
        in_specs=[seq(heads * dk), seq(heads * dk), seq(heads * dv), seq(heads * dv),
                  _resident(dm.shape), _resident(qd.shape), _resident(kd.shape),
                  _resident(cd.shape)],
        out_specs=seq(heads * dv),
        out_shape=jax.ShapeDtypeStruct((b, s, heads * dv), BF16),
        scratch_shapes=[pltpu.VMEM((heads, dk, dv), F32)],
        compiler_params=_params("parallel", "arbitrary"),
        name="ret_core",
    )(q.reshape(b, s, -1), k.reshape(b, s, -1), v.reshape(b, s, -1), g.reshape(b, s, -1),
      dm, qd, kd, cd)
    return o.reshape(t, heads * dv), tuple(tables)


def _tail_kernel(a_ref, x_ref, wo_ref, gain_ref, wup_ref, wdn_ref, o_ref, *, ff_chunk):
    x1 = x_ref[...] + _dot(a_ref[...], wo_ref[...])
    hn = _rms(x1, gain_ref[...]).astype(BF16)
    acc = x1
    for c in range(wup_ref.shape[2] // ff_chunk):
        cols = slice(c * ff_chunk, (c + 1) * ff_chunk)
        u = jnp.maximum(_dot(hn, wup_ref[0, :, cols].astype(BF16)), 0.0)
        acc = acc + _dot((u * u).astype(BF16), wdn_ref[0, cols, :].astype(BF16))
    o_ref[...] = acc


def _mixer_out_and_mlp(a2d, x2d, w_out, mlp_gain, w_up_all, w_down_all, layer):
    t, d = x2d.shape
    tm = min(ROW_TILE, t)
    wo = w_out.astype(BF16)
    row = lambda width: pl.BlockSpec((tm, width), lambda i: (i, 0))
    of_layer = lambda w: pl.BlockSpec((1,) + w.shape[1:], lambda i: (layer, 0, 0),
                                      pipeline_mode=pl.Buffered(1))
    return pl.pallas_call(
        functools.partial(_tail_kernel, ff_chunk=1024),
        grid=(t // tm,),
        in_specs=[row(a2d.shape[1]), row(d), _resident(wo.shape), _resident((1, d)),
                  of_layer(w_up_all), of_layer(w_down_all)],
        out_specs=row(d),
        out_shape=jax.ShapeDtypeStruct((t, d), F32),
        compiler_params=_params("parallel"),
        name="outproj_mlp",
    )(a2d, x2d, wo, mlp_gain.reshape(1, d), w_up_all, w_down_all)


def _norm_rope_rows(z, gain, cos, sin):
    if gain is not None:
        z = z * lax.rsqrt(jnp.mean(z * z, axis=0, keepdims=True) + EPS) * gain
    half = z.shape[0] // 2
    x1, x2 = z[:half], z[half:]
    return x1 * cos - x2 * sin, x2 * cos + x1 * sin


def _dsa_inproj_kernel(x_ref, gain_ref, wt_ref, qg_ref, kg_ref, kig_ref, c64_ref, s64_ref,
                       c32_ref, s32_ref, qt_ref, k_ref, vt_ref, qit_ref, ki_ref, wit_ref, buf_ref,
                       *, heads, kv_heads, dh, idx_heads, di, wscale):
    hn = _rms(x_ref[...], gain_ref[...]).astype(BF16)
    tm = hn.shape[0]
    proj = _dot_nt(wt_ref[...], hn)
    c64, s64 = c64_ref[...], s64_ref[...]
    c32, s32 = c32_ref[...], s32_ref[...]
    qg = jnp.broadcast_to(qg_ref[...], (dh, tm))
    kg = jnp.broadcast_to(kg_ref[...], (dh, tm))
    for h in range(heads):
        o1, o2 = _norm_rope_rows(proj[h * dh:(h + 1) * dh], qg, c64, s64)
        qt_ref[0, h * dh:h * dh + dh // 2, :] = o1.astype(BF16)
        qt_ref[0, h * dh + dh // 2:(h + 1) * dh, :] = o2.astype(BF16)
    off = heads * dh
    for h in range(kv_heads):
        o1, o2 = _norm_rope_rows(proj[off + h * dh:off + (h + 1) * dh], kg, c64, s64)
        k_ref[:, h * dh:(h + 1) * dh] = jnp.concatenate([o1, o2], axis=0).T.astype(BF16)
    off += kv_heads * dh
    vrows = vt_ref.shape[1] // kv_heads
    for g in range(kv_heads):
        vt_ref[0, g * vrows:g * vrows + dh, :] = proj[off + g * dh:off + (g + 1) * dh].astype(BF16)
        vt_ref[0, g * vrows + dh:(g + 1) * vrows, :] = jnp.ones((vrows - dh, tm), BF16)
    off += kv_heads * dh
    for h in range(idx_heads):
        o1, o2 = _norm_rope_rows(proj[off + h * di:off + (h + 1) * di], None, c32, s32)
        qit_ref[0, h * di:h * di + di // 2, :] = o1.astype(BF16)
        qit_ref[0, h * di + di // 2:(h + 1) * di, :] = o2.astype(BF16)
    off += idx_heads * di
    kig = jnp.broadcast_to(kig_ref[...], (di, tm))
    o1, o2 = _norm_rope_rows(proj[off:off + di], kig, c32, s32)
    ki_t = jnp.concatenate([o1, o2, proj[off + di:off + LANES]], axis=0)
    ki_ref[...] = ki_t.T[:, :di].astype(BF16)
    off += LANES
    wit_ref[0] = proj[off:off + idx_heads] * wscale
    buf_ref[...] = jnp.zeros_like(buf_ref)


def _dsa_attn_kernel(*refs, tile, topk, heads, kv_heads, dh, idx_heads, index_steps):
    qt_ref, k_ref, vt_ref, qit_ref, ki_ref, wit_ref = refs[:6]
    o_ref, sc_ref, lg_ref, e_ref, cap_ref = refs[-5:]
    tq = qt_ref.shape[2]
    ke = k_ref.shape[1]
    di = qit_ref.shape[1] // idx_heads
    kb = KEY_BLOCK
    slabs = kb // SUBLANES
    j = tile
    nblk = ke // kb
    neg = float(jnp.finfo(F32).min)
    big = float(jnp.finfo(F32).max)
    diag = (lax.broadcasted_iota(jnp.int32, (tq, tq), 0)
            <= lax.broadcasted_iota(jnp.int32, (tq, tq), 1))

    def rows_of(i):
        return pl.ds(pl.multiple_of(i * kb, kb), kb)

    def full(val):
        return jnp.full((SUBLANES, tq), val, F32)

    def spread(row):
        return jnp.broadcast_to(row, (SUBLANES, tq))

    group = heads // kv_heads

    def raw_logits(head, i):
        g = head // group
        lg_ref[head, rows_of(i), :] = _dot(k_ref[0, rows_of(i), g * dh:(g + 1) * dh],
                                           qt_ref[0, head * dh:(head + 1) * dh, :])

    def sweep(fn, init, head=None, unrolled=False):
        unroll = nblk if unrolled else max(u for u in range(1, SWEEP_UNROLL + 1) if nblk % u == 0)
        acc = lax.fori_loop(0, nblk, lambda i, a: fn(i, sc_ref[rows_of(i), :], a), init,
                            unroll=unroll)
        if head is not None:
            g = head // group
            lg = _dot(k_ref[0, :, g * dh:(g + 1) * dh], qt_ref[0, head * dh:(head + 1) * dh, :])
            lg_ref[head, :ke, :] = lg
            acc = acc + jnp.minimum(jnp.maximum(lg[ke - SUBLANES:, :], 0.0), 0.0)
        return acc

    if ke <= topk:
        sc_ref[:ke, :] = jnp.where(diag, 0.0, neg)
        for h in range(heads):
            lax.fori_loop(0, nblk, lambda i, c, h=h: (raw_logits(h, i), c)[1], 0, unroll=True)
    else:
        wit = wit_ref[0]
        ki = ki_ref[0]
        score = jnp.zeros((ke, tq), F32)
        for h in range(idx_heads):
            rel = jnp.maximum(_dot(ki, qit_ref[0, h * di:(h + 1) * di, :]), 0.0)
            score = score + rel * wit[h:h + 1, :]
        last = score[ke - tq:, :]
        row_max = jnp.max(jnp.where(diag, last, neg), axis=0, keepdims=True)
        row_min = jnp.min(jnp.where(diag, last, big), axis=0, keepdims=True)
        if ke > tq:
            row_max = jnp.maximum(row_max, jnp.max(score[:ke - tq, :], axis=0, keepdims=True))
            row_min = jnp.minimum(row_min, jnp.min(score[:ke - tq, :], axis=0, keepdims=True))
            sc_ref[:ke - tq, :] = score[:ke - tq, :]
        sc_ref[ke - tq:ke, :] = jnp.where(diag, last, neg)
        q_pos = j * tq + lax.broadcasted_iota(jnp.int32, (1, tq), 1)
        kvec = jnp.minimum(q_pos + 1, topk).astype(F32)

        def count_ge(th, head=None, unrolled=False):
            thb = spread(th)[None]
            acc = sweep(lambda i, blk, a: a + jnp.sum(
                jnp.where(blk.reshape(slabs, SUBLANES, tq) >= thb, 1.0, 0.0), axis=0),
                jnp.zeros((SUBLANES, tq), F32), head, unrolled)
            return jnp.sum(acc, axis=0, keepdims=True)

        def max_below(hi):
            hib = spread(hi)[None]

            def step(i, blk, a):
                blk = blk.reshape(slabs, SUBLANES, tq)
                return jnp.maximum(a, jnp.max(jnp.where(blk < hib, blk, neg), axis=0))

            return jnp.max(sweep(step, full(neg)), axis=0, keepdims=True)

        steps = max(BISECT_STEPS, heads)
        carried = {(h * steps) // heads: h for h in range(heads)}
        lo, hi, c_hi = row_min, row_max + jnp.abs(row_max) * 1e-3 + 1e-3, jnp.zeros((1, tq), F32)
        for step in range(steps):
            mid = lo * 0.5 + hi * 0.5
            c = count_ge(mid, carried.get(step), unrolled=True)
            ge = c >= kvec
            lo, hi, c_hi = jnp.where(ge, mid, lo), jnp.where(ge, hi, mid), jnp.where(ge, c_hi, c)

        def count_and_below(v):
            vb = spread(v)[None]

            def step(i, blk, a):
                blk = blk.reshape(slabs, SUBLANES, tq)
                ge = blk >= vb
                return (a[0] + jnp.sum(jnp.where(ge, 1.0, 0.0), axis=0),
                        jnp.maximum(a[1], jnp.max(jnp.where(ge, neg, blk), axis=0)))

            cnt, below = sweep(step, (jnp.zeros((SUBLANES, tq), F32), full(neg)))
            return jnp.sum(cnt, axis=0, keepdims=True), jnp.max(below, axis=0, keepdims=True)

        def pending(cv):
            return jnp.max(jnp.where(cv < kvec, 1.0, 0.0))

        def refine(state):
            c_hi, v, cv, below, _ = state
            short = cv < kvec
            c_hi = jnp.where(short, cv, c_hi)
            v = jnp.where(short, below, v)
            cv, below = count_and_below(v)
            return c_hi, v, cv, below, pending(cv)

        v = max_below(hi)
        cv, below = count_and_below(v)
        c_hi, v, cv, _, _ = lax.while_loop(lambda st: st[4] > 0.0, refine,
                                           (c_hi, v, cv, below, pending(cv)))
        need = kvec - c_hi
        surplus = jnp.max(jnp.where(cv > kvec, 1.0, 0.0)) > 0.0
        vb = spread(v)

        @pl.when(jnp.logical_not(surplus))
        def _():
            def to_bias(i, carry):
                blk = sc_ref[rows_of(i), :].reshape(slabs, SUBLANES, tq)
                sc_ref[rows_of(i), :] = jnp.where(blk >= vb[None], 0.0, neg).reshape(kb, tq)
                return carry
            lax.fori_loop(0, nblk, to_bias, 0)

        @pl.when(surplus)
        def _():
            def key_idx(i):
                return (lax.broadcasted_iota(jnp.int32, (slabs, SUBLANES, tq), 0) * SUBLANES
                        + lax.broadcasted_iota(jnp.int32, (slabs, SUBLANES, tq), 1) + i * kb)

            def idx_step(_, carry):
                lo_i, hi_i = carry
                mid_i = (lo_i + hi_i) >> 1
                midb = jnp.broadcast_to(mid_i, (SUBLANES, tq))
                acc = sweep(lambda i, blk, a: a + jnp.sum(jnp.where(
                    (blk.reshape(slabs, SUBLANES, tq) == vb[None]) & (key_idx(i) <= midb[None]),
                    1.0, 0.0), axis=0), jnp.zeros((SUBLANES, tq), F32))
                ok = jnp.sum(acc, axis=0, keepdims=True) >= need
                return jnp.where(ok, lo_i, mid_i), jnp.where(ok, mid_i, hi_i)

            _, istar = lax.fori_loop(0, index_steps, idx_step,
                                     (jnp.full((1, tq), -1, jnp.int32),
                                      jnp.full((1, tq), nblk * kb - 1, jnp.int32)))
            istarb = jnp.broadcast_to(istar, (SUBLANES, tq))

            def to_bias(i, carry):
                blk = sc_ref[rows_of(i), :].reshape(slabs, SUBLANES, tq)
                keep = (blk > vb[None]) | ((blk == vb[None]) & (key_idx(i) <= istarb[None]))
                sc_ref[rows_of(i), :] = jnp.where(keep, 0.0, neg).reshape(kb, tq)
                return carry
            lax.fori_loop(0, nblk, to_bias, 0)

    cap_val = float(jnp.finfo(BF16).max)

    packed = 2 * SUBLANES

    def max_pass(i, maxima):
        rows = rows_of(i)
        selected = sc_ref[rows, :] == 0.0
        cap_ref[rows, :] = jnp.where(selected, cap_val, 0.0).astype(BF16)
        bias = jnp.where(selected, 0.0, -cap_val).astype(BF16)
        new = []
        for h in range(heads):
            lg = lg_ref[h, rows, :].astype(BF16) + bias
            new.append(jnp.maximum(maxima[h], jnp.max(lg.reshape(kb // packed, packed, tq), axis=0)))
        return tuple(new)

    maxima = lax.fori_loop(0, nblk, max_pass, (jnp.full((packed, tq), -cap_val, BF16),) * heads,
                           unroll=min(nblk, 2))
    maxima = [m16.astype(F32) for m16 in maxima]

    vrows = vt_ref.shape[1] // kv_heads
    for h in range(heads):
        g = h // group
        m = jnp.max(maxima[h], axis=0, keepdims=True)
        for c in range(ke // tq):
            rows = slice(c * tq, (c + 1) * tq)
            e = jnp.exp2(lg_ref[h, rows, :] - m).astype(BF16)
            e_ref[h, rows, :] = jnp.minimum(e, cap_ref[rows, :])
        out_t = _dot(vt_ref[0, g * vrows:(g + 1) * vrows, :], e_ref[h, :ke, :])
        o_ref[0, :, h * dh:(h + 1) * dh] = (out_t[:dh] / out_t[dh:dh + 1]).T.astype(BF16)


def _sparse_attention_mixer(x2d, b, s, attn_gain, w_in, q_gain, k_gain, kidx_gain, tabs):
    t, d = x2d.shape
    heads, kv_heads, idx_heads, di = ATT_HEADS, ATT_KV_HEADS, IDX_HEADS, IDX_DH
    dh = d // heads
    c64, s64, c32, s32 = tabs
    tm = min(ROW_TILE, s)
    per_b = s // tm
    nq, nkv, nqi = heads * dh, kv_heads * dh, idx_heads * di
    nvt = kv_heads * (dh + 2 * SUBLANES)
    n_main = nq + 2 * nkv + nqi + di
    zeros = lambda n: jnp.zeros((d, n), w_in.dtype)
    wt = jnp.concatenate([w_in[:, :n_main], zeros(LANES - di), w_in[:, n_main:n_main + idx_heads],
                          zeros(2 * SUBLANES - idx_heads)], axis=1).T.astype(BF16)
    qg = (q_gain * (dh ** -0.5 * math.log2(math.e))).reshape(dh, 1)
    row = lambda width: pl.BlockSpec((tm, width), lambda i: (i, 0))
    col = lambda n: pl.BlockSpec((n, tm), lambda i: (0, i))
    seq_t = lambda n: pl.BlockSpec((1, n, tm), lambda i: (i // per_b, 0, i % per_b))
    shp = lambda width, dt: jax.ShapeDtypeStruct((t, width), dt)
    shp_t = lambda n, dt: jax.ShapeDtypeStruct((b, n, s), dt)
    qt, k, vt, qit, ki, wit, out = pl.pallas_call(
        functools.partial(_dsa_inproj_kernel, heads=heads, kv_heads=kv_heads, dh=dh,
                          idx_heads=idx_heads, di=di, wscale=idx_heads ** -0.5 * di ** -0.5),
        grid=(t // tm,),
        in_specs=[row(d), _resident((1, d)), _resident(wt.shape), _resident((dh, 1)),
                  _resident((dh, 1)), _resident((di, 1)), col(dh // 2), col(dh // 2),
                  col(di // 2), col(di // 2)],
        out_specs=[seq_t(nq), row(nkv), seq_t(nvt), seq_t(nqi), row(di), seq_t(idx_heads),
                   row(nq)],
        out_shape=[shp_t(nq, BF16), shp(nkv, BF16), shp_t(nvt, BF16), shp_t(nqi, BF16),
                   shp(di, BF16), shp_t(idx_heads, F32), shp(nq, BF16)],
        compiler_params=_params("parallel"),
        name="dsa_inproj",
    )(x2d, attn_gain.reshape(1, d), wt, qg, k_gain.reshape(dh, 1), kidx_gain.reshape(di, 1),
      c64, s64, c32, s32)

    topk = min(TOPK_MAX, s // 4)
    tq = min(Q_TILE, s)
    k, ki = k.reshape(b, s, nkv), ki.reshape(b, s, di)
    out = out.reshape(b, s, nq)
    for tile in range(s // tq):
        ke = (tile + 1) * tq
        operands = [qt, k, vt, qit, ki, wit]
        in_specs = [pl.BlockSpec((1, nq, tq), lambda bi, tile=tile: (bi, 0, tile)),
                    pl.BlockSpec((1, ke, nkv), lambda bi: (bi, 0, 0)),
                    pl.BlockSpec((1, nvt, ke), lambda bi: (bi, 0, 0)),
                    pl.BlockSpec((1, nqi, tq), lambda bi, tile=tile: (bi, 0, tile)),
                    pl.BlockSpec((1, ke, di), lambda bi: (bi, 0, 0)),
                    pl.BlockSpec((1, idx_heads, tq), lambda bi, tile=tile: (bi, 0, tile))]
        operands.append(out)
        in_specs.append(pl.BlockSpec(memory_space=pl.ANY))
        out = pl.pallas_call(
            functools.partial(_dsa_attn_kernel, tile=tile, topk=topk, heads=heads,
                              kv_heads=kv_heads, dh=dh, idx_heads=idx_heads,
                              index_steps=(ke - 1).bit_length() + 1),
            grid=(b,),
            in_specs=in_specs,
            out_specs=pl.BlockSpec((1, tq, nq), lambda bi, tile=tile: (bi, tile, 0)),
            out_shape=jax.ShapeDtypeStruct((b, s, nq), BF16),
            scratch_shapes=[pltpu.VMEM((ke + SUBLANES, tq), F32),
                            pltpu.VMEM((heads, ke + SUBLANES, tq), F32),
                            pltpu.VMEM((heads, ke + 2 * SUBLANES, tq), BF16),
                            pltpu.VMEM((ke + 2 * SUBLANES, tq), BF16)],
            input_output_aliases={len(operands) - 1: 0},
            compiler_params=_params("parallel"),
            name=f"dsa_attn_{tile}",
        )(*operands)
    return out.reshape(t, nq)


def kernel(x, positions, attn_norm, ret_w_in, ret_out_norm, ret_w_out, dsa_w_in, dsa_q_norm,
           dsa_k_norm, dsa_kidx_norm, dsa_w_out, mlp_norm, mlp_w_up, mlp_w_down):
    b, s, d = x.shape
    depth = attn_norm.shape[0]
    x2d = x.reshape(b * s, d)
    tables = None
    for i in range(depth):
        j = i // 2
        if i % 2 == 0:
            a, made = _retention_mixer(x2d, b, s, attn_norm[i], ret_w_in[j], positions)
            tables = made if tables is None else tables
            w_out = ret_out_norm[j].reshape(-1, 1) * ret_w_out[j]
        else:
            a = _sparse_attention_mixer(x2d, b, s, attn_norm[i], dsa_w_in[j], dsa_q_norm[j],
                                        dsa_k_norm[j], dsa_kidx_norm[j], tables)
            w_out = dsa_w_out[j]
        x2d = _mixer_out_and_mlp(a, x2d, w_out, mlp_norm[i], mlp_w_up, mlp_w_down, i)
    return x2d.reshape(b, s, d)
```

```python
import functools
import math

import jax
import jax.numpy as jnp
from jax import lax
from jax.experimental import pallas as pl
from jax.experimental.pallas import tpu as pltpu

F32 = jnp.float32
BF16 = jnp.bfloat16

EPS = 1e-6
ROPE_THETA = 10000.0
RET_HEADS = 4
RET_CHUNK = 256
ATT_HEADS = 8
ATT_KV_HEADS = 2
IDX_HEADS = 8
IDX_DH = 64
TOPK_MAX = 256
LANES = 128
SUBLANES = 8

ROW_TILE = 512
RET_ROW_BLOCK = 1024
Q_TILE = 256
KEY_BLOCK = 128
SWEEP_UNROLL = 8
VMEM_LIMIT = 56 * 1024 * 1024
BISECT_STEPS = 16


def _params(*sem):
    return pltpu.CompilerParams(dimension_semantics=sem, vmem_limit_bytes=VMEM_LIMIT)


def _resident(shape):
    nd = len(shape)
    return pl.BlockSpec(shape, lambda *_: (0,) * nd, pipeline_mode=pl.Buffered(1))


def _rms(x, gain):
    return x * lax.rsqrt(jnp.mean(x * x, axis=-1, keepdims=True) + EPS) * gain


def _dot(a, b):
    return jnp.dot(a, b, preferred_element_type=F32)


def _dot_nt(a, b):
    return lax.dot_general(a, b, (((1,), (1,)), ((), ())), preferred_element_type=F32)


def _dot_tn(a, b):
    return lax.dot_general(a, b, (((0,), (0,)), ((), ())), preferred_element_type=F32)


def _rope_slab_in_kernel(c, pos_ref, inv_ref, c64t_ref, s64t_ref, c32t_ref, s32t_ref, ct_ref,
                         st_ref):
    lanes = slice(c * LANES, (c + 1) * LANES)
    ang_t = inv_ref[...] * pos_ref[:, lanes].astype(F32)
    cos_t = jnp.cos(ang_t)
    sin_t = jnp.sin(ang_t)
    ct_ref[c] = cos_t
    st_ref[c] = sin_t
    for step, c_ref, s_ref in ((2, c64t_ref, s64t_ref), (4, c32t_ref, s32t_ref)):
        rows = pl.ds(0, LANES // step, stride=step)
        c_ref[:, lanes] = ct_ref[c, rows, :]
        s_ref[:, lanes] = st_ref[c, rows, :]
    return cos_t.T, sin_t.T


def _ret_inproj_kernel(x_ref, gain_ref, w_ref, pos_ref, inv_ref, q_ref, k_ref, v_ref, g_ref,
                       c64t_ref, s64t_ref, c32t_ref, s32t_ref, ct_ref, st_ref,
                       *, heads, dk, dv, kscale):
    hn = _rms(x_ref[...], gain_ref[...]).astype(BF16)
    half = dk // 2
    width = heads * dv
    for off, out_ref, swish in ((2 * heads * dk + width, g_ref, True),
                                (2 * heads * dk, v_ref, False)):
        for c in range(width // dv):
            z = _dot(hn, w_ref[:, off + c * dv:off + (c + 1) * dv].astype(BF16))
            if swish:
                z = z * (0.5 * jnp.tanh(0.5 * z) + 0.5)
            out_ref[:, c * dv:(c + 1) * dv] = z.astype(BF16)
    trig = [_rope_slab_in_kernel(c, pos_ref, inv_ref, c64t_ref, s64t_ref, c32t_ref, s32t_ref,
                                 ct_ref, st_ref) for c in range(ct_ref.shape[0])]
    cos = jnp.concatenate([t[0] for t in trig], axis=0)
    sin = jnp.concatenate([t[1] for t in trig], axis=0)
    for h in range(heads):
        for off, out_ref, scale in ((0, q_ref, None), (heads * dk, k_ref, kscale)):
            z = _dot(hn, w_ref[:, off + h * dk:off + (h + 1) * dk].astype(BF16))
            x1, x2 = z[:, :half], z[:, half:]
            o1 = x1 * cos - x2 * sin
            o2 = x2 * cos + x1 * sin
            if scale is not None:
                o1, o2 = o1 * scale, o2 * scale
            out_ref[:, h * dk:h * dk + half] = o1.astype(BF16)
            out_ref[:, h * dk + half:(h + 1) * dk] = o2.astype(BF16)


def _ret_core_kernel(q_ref, k_ref, v_ref, g_ref, dm_ref, qd_ref, kd_ref, cd_ref,
                     o_ref, state_ref, *, heads, chunk):
    @pl.when(pl.program_id(1) == 0)
    def _():
        state_ref[...] = jnp.zeros_like(state_ref)

    dk = q_ref.shape[2] // heads
    dv = v_ref.shape[2] // heads
    hs = range(heads)
    states = [state_ref[h] for h in hs]
    for c in range(q_ref.shape[1] // chunk):
        rows = slice(c * chunk, (c + 1) * chunk)
        q = [q_ref[0, rows, h * dk:(h + 1) * dk] for h in hs]
        k = [k_ref[0, rows, h * dk:(h + 1) * dk] for h in hs]
        v = [v_ref[0, rows, h * dv:(h + 1) * dv] for h in hs]
        scores = [_dot_nt(q[h], k[h]) for h in hs]
        kv = [_dot_tn((k[h].astype(F32) * kd_ref[h]).astype(BF16), v[h]) for h in hs]
        cross = [_dot(q[h], states[h].astype(BF16)) for h in hs]
        intra = [_dot((scores[h] * dm_ref[h]).astype(BF16), v[h]) for h in hs]
        for h in hs:
            o = intra[h] + cross[h] * qd_ref[h]
            y = o * lax.rsqrt(jnp.mean(o * o, axis=-1, keepdims=True) + EPS)
            gate = g_ref[0, rows, h * dv:(h + 1) * dv].astype(F32)
            o_ref[0, rows, h * dv:(h + 1) * dv] = (y * gate).astype(BF16)
            states[h] = states[h] * cd_ref[h] + kv[h]
    for h in hs:
        state_ref[h] = states[h]


def _retention_mixer(x2d, b, s, attn_gain, w_in, positions):
    t, d = x2d.shape
    heads = RET_HEADS
    dk = d // heads
    dv = 2 * dk
    chunk = RET_CHUNK
    tm = min(ROW_TILE, t)
    w = w_in
    row = lambda width: pl.BlockSpec((tm, width), lambda i: (i, 0))
    col = lambda n: pl.BlockSpec((n, tm), lambda i: (0, i))
    tab = lambda n: jax.ShapeDtypeStruct((n, t), F32)
    inv = ROPE_THETA ** (-jnp.arange(LANES, dtype=F32) / LANES)
    q, k, v, g, *tables = pl.pallas_call(
        functools.partial(_ret_inproj_kernel, heads=heads, dk=dk, dv=dv, kscale=dk ** -0.5),
        grid=(t // tm,),
        in_specs=[row(d), _resident((1, d)), _resident(w.shape),
                  pl.BlockSpec((1, tm), lambda i: (0, i)), _resident((LANES, 1))],
        out_specs=[row(heads * dk), row(heads * dk), row(heads * dv), row(heads * dv),
                   col(64), col(64), col(32), col(32)],
        out_shape=[jax.ShapeDtypeStruct((t, heads * dk), BF16),
                   jax.ShapeDtypeStruct((t, heads * dk), BF16),
                   jax.ShapeDtypeStruct((t, heads * dv), BF16),
                   jax.ShapeDtypeStruct((t, heads * dv), BF16),
                   tab(64), tab(64), tab(32), tab(32)],
        scratch_shapes=[pltpu.VMEM((tm // LANES, LANES, LANES), F32)] * 2,
        compiler_params=_params("parallel"),
        name="ret_inproj",
    )(x2d, attn_gain.reshape(1, d), w, positions.reshape(1, t), inv.reshape(LANES, 1))

    log_gamma = jnp.log1p(-(2.0 ** (-5.0 - jnp.arange(heads, dtype=F32))))
    i = jnp.arange(chunk, dtype=F32)
    diff = i[:, None] - i[None, :]
    dm = jnp.where(diff >= 0, jnp.exp(log_gamma[:, None, None] * jnp.maximum(diff, 0.0)), 0.0)
    qd = jnp.exp(log_gamma[:, None] * (i + 1.0))
    kd = jnp.exp(log_gamma[:, None] * (chunk - 1.0 - i))
    cd = jnp.exp(log_gamma * chunk)
    qd = jnp.broadcast_to(qd[:, :, None], (heads, chunk, dv))
    kd = jnp.broadcast_to(kd[:, :, None], (heads, chunk, dk))
    cd = jnp.broadcast_to(cd[:, None, None], (heads, 1, dv))

    rb = min(RET_ROW_BLOCK, s)
    seq = lambda width: pl.BlockSpec((1, rb, width), lambda bi, r: (bi, r, 0))
    o = pl.pallas_call(
        functools.partial(_ret_core_kernel, heads=heads, chunk=chunk),
        grid=(b, s // rb),
        in_specs=[seq(heads * dk), seq(heads * dk), seq(heads * dv), seq(heads * dv),
                  _resident(dm.shape), _resident(qd.shape), _resident(kd.shape),
                  _resident(cd.shape)],
        out_specs=seq(heads * dv),
        out_shape=jax.ShapeDtypeStruct((b, s, heads * dv), BF16),
        scratch_shapes=[pltpu.VMEM((heads, dk, dv), F32)],
        compiler_params=_params("parallel", "arbitrary"),
        name="ret_core",
    )(q.reshape(b, s, -1), k.reshape(b, s, -1), v.reshape(b, s, -1), g.reshape(b, s, -1),
      dm, qd, kd, cd)
    return o.reshape(t, heads * dv), tuple(tables)


def _tail_kernel(a_ref, x_ref, wo_ref, gain_ref, wup_ref, wdn_ref, o_ref, *, ff_chunk):
    x1 = x_ref[...] + _dot(a_ref[...], wo_ref[...])
    hn = _rms(x1, gain_ref[...]).astype(BF16)
    acc = x1
    for c in range(wup_ref.shape[2] // ff_chunk):
        cols = slice(c * ff_chunk, (c + 1) * ff_chunk)
        u = jnp.maximum(_dot(hn, wup_ref[0, :, cols].astype(BF16)), 0.0)
        acc = acc + _dot((u * u).astype(BF16), wdn_ref[0, cols, :].astype(BF16))
    o_ref[...] = acc


def _mixer_out_and_mlp(a2d, x2d, w_out, mlp_gain, w_up_all, w_down_all, layer):
    t, d = x2d.shape
    tm = min(ROW_TILE, t)
    wo = w_out.astype(BF16)
    row = lambda width: pl.BlockSpec((tm, width), lambda i: (i, 0))
    of_layer = lambda w: pl.BlockSpec((1,) + w.shape[1:], lambda i: (layer, 0, 0),
                                      pipeline_mode=pl.Buffered(1))
    return pl.pallas_call(
        functools.partial(_tail_kernel, ff_chunk=1024),
        grid=(t // tm,),
        in_specs=[row(a2d.shape[1]), row(d), _resident(wo.shape), _resident((1, d)),
                  of_layer(w_up_all), of_layer(w_down_all)],
        out_specs=row(d),
        out_shape=jax.ShapeDtypeStruct((t, d), F32),
        compiler_params=_params("parallel"),
        name="outproj_mlp",
    )(a2d, x2d, wo, mlp_gain.reshape(1, d), w_up_all, w_down_all)


def _norm_rope_rows(z, gain, cos, sin):
    if gain is not None:
        z = z * lax.rsqrt(jnp.mean(z * z, axis=0, keepdims=True) + EPS) * gain
    half = z.shape[0] // 2
    x1, x2 = z[:half], z[half:]
    return x1 * cos - x2 * sin, x2 * cos + x1 * sin


def _dsa_inproj_kernel(x_ref, gain_ref, wt_ref, qg_ref, kg_ref, kig_ref, c64_ref, s64_ref,
                       c32_ref, s32_ref, qt_ref, k_ref, vt_ref, qit_ref, ki_ref, wit_ref, buf_ref,
                       *, heads, kv_heads, dh, idx_heads, di, wscale):
    hn = _rms(x_ref[...], gain_ref[...]).astype(BF16)
    tm = hn.shape[0]
    proj = _dot_nt(wt_ref[...], hn)
    c64, s64 = c64_ref[...], s64_ref[...]
    c32, s32 = c32_ref[...], s32_ref[...]
    qg = jnp.broadcast_to(qg_ref[...], (dh, tm))
    kg = jnp.broadcast_to(kg_ref[...], (dh, tm))
    for h in range(heads):
        o1, o2 = _norm_rope_rows(proj[h * dh:(h + 1) * dh], qg, c64, s64)
        qt_ref[0, h * dh:h * dh + dh // 2, :] = o1.astype(BF16)
        qt_ref[0, h * dh + dh // 2:(h + 1) * dh, :] = o2.astype(BF16)
    off = heads * dh
    for h in range(kv_heads):
        o1, o2 = _norm_rope_rows(proj[off + h * dh:off + (h + 1) * dh], kg, c64, s64)
        k_ref[:, h * dh:(h + 1) * dh] = jnp.concatenate([o1, o2], axis=0).T.astype(BF16)
    off += kv_heads * dh
    vrows = vt_ref.shape[1] // kv_heads
    for g in range(kv_heads):
        vt_ref[0, g * vrows:g * vrows + dh, :] = proj[off + g * dh:off + (g + 1) * dh].astype(BF16)
        vt_ref[0, g * vrows + dh:(g + 1) * vrows, :] = jnp.ones((vrows - dh, tm), BF16)
    off += kv_heads * dh
    for h in range(idx_heads):
        o1, o2 = _norm_rope_rows(proj[off + h * di:off + (h + 1) * di], None, c32, s32)
        qit_ref[0, h * di:h * di + di // 2, :] = o1.astype(BF16)
        qit_ref[0, h * di + di // 2:(h + 1) * di, :] = o2.astype(BF16)
    off += idx_heads * di
    kig = jnp.broadcast_to(kig_ref[...], (di, tm))
    o1, o2 = _norm_rope_rows(proj[off:off + di], kig, c32, s32)
    ki_t = jnp.concatenate([o1, o2, proj[off + di:off + LANES]], axis=0)
    ki_ref[...] = ki_t.T[:, :di].astype(BF16)
    off += LANES
    wit_ref[0] = proj[off:off + idx_heads] * wscale
    buf_ref[...] = jnp.zeros_like(buf_ref)


def _dsa_attn_kernel(*refs, tile, topk, heads, kv_heads, dh, idx_heads, index_steps):
    qt_ref, k_ref, vt_ref, qit_ref, ki_ref, wit_ref = refs[:6]
    o_ref, sc_ref, lg_ref, e_ref, cap_ref = refs[-5:]
    tq = qt_ref.shape[2]
    ke = k_ref.shape[1]
    di = qit_ref.shape[1] // idx_heads
    kb = KEY_BLOCK
    slabs = kb // SUBLANES
    j = tile
    nblk = ke // kb
    neg = float(jnp.finfo(F32).min)
    big = float(jnp.finfo(F32).max)
    diag = (lax.broadcasted_iota(jnp.int32, (tq, tq), 0)
            <= lax.broadcasted_iota(jnp.int32, (tq, tq), 1))

    def rows_of(i):
        return pl.ds(pl.multiple_of(i * kb, kb), kb)

    def full(val):
        return jnp.full((SUBLANES, tq), val, F32)

    def spread(row):
        return jnp.broadcast_to(row, (SUBLANES, tq))

    group = heads // kv_heads

    def raw_logits(head, i):
        g = head // group
        lg_ref[head, rows_of(i), :] = _dot(k_ref[0, rows_of(i), g * dh:(g + 1) * dh],
                                           qt_ref[0, head * dh:(head + 1) * dh, :])

    def sweep(fn, init, head=None, unrolled=False):
        unroll = nblk if unrolled else max(u for u in range(1, SWEEP_UNROLL + 1) if nblk % u == 0)
        acc = lax.fori_loop(0, nblk, lambda i, a: fn(i, sc_ref[rows_of(i), :], a), init,
                            unroll=unroll)
        if head is not None:
            g = head // group
            lg = _dot(k_ref[0, :, g * dh:(g + 1) * dh], qt_ref[0, head * dh:(head + 1) * dh, :])
            lg_ref[head] = lg
            acc = acc + jnp.minimum(jnp.maximum(lg[ke - SUBLANES:, :], 0.0), 0.0)
        return acc

    if ke <= topk:
        sc_ref[...] = jnp.where(diag, 0.0, neg)
        for h in range(heads):
            lax.fori_loop(0, nblk, lambda i, c, h=h: (raw_logits(h, i), c)[1], 0, unroll=True)
    else:
        wit = wit_ref[0]
        ki = ki_ref[0]
        score = jnp.zeros((ke, tq), F32)
        for h in range(idx_heads):
            rel = jnp.maximum(_dot(ki, qit_ref[0, h * di:(h + 1) * di, :]), 0.0)
            score = score + rel * wit[h:h + 1, :]
        last = score[ke - tq:, :]
        row_max = jnp.max(jnp.where(diag, last, neg), axis=0, keepdims=True)
        row_min = jnp.min(jnp.where(diag, last, big), axis=0, keepdims=True)
        if ke > tq:
            row_max = jnp.maximum(row_max, jnp.max(score[:ke - tq, :], axis=0, keepdims=True))
            row_min = jnp.minimum(row_min, jnp.min(score[:ke - tq, :], axis=0, keepdims=True))
            sc_ref[:ke - tq, :] = score[:ke - tq, :]
        sc_ref[ke - tq:, :] = jnp.where(diag, last, neg)
        q_pos = j * tq + lax.broadcasted_iota(jnp.int32, (1, tq), 1)
        kvec = jnp.minimum(q_pos + 1, topk).astype(F32)

        def count_ge(th, head=None, unrolled=False):
            thb = spread(th)[None]
            acc = sweep(lambda i, blk, a: a + jnp.sum(
                jnp.where(blk.reshape(slabs, SUBLANES, tq) >= thb, 1.0, 0.0), axis=0),
                jnp.zeros((SUBLANES, tq), F32), head, unrolled)
            return jnp.sum(acc, axis=0, keepdims=True)

        def max_below(hi):
            hib = spread(hi)[None]

            def step(i, blk, a):
                blk = blk.reshape(slabs, SUBLANES, tq)
                return jnp.maximum(a, jnp.max(jnp.where(blk < hib, blk, neg), axis=0))

            return jnp.max(sweep(step, full(neg)), axis=0, keepdims=True)

        steps = max(BISECT_STEPS, heads)
        carried = {(h * steps) // heads: h for h in range(heads)}
        lo, hi, c_hi = row_min, row_max + jnp.abs(row_max) * 1e-3 + 1e-3, jnp.zeros((1, tq), F32)
        for step in range(steps):
            mid = lo * 0.5 + hi * 0.5
            c = count_ge(mid, carried.get(step), unrolled=True)
            ge = c >= kvec
            lo, hi, c_hi = jnp.where(ge, mid, lo), jnp.where(ge, hi, mid), jnp.where(ge, c_hi, c)

        def count_and_below(v):
            vb = spread(v)[None]

            def step(i, blk, a):
                blk = blk.reshape(slabs, SUBLANES, tq)
                ge = blk >= vb
                return (a[0] + jnp.sum(jnp.where(ge, 1.0, 0.0), axis=0),
                        jnp.maximum(a[1], jnp.max(jnp.where(ge, neg, blk), axis=0)))

            cnt, below = sweep(step, (jnp.zeros((SUBLANES, tq), F32), full(neg)))
            return jnp.sum(cnt, axis=0, keepdims=True), jnp.max(below, axis=0, keepdims=True)

        def pending(cv):
            return jnp.max(jnp.where(cv < kvec, 1.0, 0.0))

        def refine(state):
            c_hi, v, cv, below, _ = state
            short = cv < kvec
            c_hi = jnp.where(short, cv, c_hi)
            v = jnp.where(short, below, v)
            cv, below = count_and_below(v)
            return c_hi, v, cv, below, pending(cv)

        v = max_below(hi)
        cv, below = count_and_below(v)
        c_hi, v, cv, _, _ = lax.while_loop(lambda st: st[4] > 0.0, refine,
                                           (c_hi, v, cv, below, pending(cv)))
        need = kvec - c_hi
        surplus = jnp.max(jnp.where(cv > kvec, 1.0, 0.0)) > 0.0
        vb = spread(v)

        @pl.when(jnp.logical_not(surplus))
        def _():
            def to_bias(i, carry):
                blk = sc_ref[rows_of(i), :].reshape(slabs, SUBLANES, tq)
                sc_ref[rows_of(i), :] = jnp.where(blk >= vb[None], 0.0, neg).reshape(kb, tq)
                return carry
            lax.fori_loop(0, nblk, to_bias, 0)

        @pl.when(surplus)
        def _():
            def key_idx(i):
                return (lax.broadcasted_iota(jnp.int32, (slabs, SUBLANES, tq), 0) * SUBLANES
                        + lax.broadcasted_iota(jnp.int32, (slabs, SUBLANES, tq), 1) + i * kb)

            def idx_step(_, carry):
                lo_i, hi_i = carry
                mid_i = (lo_i + hi_i) >> 1
                midb = jnp.broadcast_to(mid_i, (SUBLANES, tq))
                acc = sweep(lambda i, blk, a: a + jnp.sum(jnp.where(
                    (blk.reshape(slabs, SUBLANES, tq) == vb[None]) & (key_idx(i) <= midb[None]),
                    1.0, 0.0), axis=0), jnp.zeros((SUBLANES, tq), F32))
                ok = jnp.sum(acc, axis=0, keepdims=True) >= need
                return jnp.where(ok, lo_i, mid_i), jnp.where(ok, mid_i, hi_i)

            _, istar = lax.fori_loop(0, index_steps, idx_step,
                                     (jnp.full((1, tq), -1, jnp.int32),
                                      jnp.full((1, tq), nblk * kb - 1, jnp.int32)))
            istarb = jnp.broadcast_to(istar, (SUBLANES, tq))

            def to_bias(i, carry):
                blk = sc_ref[rows_of(i), :].reshape(slabs, SUBLANES, tq)
                keep = (blk > vb[None]) | ((blk == vb[None]) & (key_idx(i) <= istarb[None]))
                sc_ref[rows_of(i), :] = jnp.where(keep, 0.0, neg).reshape(kb, tq)
                return carry
            lax.fori_loop(0, nblk, to_bias, 0)

    cap_val = float(jnp.finfo(BF16).max)

    packed = 2 * SUBLANES

    def max_pass(i, maxima):
        rows = rows_of(i)
        selected = sc_ref[rows, :] == 0.0
        cap_ref[rows, :] = jnp.where(selected, cap_val, 0.0).astype(BF16)
        bias = jnp.where(selected, 0.0, -cap_val).astype(BF16)
        new = []
        for h in range(heads):
            lg = lg_ref[h, rows, :].astype(BF16) + bias
            new.append(jnp.maximum(maxima[h], jnp.max(lg.reshape(kb // packed, packed, tq), axis=0)))
        return tuple(new)

    maxima = lax.fori_loop(0, nblk, max_pass, (jnp.full((packed, tq), -cap_val, BF16),) * heads,
                           unroll=min(nblk, 2))
    maxima = [m16.astype(F32) for m16 in maxima]

    vrows = vt_ref.shape[1] // kv_heads
    for h in range(heads):
        g = h // group
        m = jnp.max(maxima[h], axis=0, keepdims=True)
        for c in range(ke // tq):
            rows = slice(c * tq, (c + 1) * tq)
            e = jnp.exp2(lg_ref[h, rows, :] - m).astype(BF16)
            e_ref[h, rows, :] = jnp.minimum(e, cap_ref[rows, :])
        out_t = _dot(vt_ref[0, g * vrows:(g + 1) * vrows, :], e_ref[h])
        o_ref[0, :, h * dh:(h + 1) * dh] = (out_t[:dh] / out_t[dh:dh + 1]).T.astype(BF16)


def _sparse_attention_mixer(x2d, b, s, attn_gain, w_in, q_gain, k_gain, kidx_gain, tabs):
    t, d = x2d.shape
    heads, kv_heads, idx_heads, di = ATT_HEADS, ATT_KV_HEADS, IDX_HEADS, IDX_DH
    dh = d // heads
    c64, s64, c32, s32 = tabs
    tm = min(ROW_TILE, s)
    per_b = s // tm
    nq, nkv, nqi = heads * dh, kv_heads * dh, idx_heads * di
    nvt = kv_heads * (dh + 2 * SUBLANES)
    n_main = nq + 2 * nkv + nqi + di
    zeros = lambda n: jnp.zeros((d, n), w_in.dtype)
    wt = jnp.concatenate([w_in[:, :n_main], zeros(LANES - di), w_in[:, n_main:n_main + idx_heads],
                          zeros(2 * SUBLANES - idx_heads)], axis=1).T.astype(BF16)
    qg = (q_gain * (dh ** -0.5 * math.log2(math.e))).reshape(dh, 1)
    row = lambda width: pl.BlockSpec((tm, width), lambda i: (i, 0))
    col = lambda n: pl.BlockSpec((n, tm), lambda i: (0, i))
    seq_t = lambda n: pl.BlockSpec((1, n, tm), lambda i: (i // per_b, 0, i % per_b))
    shp = lambda width, dt: jax.ShapeDtypeStruct((t, width), dt)
    shp_t = lambda n, dt: jax.ShapeDtypeStruct((b, n, s), dt)
    qt, k, vt, qit, ki, wit, out = pl.pallas_call(
        functools.partial(_dsa_inproj_kernel, heads=heads, kv_heads=kv_heads, dh=dh,
                          idx_heads=idx_heads, di=di, wscale=idx_heads ** -0.5 * di ** -0.5),
        grid=(t // tm,),
        in_specs=[row(d), _resident((1, d)), _resident(wt.shape), _resident((dh, 1)),
                  _resident((dh, 1)), _resident((di, 1)), col(dh // 2), col(dh // 2),
                  col(di // 2), col(di // 2)],
        out_specs=[seq_t(nq), row(nkv), seq_t(nvt), seq_t(nqi), row(di), seq_t(idx_heads),
                   row(nq)],
        out_shape=[shp_t(nq, BF16), shp(nkv, BF16), shp_t(nvt, BF16), shp_t(nqi, BF16),
                   shp(di, BF16), shp_t(idx_heads, F32), shp(nq, BF16)],
        compiler_params=_params("parallel"),
        name="dsa_inproj",
    )(x2d, attn_gain.reshape(1, d), wt, qg, k_gain.reshape(dh, 1), kidx_gain.reshape(di, 1),
      c64, s64, c32, s32)

    topk = min(TOPK_MAX, s // 4)
    tq = min(Q_TILE, s)
    k, ki = k.reshape(b, s, nkv), ki.reshape(b, s, di)
    out = out.reshape(b, s, nq)
    for tile in range(s // tq):
        ke = (tile + 1) * tq
        operands = [qt, k, vt, qit, ki, wit]
        in_specs = [pl.BlockSpec((1, nq, tq), lambda bi, tile=tile: (bi, 0, tile)),
                    pl.BlockSpec((1, ke, nkv), lambda bi: (bi, 0, 0)),
                    pl.BlockSpec((1, nvt, ke), lambda bi: (bi, 0, 0)),
                    pl.BlockSpec((1, nqi, tq), lambda bi, tile=tile: (bi, 0, tile)),
                    pl.BlockSpec((1, ke, di), lambda bi: (bi, 0, 0)),
                    pl.BlockSpec((1, idx_heads, tq), lambda bi, tile=tile: (bi, 0, tile))]
        operands.append(out)
        in_specs.append(pl.BlockSpec(memory_space=pl.ANY))
        out = pl.pallas_call(
            functools.partial(_dsa_attn_kernel, tile=tile, topk=topk, heads=heads,
                              kv_heads=kv_heads, dh=dh, idx_heads=idx_heads,
                              index_steps=(ke - 1).bit_length() + 1),
            grid=(b,),
            in_specs=in_specs,
            out_specs=pl.BlockSpec((1, tq, nq), lambda bi, tile=tile: (bi, tile, 0)),
            out_shape=jax.ShapeDtypeStruct((b, s, nq), BF16),
            scratch_shapes=[pltpu.VMEM((ke, tq), F32), pltpu.VMEM((heads, ke, tq), F32),
                            pltpu.VMEM((heads, ke, tq), BF16), pltpu.VMEM((ke, tq), BF16)],
            input_output_aliases={len(operands) - 1: 0},
            compiler_params=_params("parallel"),
            name=f"dsa_attn_{tile}",
        )(*operands)
    return out.reshape(t, nq)


def kernel(x, positions, attn_norm, ret_w_in, ret_out_norm, ret_w_out, dsa_w_in, dsa_q_norm,
           dsa_k_norm, dsa_kidx_norm, dsa_w_out, mlp_norm, mlp_w_up, mlp_w_down):
    b, s, d = x.shape
    depth = attn_norm.shape[0]
    x2d = x.reshape(b * s, d)
    tables = None
    for i in range(depth):
        j = i // 2
        if i % 2 == 0:
            a, made = _retention_mixer(x2d, b, s, attn_norm[i], ret_w_in[j], positions)
            tables = made if tables is None else tables
            w_out = ret_out_norm[j].reshape(-1, 1) * ret_w_out[j]
        else:
            a = _sparse_attention_mixer(x2d, b, s, attn_norm[i], dsa_w_in[j], dsa_q_norm[j],
                                        dsa_k_norm[j], dsa_kidx_norm[j], tables)
            w_out = dsa_w_out[j]
        x2d = _mixer_out_and_mlp(a, x2d, w_out, mlp_norm[i], mlp_w_up, mlp_w_down, i)
    return x2d.reshape(b, s, d)
```

```python
import functools
import math

import jax
import jax.numpy as jnp
from jax import lax
from jax.experimental import pallas as pl
from jax.experimental.pallas import tpu as pltpu

F32 = jnp.float32
BF16 = jnp.bfloat16

EPS = 1e-6
ROPE_THETA = 10000.0
RET_HEADS = 4
RET_CHUNK = 256
ATT_HEADS = 8
ATT_KV_HEADS = 2
IDX_HEADS = 8
IDX_DH = 64
TOPK_MAX = 256
LANES = 128
SUBLANES = 8

ROW_TILE = 512
RET_ROW_BLOCK = 1024
Q_TILE = 256
KEY_BLOCK = 128
SWEEP_UNROLL = 8
VMEM_LIMIT = 56 * 1024 * 1024
BISECT_STEPS = 16


def _params(*sem):
    return pltpu.CompilerParams(dimension_semantics=sem, vmem_limit_bytes=VMEM_LIMIT)


def _resident(shape):
    nd = len(shape)
    return pl.BlockSpec(shape, lambda *_: (0,) * nd, pipeline_mode=pl.Buffered(1))


def _rms(x, gain):
    return x * lax.rsqrt(jnp.mean(x * x, axis=-1, keepdims=True) + EPS) * gain


def _dot(a, b):
    return jnp.dot(a, b, preferred_element_type=F32)


def _dot_nt(a, b):
    return lax.dot_general(a, b, (((1,), (1,)), ((), ())), preferred_element_type=F32)


def _dot_tn(a, b):
    return lax.dot_general(a, b, (((0,), (0,)), ((), ())), preferred_element_type=F32)


def _rope_slab_in_kernel(c, pos_ref, inv_ref, c64t_ref, s64t_ref, c32t_ref, s32t_ref, ct_ref,
                         st_ref):
    lanes = slice(c * LANES, (c + 1) * LANES)
    ang_t = inv_ref[...] * pos_ref[:, lanes].astype(F32)
    cos_t = jnp.cos(ang_t)
    sin_t = jnp.sin(ang_t)
    ct_ref[c] = cos_t
    st_ref[c] = sin_t
    for step, c_ref, s_ref in ((2, c64t_ref, s64t_ref), (4, c32t_ref, s32t_ref)):
        rows = pl.ds(0, LANES // step, stride=step)
        c_ref[:, lanes] = ct_ref[c, rows, :]
        s_ref[:, lanes] = st_ref[c, rows, :]
    return cos_t.T, sin_t.T


def _ret_inproj_kernel(x_ref, gain_ref, w_ref, pos_ref, inv_ref, q_ref, k_ref, v_ref, g_ref,
                       c64t_ref, s64t_ref, c32t_ref, s32t_ref, ct_ref, st_ref,
                       *, heads, dk, dv, kscale):
    hn = _rms(x_ref[...], gain_ref[...]).astype(BF16)
    half = dk // 2
    width = heads * dv
    for off, out_ref, swish in ((2 * heads * dk + width, g_ref, True),
                                (2 * heads * dk, v_ref, False)):
        for c in range(width // dv):
            z = _dot(hn, w_ref[:, off + c * dv:off + (c + 1) * dv].astype(BF16))
            if swish:
                z = z * (0.5 * jnp.tanh(0.5 * z) + 0.5)
            out_ref[:, c * dv:(c + 1) * dv] = z.astype(BF16)
    trig = [_rope_slab_in_kernel(c, pos_ref, inv_ref, c64t_ref, s64t_ref, c32t_ref, s32t_ref,
                                 ct_ref, st_ref) for c in range(ct_ref.shape[0])]
    cos = jnp.concatenate([t[0] for t in trig], axis=0)
    sin = jnp.concatenate([t[1] for t in trig], axis=0)
    for h in range(heads):
        for off, out_ref, scale in ((0, q_ref, None), (heads * dk, k_ref, kscale)):
            z = _dot(hn, w_ref[:, off + h * dk:off + (h + 1) * dk].astype(BF16))
            x1, x2 = z[:, :half], z[:, half:]
            o1 = x1 * cos - x2 * sin
            o2 = x2 * cos + x1 * sin
            if scale is not None:
                o1, o2 = o1 * scale, o2 * scale
            out_ref[:, h * dk:h * dk + half] = o1.astype(BF16)
            out_ref[:, h * dk + half:(h + 1) * dk] = o2.astype(BF16)


def _ret_core_kernel(q_ref, k_ref, v_ref, g_ref, dm_ref, qd_ref, kd_ref, cd_ref,
                     o_ref, state_ref, *, heads, chunk):
    @pl.when(pl.program_id(1) == 0)
    def _():
        state_ref[...] = jnp.zeros_like(state_ref)

    dk = q_ref.shape[2] // heads
    dv = v_ref.shape[2] // heads
    hs = range(heads)
    states = [state_ref[h] for h in hs]
    for c in range(q_ref.shape[1] // chunk):
        rows = slice(c * chunk, (c + 1) * chunk)
        q = [q_ref[0, rows, h * dk:(h + 1) * dk] for h in hs]
        k = [k_ref[0, rows, h * dk:(h + 1) * dk] for h in hs]
        v = [v_ref[0, rows, h * dv:(h + 1) * dv] for h in hs]
        scores = [_dot_nt(q[h], k[h]) for h in hs]
        kv = [_dot_tn((k[h].astype(F32) * kd_ref[h]).astype(BF16), v[h]) for h in hs]
        cross = [_dot(q[h], states[h].astype(BF16)) for h in hs]
        intra = [_dot((scores[h] * dm_ref[h]).astype(BF16), v[h]) for h in hs]
        for h in hs:
            o = intra[h] + cross[h] * qd_ref[h]
            y = o * lax.rsqrt(jnp.mean(o * o, axis=-1, keepdims=True) + EPS)
            gate = g_ref[0, rows, h * dv:(h + 1) * dv].astype(F32)
            o_ref[0, rows, h * dv:(h + 1) * dv] = (y * gate).astype(BF16)
            states[h] = states[h] * cd_ref[h] + kv[h]
    for h in hs:
        state_ref[h] = states[h]


def _retention_mixer(x2d, b, s, attn_gain, w_in, positions):
    t, d = x2d.shape
    heads = RET_HEADS
    dk = d // heads
    dv = 2 * dk
    chunk = RET_CHUNK
    tm = min(ROW_TILE, t)
    w = w_in
    row = lambda width: pl.BlockSpec((tm, width), lambda i: (i, 0))
    col = lambda n: pl.BlockSpec((n, tm), lambda i: (0, i))
    tab = lambda n: jax.ShapeDtypeStruct((n, t), F32)
    inv = ROPE_THETA ** (-jnp.arange(LANES, dtype=F32) / LANES)
    q, k, v, g, *tables = pl.pallas_call(
        functools.partial(_ret_inproj_kernel, heads=heads, dk=dk, dv=dv, kscale=dk ** -0.5),
        grid=(t // tm,),
        in_specs=[row(d), _resident((1, d)), _resident(w.shape),
                  pl.BlockSpec((1, tm), lambda i: (0, i)), _resident((LANES, 1))],
        out_specs=[row(heads * dk), row(heads * dk), row(heads * dv), row(heads * dv),
                   col(64), col(64), col(32), col(32)],
        out_shape=[jax.ShapeDtypeStruct((t, heads * dk), BF16),
                   jax.ShapeDtypeStruct((t, heads * dk), BF16),
                   jax.ShapeDtypeStruct((t, heads * dv), BF16),
                   jax.ShapeDtypeStruct((t, heads * dv), BF16),
                   tab(64), tab(64), tab(32), tab(32)],
        scratch_shapes=[pltpu.VMEM((tm // LANES, LANES, LANES), F32)] * 2,
        compiler_params=_params("parallel"),
        name="ret_inproj",
    )(x2d, attn_gain.reshape(1, d), w, positions.reshape(1, t), inv.reshape(LANES, 1))

    log_gamma = jnp.log1p(-(2.0 ** (-5.0 - jnp.arange(heads, dtype=F32))))
    i = jnp.arange(chunk, dtype=F32)
    diff = i[:, None] - i[None, :]
    dm = jnp.where(diff >= 0, jnp.exp(log_gamma[:, None, None] * jnp.maximum(diff, 0.0)), 0.0)
    qd = jnp.exp(log_gamma[:, None] * (i + 1.0))
    kd = jnp.exp(log_gamma[:, None] * (chunk - 1.0 - i))
    cd = jnp.exp(log_gamma * chunk)
    qd = jnp.broadcast_to(qd[:, :, None], (heads, chunk, dv))
    kd = jnp.broadcast_to(kd[:, :, None], (heads, chunk, dk))
    cd = jnp.broadcast_to(cd[:, None, None], (heads, 1, dv))

    rb = min(RET_ROW_BLOCK, s)
    seq = lambda width: pl.BlockSpec((1, rb, width), lambda bi, r: (bi, r, 0))
    o = pl.pallas_call(
        functools.partial(_ret_core_kernel, heads=heads, chunk=chunk),
        grid=(b, s // rb),
        in_specs=[seq(heads * dk), seq(heads * dk), seq(heads * dv), seq(heads * dv),
                  _resident(dm.shape), _resident(qd.shape), _resident(kd.shape),
                  _resident(cd.shape)],
        out_specs=seq(heads * dv),
        out_shape=jax.ShapeDtypeStruct((b, s, heads * dv), BF16),
        scratch_shapes=[pltpu.VMEM((heads, dk, dv), F32)],
        compiler_params=_params("parallel", "arbitrary"),
        name="ret_core",
    )(q.reshape(b, s, -1), k.reshape(b, s, -1), v.reshape(b, s, -1), g.reshape(b, s, -1),
      dm, qd, kd, cd)
    return o.reshape(t, heads * dv), tuple(tables)


def _tail_kernel(a_ref, x_ref, wo_ref, gain_ref, wup_ref, wdn_ref, o_ref, *, ff_chunk):
    x1 = x_ref[...] + _dot(a_ref[...], wo_ref[...])
    hn = _rms(x1, gain_ref[...]).astype(BF16)
    acc = x1
    for c in range(wup_ref.shape[2] // ff_chunk):
        cols = slice(c * ff_chunk, (c + 1) * ff_chunk)
        u = jnp.maximum(_dot(hn, wup_ref[0, :, cols].astype(BF16)), 0.0)
        acc = acc + _dot((u * u).astype(BF16), wdn_ref[0, cols, :].astype(BF16))
    o_ref[...] = acc


def _mixer_out_and_mlp(a2d, x2d, w_out, mlp_gain, w_up_all, w_down_all, layer):
    t, d = x2d.shape
    tm = min(ROW_TILE, t)
    wo = w_out.astype(BF16)
    row = lambda width: pl.BlockSpec((tm, width), lambda i: (i, 0))
    of_layer = lambda w: pl.BlockSpec((1,) + w.shape[1:], lambda i: (layer, 0, 0),
                                      pipeline_mode=pl.Buffered(1))
    return pl.pallas_call(
        functools.partial(_tail_kernel, ff_chunk=2048),
        grid=(t // tm,),
        in_specs=[row(a2d.shape[1]), row(d), _resident(wo.shape), _resident((1, d)),
                  of_layer(w_up_all), of_layer(w_down_all)],
        out_specs=row(d),
        out_shape=jax.ShapeDtypeStruct((t, d), F32),
        compiler_params=_params("parallel"),
        name="outproj_mlp",
    )(a2d, x2d, wo, mlp_gain.reshape(1, d), w_up_all, w_down_all)


def _norm_rope_rows(z, gain, cos, sin):
    if gain is not None:
        z = z * lax.rsqrt(jnp.mean(z * z, axis=0, keepdims=True) + EPS) * gain
    half = z.shape[0] // 2
    x1, x2 = z[:half], z[half:]
    return x1 * cos - x2 * sin, x2 * cos + x1 * sin


def _dsa_inproj_kernel(x_ref, gain_ref, wt_ref, qg_ref, kg_ref, kig_ref, c64_ref, s64_ref,
                       c32_ref, s32_ref, qt_ref, k_ref, vt_ref, qit_ref, ki_ref, wit_ref, buf_ref,
                       *, heads, kv_heads, dh, idx_heads, di, wscale):
    hn = _rms(x_ref[...], gain_ref[...]).astype(BF16)
    tm = hn.shape[0]
    proj = _dot_nt(wt_ref[...], hn)
    c64, s64 = c64_ref[...], s64_ref[...]
    c32, s32 = c32_ref[...], s32_ref[...]
    qg = jnp.broadcast_to(qg_ref[...], (dh, tm))
    kg = jnp.broadcast_to(kg_ref[...], (dh, tm))
    for h in range(heads):
        o1, o2 = _norm_rope_rows(proj[h * dh:(h + 1) * dh], qg, c64, s64)
        qt_ref[0, h * dh:h * dh + dh // 2, :] = o1.astype(BF16)
        qt_ref[0, h * dh + dh // 2:(h + 1) * dh, :] = o2.astype(BF16)
    off = heads * dh
    for h in range(kv_heads):
        o1, o2 = _norm_rope_rows(proj[off + h * dh:off + (h + 1) * dh], kg, c64, s64)
        k_ref[:, h * dh:(h + 1) * dh] = jnp.concatenate([o1, o2], axis=0).T.astype(BF16)
    off += kv_heads * dh
    vrows = vt_ref.shape[1] // kv_heads
    for g in range(kv_heads):
        vt_ref[0, g * vrows:g * vrows + dh, :] = proj[off + g * dh:off + (g + 1) * dh].astype(BF16)
        vt_ref[0, g * vrows + dh:(g + 1) * vrows, :] = jnp.ones((vrows - dh, tm), BF16)
    off += kv_heads * dh
    for h in range(idx_heads):
        o1, o2 = _norm_rope_rows(proj[off + h * di:off + (h + 1) * di], None, c32, s32)
        qit_ref[0, h * di:h * di + di // 2, :] = o1.astype(BF16)
        qit_ref[0, h * di + di // 2:(h + 1) * di, :] = o2.astype(BF16)
    off += idx_heads * di
    kig = jnp.broadcast_to(kig_ref[...], (di, tm))
    o1, o2 = _norm_rope_rows(proj[off:off + di], kig, c32, s32)
    ki_t = jnp.concatenate([o1, o2, proj[off + di:off + LANES]], axis=0)
    ki_ref[...] = ki_t.T[:, :di].astype(BF16)
    off += LANES
    wit_ref[0] = proj[off:off + idx_heads] * wscale
    buf_ref[...] = jnp.zeros_like(buf_ref)


def _dsa_attn_kernel(*refs, tile, topk, heads, kv_heads, dh, idx_heads, index_steps):
    qt_ref, k_ref, vt_ref, qit_ref, ki_ref, wit_ref = refs[:6]
    o_ref, sc_ref, lg_ref, e_ref, cap_ref = refs[-5:]
    tq = qt_ref.shape[2]
    ke = k_ref.shape[1]
    di = qit_ref.shape[1] // idx_heads
    kb = KEY_BLOCK
    slabs = kb // SUBLANES
    j = tile
    nblk = ke // kb
    neg = float(jnp.finfo(F32).min)
    big = float(jnp.finfo(F32).max)
    diag = (lax.broadcasted_iota(jnp.int32, (tq, tq), 0)
            <= lax.broadcasted_iota(jnp.int32, (tq, tq), 1))

    def rows_of(i):
        return pl.ds(pl.multiple_of(i * kb, kb), kb)

    def full(val):
        return jnp.full((SUBLANES, tq), val, F32)

    def spread(row):
        return jnp.broadcast_to(row, (SUBLANES, tq))

    group = heads // kv_heads

    def raw_logits(head, i):
        g = head // group
        lg_ref[head, rows_of(i), :] = _dot(k_ref[0, rows_of(i), g * dh:(g + 1) * dh],
                                           qt_ref[0, head * dh:(head + 1) * dh, :])

    def sweep(fn, init, head=None, unrolled=False):
        unroll = nblk if unrolled else max(u for u in range(1, SWEEP_UNROLL + 1) if nblk % u == 0)
        acc = lax.fori_loop(0, nblk, lambda i, a: fn(i, sc_ref[rows_of(i), :], a), init,
                            unroll=unroll)
        if head is not None:
            g = head // group
            lg = _dot(k_ref[0, :, g * dh:(g + 1) * dh], qt_ref[0, head * dh:(head + 1) * dh, :])
            lg_ref[head] = lg
            acc = acc + jnp.minimum(jnp.maximum(lg[ke - SUBLANES:, :], 0.0), 0.0)
        return acc

    if ke <= topk:
        sc_ref[...] = jnp.where(diag, 0.0, neg)
        for h in range(heads):
            lax.fori_loop(0, nblk, lambda i, c, h=h: (raw_logits(h, i), c)[1], 0, unroll=True)
    else:
        wit = wit_ref[0]
        ki = ki_ref[0]
        score = jnp.zeros((ke, tq), F32)
        for h in range(idx_heads):
            rel = jnp.maximum(_dot(ki, qit_ref[0, h * di:(h + 1) * di, :]), 0.0)
            score = score + rel * wit[h:h + 1, :]
        last = score[ke - tq:, :]
        row_max = jnp.max(jnp.where(diag, last, neg), axis=0, keepdims=True)
        row_min = jnp.min(jnp.where(diag, last, big), axis=0, keepdims=True)
        if ke > tq:
            row_max = jnp.maximum(row_max, jnp.max(score[:ke - tq, :], axis=0, keepdims=True))
            row_min = jnp.minimum(row_min, jnp.min(score[:ke - tq, :], axis=0, keepdims=True))
            sc_ref[:ke - tq, :] = score[:ke - tq, :]
        sc_ref[ke - tq:, :] = jnp.where(diag, last, neg)
        q_pos = j * tq + lax.broadcasted_iota(jnp.int32, (1, tq), 1)
        kvec = jnp.minimum(q_pos + 1, topk).astype(F32)

        def count_ge(th, head=None, unrolled=False):
            thb = spread(th)[None]
            acc = sweep(lambda i, blk, a: a + jnp.sum(
                jnp.where(blk.reshape(slabs, SUBLANES, tq) >= thb, 1.0, 0.0), axis=0),
                jnp.zeros((SUBLANES, tq), F32), head, unrolled)
            return jnp.sum(acc, axis=0, keepdims=True)

        def max_below(hi):
            hib = spread(hi)[None]

            def step(i, blk, a):
                blk = blk.reshape(slabs, SUBLANES, tq)
                return jnp.maximum(a, jnp.max(jnp.where(blk < hib, blk, neg), axis=0))

            return jnp.max(sweep(step, full(neg)), axis=0, keepdims=True)

        steps = max(BISECT_STEPS, heads)
        carried = {(h * steps) // heads: h for h in range(heads)}
        lo, hi, c_hi = row_min, row_max + jnp.abs(row_max) * 1e-3 + 1e-3, jnp.zeros((1, tq), F32)
        for step in range(steps):
            mid = lo * 0.5 + hi * 0.5
            c = count_ge(mid, carried.get(step), unrolled=True)
            ge = c >= kvec
            lo, hi, c_hi = jnp.where(ge, mid, lo), jnp.where(ge, hi, mid), jnp.where(ge, c_hi, c)

        def count_and_below(v):
            vb = spread(v)[None]

            def step(i, blk, a):
                blk = blk.reshape(slabs, SUBLANES, tq)
                ge = blk >= vb
                return (a[0] + jnp.sum(jnp.where(ge, 1.0, 0.0), axis=0),
                        jnp.maximum(a[1], jnp.max(jnp.where(ge, neg, blk), axis=0)))

            cnt, below = sweep(step, (jnp.zeros((SUBLANES, tq), F32), full(neg)))
            return jnp.sum(cnt, axis=0, keepdims=True), jnp.max(below, axis=0, keepdims=True)

        def pending(cv):
            return jnp.max(jnp.where(cv < kvec, 1.0, 0.0))

        def refine(state):
            c_hi, v, cv, below, _ = state
            short = cv < kvec
            c_hi = jnp.where(short, cv, c_hi)
            v = jnp.where(short, below, v)
            cv, below = count_and_below(v)
            return c_hi, v, cv, below, pending(cv)

        v = max_below(hi)
        cv, below = count_and_below(v)
        c_hi, v, cv, _, _ = lax.while_loop(lambda st: st[4] > 0.0, refine,
                                           (c_hi, v, cv, below, pending(cv)))
        need = kvec - c_hi
        surplus = jnp.max(jnp.where(cv > kvec, 1.0, 0.0)) > 0.0
        vb = spread(v)

        @pl.when(jnp.logical_not(surplus))
        def _():
            def to_bias(i, carry):
                blk = sc_ref[rows_of(i), :].reshape(slabs, SUBLANES, tq)
                sc_ref[rows_of(i), :] = jnp.where(blk >= vb[None], 0.0, neg).reshape(kb, tq)
                return carry
            lax.fori_loop(0, nblk, to_bias, 0)

        @pl.when(surplus)
        def _():
            def key_idx(i):
                return (lax.broadcasted_iota(jnp.int32, (slabs, SUBLANES, tq), 0) * SUBLANES
                        + lax.broadcasted_iota(jnp.int32, (slabs, SUBLANES, tq), 1) + i * kb)

            def idx_step(_, carry):
                lo_i, hi_i = carry
                mid_i = (lo_i + hi_i) >> 1
                midb = jnp.broadcast_to(mid_i, (SUBLANES, tq))
                acc = sweep(lambda i, blk, a: a + jnp.sum(jnp.where(
                    (blk.reshape(slabs, SUBLANES, tq) == vb[None]) & (key_idx(i) <= midb[None]),
                    1.0, 0.0), axis=0), jnp.zeros((SUBLANES, tq), F32))
                ok = jnp.sum(acc, axis=0, keepdims=True) >= need
                return jnp.where(ok, lo_i, mid_i), jnp.where(ok, mid_i, hi_i)

            _, istar = lax.fori_loop(0, index_steps, idx_step,
                                     (jnp.full((1, tq), -1, jnp.int32),
                                      jnp.full((1, tq), nblk * kb - 1, jnp.int32)))
            istarb = jnp.broadcast_to(istar, (SUBLANES, tq))

            def to_bias(i, carry):
                blk = sc_ref[rows_of(i), :].reshape(slabs, SUBLANES, tq)
                keep = (blk > vb[None]) | ((blk == vb[None]) & (key_idx(i) <= istarb[None]))
                sc_ref[rows_of(i), :] = jnp.where(keep, 0.0, neg).reshape(kb, tq)
                return carry
            lax.fori_loop(0, nblk, to_bias, 0)

    cap_val = float(jnp.finfo(BF16).max)

    packed = 2 * SUBLANES

    def max_pass(i, maxima):
        rows = rows_of(i)
        selected = sc_ref[rows, :] == 0.0
        cap_ref[rows, :] = jnp.where(selected, cap_val, 0.0).astype(BF16)
        bias = jnp.where(selected, 0.0, -cap_val).astype(BF16)
        new = []
        for h in range(heads):
            lg = lg_ref[h, rows, :].astype(BF16) + bias
            new.append(jnp.maximum(maxima[h], jnp.max(lg.reshape(kb // packed, packed, tq), axis=0)))
        return tuple(new)

    maxima = lax.fori_loop(0, nblk, max_pass, (jnp.full((packed, tq), -cap_val, BF16),) * heads,
                           unroll=min(nblk, 2))
    maxima = [m16.astype(F32) for m16 in maxima]

    vrows = vt_ref.shape[1] // kv_heads
    for h in range(heads):
        g = h // group
        m = jnp.max(maxima[h], axis=0, keepdims=True)
        for c in range(ke // tq):
            rows = slice(c * tq, (c + 1) * tq)
            e = jnp.exp2(lg_ref[h, rows, :] - m).astype(BF16)
            e_ref[h, rows, :] = jnp.minimum(e, cap_ref[rows, :])
        out_t = _dot(vt_ref[0, g * vrows:(g + 1) * vrows, :], e_ref[h])
        o_ref[0, :, h * dh:(h + 1) * dh] = (out_t[:dh] / out_t[dh:dh + 1]).T.astype(BF16)


def _sparse_attention_mixer(x2d, b, s, attn_gain, w_in, q_gain, k_gain, kidx_gain, tabs):
    t, d = x2d.shape
    heads, kv_heads, idx_heads, di = ATT_HEADS, ATT_KV_HEADS, IDX_HEADS, IDX_DH
    dh = d // heads
    c64, s64, c32, s32 = tabs
    tm = min(2 * ROW_TILE, s)
    per_b = s // tm
    nq, nkv, nqi = heads * dh, kv_heads * dh, idx_heads * di
    nvt = kv_heads * (dh + 2 * SUBLANES)
    n_main = nq + 2 * nkv + nqi + di
    zeros = lambda n: jnp.zeros((d, n), w_in.dtype)
    wt = jnp.concatenate([w_in[:, :n_main], zeros(LANES - di), w_in[:, n_main:n_main + idx_heads],
                          zeros(2 * SUBLANES - idx_heads)], axis=1).T.astype(BF16)
    qg = (q_gain * (dh ** -0.5 * math.log2(math.e))).reshape(dh, 1)
    row = lambda width: pl.BlockSpec((tm, width), lambda i: (i, 0))
    col = lambda n: pl.BlockSpec((n, tm), lambda i: (0, i))
    seq_t = lambda n: pl.BlockSpec((1, n, tm), lambda i: (i // per_b, 0, i % per_b))
    shp = lambda width, dt: jax.ShapeDtypeStruct((t, width), dt)
    shp_t = lambda n, dt: jax.ShapeDtypeStruct((b, n, s), dt)
    qt, k, vt, qit, ki, wit, out = pl.pallas_call(
        functools.partial(_dsa_inproj_kernel, heads=heads, kv_heads=kv_heads, dh=dh,
                          idx_heads=idx_heads, di=di, wscale=idx_heads ** -0.5 * di ** -0.5),
        grid=(t // tm,),
        in_specs=[row(d), _resident((1, d)), _resident(wt.shape), _resident((dh, 1)),
                  _resident((dh, 1)), _resident((di, 1)), col(dh // 2), col(dh // 2),
                  col(di // 2), col(di // 2)],
        out_specs=[seq_t(nq), row(nkv), seq_t(nvt), seq_t(nqi), row(di), seq_t(idx_heads),
                   row(nq)],
        out_shape=[shp_t(nq, BF16), shp(nkv, BF16), shp_t(nvt, BF16), shp_t(nqi, BF16),
                   shp(di, BF16), shp_t(idx_heads, F32), shp(nq, BF16)],
        compiler_params=_params("parallel"),
        name="dsa_inproj",
    )(x2d, attn_gain.reshape(1, d), wt, qg, k_gain.reshape(dh, 1), kidx_gain.reshape(di, 1),
      c64, s64, c32, s32)

    topk = min(TOPK_MAX, s // 4)
    tq = min(Q_TILE, s)
    k, ki = k.reshape(b, s, nkv), ki.reshape(b, s, di)
    out = out.reshape(b, s, nq)
    for tile in range(s // tq):
        ke = (tile + 1) * tq
        operands = [qt, k, vt, qit, ki, wit]
        in_specs = [pl.BlockSpec((1, nq, tq), lambda bi, tile=tile: (bi, 0, tile)),
                    pl.BlockSpec((1, ke, nkv), lambda bi: (bi, 0, 0)),
                    pl.BlockSpec((1, nvt, ke), lambda bi: (bi, 0, 0)),
                    pl.BlockSpec((1, nqi, tq), lambda bi, tile=tile: (bi, 0, tile)),
                    pl.BlockSpec((1, ke, di), lambda bi: (bi, 0, 0)),
                    pl.BlockSpec((1, idx_heads, tq), lambda bi, tile=tile: (bi, 0, tile))]
        operands.append(out)
        in_specs.append(pl.BlockSpec(memory_space=pl.ANY))
        out = pl.pallas_call(
            functools.partial(_dsa_attn_kernel, tile=tile, topk=topk, heads=heads,
                              kv_heads=kv_heads, dh=dh, idx_heads=idx_heads,
                              index_steps=(ke - 1).bit_length() + 1),
            grid=(b,),
            in_specs=in_specs,
            out_specs=pl.BlockSpec((1, tq, nq), lambda bi, tile=tile: (bi, tile, 0)),
            out_shape=jax.ShapeDtypeStruct((b, s, nq), BF16),
            scratch_shapes=[pltpu.VMEM((ke, tq), F32), pltpu.VMEM((heads, ke, tq), F32),
                            pltpu.VMEM((heads, ke, tq), BF16), pltpu.VMEM((ke, tq), BF16)],
            input_output_aliases={len(operands) - 1: 0},
            compiler_params=_params("parallel"),
            name=f"dsa_attn_{tile}",
        )(*operands)
    return out.reshape(t, nq)


def kernel(x, positions, attn_norm, ret_w_in, ret_out_norm, ret_w_out, dsa_w_in, dsa_q_norm,
           dsa_k_norm, dsa_kidx_norm, dsa_w_out, mlp_norm, mlp_w_up, mlp_w_down):
    b, s, d = x.shape
    depth = attn_norm.shape[0]
    x2d = x.reshape(b * s, d)
    tables = None
    for i in range(depth):
        j = i // 2
        if i % 2 == 0:
            a, made = _retention_mixer(x2d, b, s, attn_norm[i], ret_w_in[j], positions)
            tables = made if tables is None else tables
            w_out = ret_out_norm[j].reshape(-1, 1) * ret_w_out[j]
        else:
            a = _sparse_attention_mixer(x2d, b, s, attn_norm[i], dsa_w_in[j], dsa_q_norm[j],
                                        dsa_k_norm[j], dsa_kidx_norm[j], tables)
            w_out = dsa_w_out[j]
        x2d = _mixer_out_and_mlp(a, x2d, w_out, mlp_norm[i], mlp_w_up, mlp_w_down, i)
    return x2d.reshape(b, s, d)
```

```python
import functools
import math

import jax
import jax.numpy as jnp
from jax import lax
from jax.experimental import pallas as pl
from jax.experimental.pallas import tpu as pltpu

F32 = jnp.float32
BF16 = jnp.bfloat16

EPS = 1e-6
ROPE_THETA = 10000.0
RET_HEADS = 4
RET_CHUNK = 256
ATT_HEADS = 8
ATT_KV_HEADS = 2
IDX_HEADS = 8
IDX_DH = 64
TOPK_MAX = 256
LANES = 128
SUBLANES = 8

ROW_TILE = 512
DSA_ROW_TILE = 1024
RET_ROW_BLOCK = 1024
Q_TILE = 256
KEY_BLOCK = 128
SWEEP_UNROLL = 8
VMEM_LIMIT = 56 * 1024 * 1024
BISECT_STEPS = 18


def _params(*sem):
    return pltpu.CompilerParams(dimension_semantics=sem, vmem_limit_bytes=VMEM_LIMIT)


def _resident(shape):
    nd = len(shape)
    return pl.BlockSpec(shape, lambda *_: (0,) * nd, pipeline_mode=pl.Buffered(1))


def _rms(x, gain):
    return x * lax.rsqrt(jnp.mean(x * x, axis=-1, keepdims=True) + EPS) * gain


def _dot(a, b):
    return jnp.dot(a, b, preferred_element_type=F32)


def _dot_nt(a, b):
    return lax.dot_general(a, b, (((1,), (1,)), ((), ())), preferred_element_type=F32)


def _dot_tn(a, b):
    return lax.dot_general(a, b, (((0,), (0,)), ((), ())), preferred_element_type=F32)


def _rope_slab_in_kernel(c, pos_ref, inv_ref, c64t_ref, s64t_ref, c32t_ref, s32t_ref, ct_ref,
                         st_ref):
    lanes = slice(c * LANES, (c + 1) * LANES)
    ang_t = inv_ref[...] * pos_ref[:, lanes].astype(F32)
    cos_t = jnp.cos(ang_t)
    sin_t = jnp.sin(ang_t)
    ct_ref[c] = cos_t
    st_ref[c] = sin_t
    for step, c_ref, s_ref in ((2, c64t_ref, s64t_ref), (4, c32t_ref, s32t_ref)):
        rows = pl.ds(0, LANES // step, stride=step)
        c_ref[:, lanes] = ct_ref[c, rows, :]
        s_ref[:, lanes] = st_ref[c, rows, :]
    return cos_t.T, sin_t.T


def _ret_inproj_kernel(x_ref, gain_ref, w_ref, pos_ref, inv_ref, q_ref, k_ref, v_ref, g_ref,
                       c64t_ref, s64t_ref, c32t_ref, s32t_ref, ct_ref, st_ref,
                       *, heads, dk, dv, kscale):
    hn = _rms(x_ref[...], gain_ref[...]).astype(BF16)
    half = dk // 2
    width = heads * dv
    for off, out_ref, swish in ((2 * heads * dk + width, g_ref, True),
                                (2 * heads * dk, v_ref, False)):
        for c in range(width // dv):
            z = _dot(hn, w_ref[:, off + c * dv:off + (c + 1) * dv].astype(BF16))
            if swish:
                z = z * (0.5 * jnp.tanh(0.5 * z) + 0.5)
            out_ref[:, c * dv:(c + 1) * dv] = z.astype(BF16)
    trig = [_rope_slab_in_kernel(c, pos_ref, inv_ref, c64t_ref, s64t_ref, c32t_ref, s32t_ref,
                                 ct_ref, st_ref) for c in range(ct_ref.shape[0])]
    cos = jnp.concatenate([t[0] for t in trig], axis=0)
    sin = jnp.concatenate([t[1] for t in trig], axis=0)
    for h in range(heads):
        for off, out_ref, scale in ((0, q_ref, None), (heads * dk, k_ref, kscale)):
            z = _dot(hn, w_ref[:, off + h * dk:off + (h + 1) * dk].astype(BF16))
            x1, x2 = z[:, :half], z[:, half:]
            o1 = x1 * cos - x2 * sin
            o2 = x2 * cos + x1 * sin
            if scale is not None:
                o1, o2 = o1 * scale, o2 * scale
            out_ref[:, h * dk:h * dk + half] = o1.astype(BF16)
            out_ref[:, h * dk + half:(h + 1) * dk] = o2.astype(BF16)


def _ret_core_kernel(q_ref, k_ref, v_ref, g_ref, dm_ref, qd_ref, kd_ref, cd_ref,
                     o_ref, state_ref, *, heads, chunk):
    @pl.when(pl.program_id(1) == 0)
    def _():
        state_ref[...] = jnp.zeros_like(state_ref)

    dk = q_ref.shape[2] // heads
    dv = v_ref.shape[2] // heads
    hs = range(heads)
    states = [state_ref[h] for h in hs]
    for c in range(q_ref.shape[1] // chunk):
        rows = slice(c * chunk, (c + 1) * chunk)
        q = [q_ref[0, rows, h * dk:(h + 1) * dk] for h in hs]
        k = [k_ref[0, rows, h * dk:(h + 1) * dk] for h in hs]
        v = [v_ref[0, rows, h * dv:(h + 1) * dv] for h in hs]
        scores = [_dot_nt(q[h], k[h]) for h in hs]
        kv = [_dot_tn((k[h].astype(F32) * kd_ref[h]).astype(BF16), v[h]) for h in hs]
        cross = [_dot(q[h], states[h].astype(BF16)) for h in hs]
        intra = [_dot((scores[h] * dm_ref[h]).astype(BF16), v[h]) for h in hs]
        for h in hs:
            o = intra[h] + cross[h] * qd_ref[h]
            y = o * lax.rsqrt(jnp.mean(o * o, axis=-1, keepdims=True) + EPS)
            gate = g_ref[0, rows, h * dv:(h + 1) * dv].astype(F32)
            o_ref[0, rows, h * dv:(h + 1) * dv] = (y * gate).astype(BF16)
            states[h] = states[h] * cd_ref[h] + kv[h]
    for h in hs:
        state_ref[h] = states[h]


def _retention_mixer(x2d, b, s, attn_gain, w_in, positions):
    t, d = x2d.shape
    heads = RET_HEADS
    dk = d // heads
    dv = 2 * dk
    chunk = RET_CHUNK
    tm = min(ROW_TILE, t)
    w = w_in
    row = lambda width: pl.BlockSpec((tm, width), lambda i: (i, 0))
    col = lambda n: pl.BlockSpec((n, tm), lambda i: (0, i))
    tab = lambda n: jax.ShapeDtypeStruct((n, t), F32)
    inv = ROPE_THETA ** (-jnp.arange(LANES, dtype=F32) / LANES)
    q, k, v, g, *tables = pl.pallas_call(
        functools.partial(_ret_inproj_kernel, heads=heads, dk=dk, dv=dv, kscale=dk ** -0.5),
        grid=(t // tm,),
        in_specs=[row(d), _resident((1, d)), _resident(w.shape),
                  pl.BlockSpec((1, tm), lambda i: (0, i)), _resident((LANES, 1))],
        out_specs=[row(heads * dk), row(heads * dk), row(heads * dv), row(heads * dv),
                   col(64), col(64), col(32), col(32)],
        out_shape=[jax.ShapeDtypeStruct((t, heads * dk), BF16),
                   jax.ShapeDtypeStruct((t, heads * dk), BF16),
                   jax.ShapeDtypeStruct((t, heads * dv), BF16),
                   jax.ShapeDtypeStruct((t, heads * dv), BF16),
                   tab(64), tab(64), tab(32), tab(32)],
        scratch_shapes=[pltpu.VMEM((tm // LANES, LANES, LANES), F32)] * 2,
        compiler_params=_params("parallel"),
        name="ret_inproj",
    )(x2d, attn_gain.reshape(1, d), w, positions.reshape(1, t), inv.reshape(LANES, 1))

    log_gamma = jnp.log1p(-(2.0 ** (-5.0 - jnp.arange(heads, dtype=F32))))
    i = jnp.arange(chunk, dtype=F32)
    diff = i[:, None] - i[None, :]
    dm = jnp.where(diff >= 0, jnp.exp(log_gamma[:, None, None] * jnp.maximum(diff, 0.0)), 0.0)
    qd = jnp.exp(log_gamma[:, None] * (i + 1.0))
    kd = jnp.exp(log_gamma[:, None] * (chunk - 1.0 - i))
    cd = jnp.exp(log_gamma * chunk)
    qd = jnp.broadcast_to(qd[:, :, None], (heads, chunk, dv))
    kd = jnp.broadcast_to(kd[:, :, None], (heads, chunk, dk))
    cd = jnp.broadcast_to(cd[:, None, None], (heads, 1, dv))

    rb = min(RET_ROW_BLOCK, s)
    seq = lambda width: pl.BlockSpec((1, rb, width), lambda bi, r: (bi, r, 0))
    o = pl.pallas_call(
        functools.partial(_ret_core_kernel, heads=heads, chunk=chunk),
        grid=(b, s // rb),
        in_specs=[seq(heads * dk), seq(heads * dk), seq(heads * dv), seq(heads * dv),
                  _resident(dm.shape), _resident(qd.shape), _resident(kd.shape),
                  _resident(cd.shape)],
        out_specs=seq(heads * dv),
        out_shape=jax.ShapeDtypeStruct((b, s, heads * dv), BF16),
        scratch_shapes=[pltpu.VMEM((heads, dk, dv), F32)],
        compiler_params=_params("parallel", "arbitrary"),
        name="ret_core",
    )(q.reshape(b, s, -1), k.reshape(b, s, -1), v.reshape(b, s, -1), g.reshape(b, s, -1),
      dm, qd, kd, cd)
    return o.reshape(t, heads * dv), tuple(tables)


def _tail_kernel(a_ref, x_ref, wo_ref, gain_ref, wup_ref, wdn_ref, o_ref, *, ff_chunk):
    x1 = x_ref[...] + _dot(a_ref[...], wo_ref[...])
    hn = _rms(x1, gain_ref[...]).astype(BF16)
    acc = x1
    for c in range(wup_ref.shape[2] // ff_chunk):
        cols = slice(c * ff_chunk, (c + 1) * ff_chunk)
        u = jnp.maximum(_dot(hn, wup_ref[0, :, cols].astype(BF16)), 0.0)
        acc = acc + _dot((u * u).astype(BF16), wdn_ref[0, cols, :].astype(BF16))
    o_ref[...] = acc


def _mixer_out_and_mlp(a2d, x2d, w_out, mlp_gain, w_up_all, w_down_all, layer):
    t, d = x2d.shape
    tm = min(ROW_TILE, t)
    wo = w_out.astype(BF16)
    row = lambda width: pl.BlockSpec((tm, width), lambda i: (i, 0))
    of_layer = lambda w: pl.BlockSpec((1,) + w.shape[1:], lambda i: (layer, 0, 0),
                                      pipeline_mode=pl.Buffered(1))
    return pl.pallas_call(
        functools.partial(_tail_kernel, ff_chunk=2048),
        grid=(t // tm,),
        in_specs=[row(a2d.shape[1]), row(d), _resident(wo.shape), _resident((1, d)),
                  of_layer(w_up_all), of_layer(w_down_all)],
        out_specs=row(d),
        out_shape=jax.ShapeDtypeStruct((t, d), F32),
        compiler_params=_params("parallel"),
        name="outproj_mlp",
    )(a2d, x2d, wo, mlp_gain.reshape(1, d), w_up_all, w_down_all)


def _norm_rope_rows(z, gain, cos, sin):
    if gain is not None:
        z = z * lax.rsqrt(jnp.mean(z * z, axis=0, keepdims=True) + EPS) * gain
    half = z.shape[0] // 2
    x1, x2 = z[:half], z[half:]
    return x1 * cos - x2 * sin, x2 * cos + x1 * sin


def _dsa_inproj_kernel(x_ref, gain_ref, wt_ref, qg_ref, kg_ref, kig_ref, c64_ref, s64_ref,
                       c32_ref, s32_ref, qt_ref, k_ref, vt_ref, qit_ref, ki_ref, wit_ref, buf_ref,
                       *, heads, kv_heads, dh, idx_heads, di, wscale):
    hn = _rms(x_ref[...], gain_ref[...]).astype(BF16)
    tm = hn.shape[0]
    proj = _dot_nt(wt_ref[...], hn)
    c64, s64 = c64_ref[...], s64_ref[...]
    c32, s32 = c32_ref[...], s32_ref[...]
    qg = jnp.broadcast_to(qg_ref[...], (dh, tm))
    kg = jnp.broadcast_to(kg_ref[...], (dh, tm))
    for h in range(heads):
        o1, o2 = _norm_rope_rows(proj[h * dh:(h + 1) * dh], qg, c64, s64)
        qt_ref[0, h * dh:h * dh + dh // 2, :] = o1.astype(BF16)
        qt_ref[0, h * dh + dh // 2:(h + 1) * dh, :] = o2.astype(BF16)
    off = heads * dh
    for h in range(kv_heads):
        o1, o2 = _norm_rope_rows(proj[off + h * dh:off + (h + 1) * dh], kg, c64, s64)
        k_ref[:, h * dh:(h + 1) * dh] = jnp.concatenate([o1, o2], axis=0).T.astype(BF16)
    off += kv_heads * dh
    vrows = vt_ref.shape[1] // kv_heads
    for g in range(kv_heads):
        vt_ref[0, g * vrows:g * vrows + dh, :] = proj[off + g * dh:off + (g + 1) * dh].astype(BF16)
        vt_ref[0, g * vrows + dh:(g + 1) * vrows, :] = jnp.ones((vrows - dh, tm), BF16)
    off += kv_heads * dh
    for h in range(idx_heads):
        o1, o2 = _norm_rope_rows(proj[off + h * di:off + (h + 1) * di], None, c32, s32)
        qit_ref[0, h * di:h * di + di // 2, :] = o1.astype(BF16)
        qit_ref[0, h * di + di // 2:(h + 1) * di, :] = o2.astype(BF16)
    off += idx_heads * di
    kig = jnp.broadcast_to(kig_ref[...], (di, tm))
    o1, o2 = _norm_rope_rows(proj[off:off + di], kig, c32, s32)
    ki_t = jnp.concatenate([o1, o2, proj[off + di:off + LANES]], axis=0)
    ki_ref[...] = ki_t.T[:, :di].astype(BF16)
    off += LANES
    wit_ref[0] = proj[off:off + idx_heads] * wscale
    buf_ref[...] = jnp.zeros_like(buf_ref)


def _dsa_attn_kernel(*refs, tile, topk, heads, kv_heads, dh, idx_heads, index_steps):
    qt_ref, k_ref, vt_ref, qit_ref, ki_ref, wit_ref = refs[:6]
    o_ref, sc_ref, lg_ref, e_ref, cap_ref = refs[-5:]
    tq = qt_ref.shape[2]
    ke = k_ref.shape[1]
    di = qit_ref.shape[1] // idx_heads
    kb = KEY_BLOCK
    slabs = kb // SUBLANES
    j = tile
    nblk = ke // kb
    neg = float(jnp.finfo(F32).min)
    big = float(jnp.finfo(F32).max)
    diag = (lax.broadcasted_iota(jnp.int32, (tq, tq), 0)
            <= lax.broadcasted_iota(jnp.int32, (tq, tq), 1))

    def rows_of(i):
        return pl.ds(pl.multiple_of(i * kb, kb), kb)

    def full(val):
        return jnp.full((SUBLANES, tq), val, F32)

    def spread(row):
        return jnp.broadcast_to(row, (SUBLANES, tq))

    group = heads // kv_heads

    def raw_logits(head, i):
        g = head // group
        lg_ref[head, rows_of(i), :] = _dot(k_ref[0, rows_of(i), g * dh:(g + 1) * dh],
                                           qt_ref[0, head * dh:(head + 1) * dh, :])

    def sweep(fn, init, head=None, unrolled=False):
        unroll = nblk if unrolled else max(u for u in range(1, SWEEP_UNROLL + 1) if nblk % u == 0)
        acc = lax.fori_loop(0, nblk, lambda i, a: fn(i, sc_ref[rows_of(i), :], a), init,
                            unroll=unroll)
        if head is not None:
            g = head // group
            lg = _dot(k_ref[0, :, g * dh:(g + 1) * dh], qt_ref[0, head * dh:(head + 1) * dh, :])
            lg_ref[head] = lg
            acc = acc + jnp.minimum(jnp.maximum(lg[ke - SUBLANES:, :], 0.0), 0.0)
        return acc

    if ke <= topk:
        sc_ref[...] = jnp.where(diag, 0.0, neg)
        for h in range(heads):
            lax.fori_loop(0, nblk, lambda i, c, h=h: (raw_logits(h, i), c)[1], 0, unroll=True)
    else:
        wit = wit_ref[0]
        ki = ki_ref[0]
        score = jnp.zeros((ke, tq), F32)
        for h in range(idx_heads):
            rel = jnp.maximum(_dot(ki, qit_ref[0, h * di:(h + 1) * di, :]), 0.0)
            score = score + rel * wit[h:h + 1, :]
        last = score[ke - tq:, :]
        row_max = jnp.max(jnp.where(diag, last, neg), axis=0, keepdims=True)
        row_min = jnp.min(jnp.where(diag, last, big), axis=0, keepdims=True)
        if ke > tq:
            row_max = jnp.maximum(row_max, jnp.max(score[:ke - tq, :], axis=0, keepdims=True))
            row_min = jnp.minimum(row_min, jnp.min(score[:ke - tq, :], axis=0, keepdims=True))
            sc_ref[:ke - tq, :] = score[:ke - tq, :]
        sc_ref[ke - tq:, :] = jnp.where(diag, last, neg)
        q_pos = j * tq + lax.broadcasted_iota(jnp.int32, (1, tq), 1)
        kvec = jnp.minimum(q_pos + 1, topk).astype(F32)

        def count_ge(th, head=None, unrolled=False):
            thb = spread(th)[None]
            acc = sweep(lambda i, blk, a: a + jnp.sum(
                jnp.where(blk.reshape(slabs, SUBLANES, tq) >= thb, 1.0, 0.0), axis=0),
                jnp.zeros((SUBLANES, tq), F32), head, unrolled)
            return jnp.sum(acc, axis=0, keepdims=True)

        def max_below(hi):
            hib = spread(hi)[None]

            def step(i, blk, a):
                blk = blk.reshape(slabs, SUBLANES, tq)
                return jnp.maximum(a, jnp.max(jnp.where(blk < hib, blk, neg), axis=0))

            return jnp.max(sweep(step, full(neg)), axis=0, keepdims=True)

        steps = max(BISECT_STEPS, heads)
        carried = {(h * steps) // heads: h for h in range(heads)}
        lo, hi, c_hi = row_min, row_max + jnp.abs(row_max) * 1e-3 + 1e-3, jnp.zeros((1, tq), F32)
        for step in range(steps):
            mid = lo * 0.5 + hi * 0.5
            c = count_ge(mid, carried.get(step), unrolled=True)
            ge = c >= kvec
            lo, hi, c_hi = jnp.where(ge, mid, lo), jnp.where(ge, hi, mid), jnp.where(ge, c_hi, c)

        def count_and_below(v):
            vb = spread(v)[None]

            def step(i, blk, a):
                blk = blk.reshape(slabs, SUBLANES, tq)
                ge = blk >= vb
                return (a[0] + jnp.sum(jnp.where(ge, 1.0, 0.0), axis=0),
                        jnp.maximum(a[1], jnp.max(jnp.where(ge, neg, blk), axis=0)))

            cnt, below = sweep(step, (jnp.zeros((SUBLANES, tq), F32), full(neg)))
            return jnp.sum(cnt, axis=0, keepdims=True), jnp.max(below, axis=0, keepdims=True)

        def pending(cv):
            return jnp.max(jnp.where(cv < kvec, 1.0, 0.0))

        def refine(state):
            c_hi, v, cv, below, _ = state
            short = cv < kvec
            c_hi = jnp.where(short, cv, c_hi)
            v = jnp.where(short, below, v)
            cv, below = count_and_below(v)
            return c_hi, v, cv, below, pending(cv)

        v = max_below(hi)
        cv, below = count_and_below(v)
        c_hi, v, cv, _, _ = lax.while_loop(lambda st: st[4] > 0.0, refine,
                                           (c_hi, v, cv, below, pending(cv)))
        need = kvec - c_hi
        surplus = jnp.max(jnp.where(cv > kvec, 1.0, 0.0)) > 0.0
        vb = spread(v)

        @pl.when(jnp.logical_not(surplus))
        def _():
            def to_bias(i, carry):
                blk = sc_ref[rows_of(i), :].reshape(slabs, SUBLANES, tq)
                sc_ref[rows_of(i), :] = jnp.where(blk >= vb[None], 0.0, neg).reshape(kb, tq)
                return carry
            lax.fori_loop(0, nblk, to_bias, 0)

        @pl.when(surplus)
        def _():
            def key_idx(i):
                return (lax.broadcasted_iota(jnp.int32, (slabs, SUBLANES, tq), 0) * SUBLANES
                        + lax.broadcasted_iota(jnp.int32, (slabs, SUBLANES, tq), 1) + i * kb)

            def idx_step(_, carry):
                lo_i, hi_i = carry
                mid_i = (lo_i + hi_i) >> 1
                midb = jnp.broadcast_to(mid_i, (SUBLANES, tq))
                acc = sweep(lambda i, blk, a: a + jnp.sum(jnp.where(
                    (blk.reshape(slabs, SUBLANES, tq) == vb[None]) & (key_idx(i) <= midb[None]),
                    1.0, 0.0), axis=0), jnp.zeros((SUBLANES, tq), F32))
                ok = jnp.sum(acc, axis=0, keepdims=True) >= need
                return jnp.where(ok, lo_i, mid_i), jnp.where(ok, mid_i, hi_i)

            _, istar = lax.fori_loop(0, index_steps, idx_step,
                                     (jnp.full((1, tq), -1, jnp.int32),
                                      jnp.full((1, tq), nblk * kb - 1, jnp.int32)))
            istarb = jnp.broadcast_to(istar, (SUBLANES, tq))

            def to_bias(i, carry):
                blk = sc_ref[rows_of(i), :].reshape(slabs, SUBLANES, tq)
                keep = (blk > vb[None]) | ((blk == vb[None]) & (key_idx(i) <= istarb[None]))
                sc_ref[rows_of(i), :] = jnp.where(keep, 0.0, neg).reshape(kb, tq)
                return carry
            lax.fori_loop(0, nblk, to_bias, 0)

    cap_val = float(jnp.finfo(BF16).max)

    packed = 2 * SUBLANES

    def max_pass(i, maxima):
        rows = rows_of(i)
        selected = sc_ref[rows, :] == 0.0
        cap_ref[rows, :] = jnp.where(selected, cap_val, 0.0).astype(BF16)
        bias = jnp.where(selected, 0.0, -cap_val).astype(BF16)
        new = []
        for h in range(heads):
            lg = lg_ref[h, rows, :].astype(BF16) + bias
            new.append(jnp.maximum(maxima[h], jnp.max(lg.reshape(kb // packed, packed, tq), axis=0)))
        return tuple(new)

    maxima = lax.fori_loop(0, nblk, max_pass, (jnp.full((packed, tq), -cap_val, BF16),) * heads,
                           unroll=min(nblk, 2))
    maxima = [m16.astype(F32) for m16 in maxima]

    vrows = vt_ref.shape[1] // kv_heads
    for h in range(heads):
        g = h // group
        m = jnp.max(maxima[h], axis=0, keepdims=True)
        for c in range(ke // tq):
            rows = slice(c * tq, (c + 1) * tq)
            e = jnp.exp2(lg_ref[h, rows, :] - m).astype(BF16)
            e_ref[h, rows, :] = jnp.minimum(e, cap_ref[rows, :])
        out_t = _dot(vt_ref[0, g * vrows:(g + 1) * vrows, :], e_ref[h])
        o_ref[0, :, h * dh:(h + 1) * dh] = (out_t[:dh] / out_t[dh:dh + 1]).T.astype(BF16)


def _sparse_attention_mixer(x2d, b, s, attn_gain, w_in, q_gain, k_gain, kidx_gain, tabs):
    t, d = x2d.shape
    heads, kv_heads, idx_heads, di = ATT_HEADS, ATT_KV_HEADS, IDX_HEADS, IDX_DH
    dh = d // heads
    c64, s64, c32, s32 = tabs
    tm = min(DSA_ROW_TILE, s)
    per_b = s // tm
    nq, nkv, nqi = heads * dh, kv_heads * dh, idx_heads * di
    nvt = kv_heads * (dh + 2 * SUBLANES)
    n_main = nq + 2 * nkv + nqi + di
    zeros = lambda n: jnp.zeros((d, n), w_in.dtype)
    wt = jnp.concatenate([w_in[:, :n_main], zeros(LANES - di), w_in[:, n_main:n_main + idx_heads],
                          zeros(2 * SUBLANES - idx_heads)], axis=1).T.astype(BF16)
    qg = (q_gain * (dh ** -0.5 * math.log2(math.e))).reshape(dh, 1)
    row = lambda width: pl.BlockSpec((tm, width), lambda i: (i, 0))
    col = lambda n: pl.BlockSpec((n, tm), lambda i: (0, i))
    seq_t = lambda n: pl.BlockSpec((1, n, tm), lambda i: (i // per_b, 0, i % per_b))
    shp = lambda width, dt: jax.ShapeDtypeStruct((t, width), dt)
    shp_t = lambda n, dt: jax.ShapeDtypeStruct((b, n, s), dt)
    qt, k, vt, qit, ki, wit, out = pl.pallas_call(
        functools.partial(_dsa_inproj_kernel, heads=heads, kv_heads=kv_heads, dh=dh,
                          idx_heads=idx_heads, di=di, wscale=idx_heads ** -0.5 * di ** -0.5),
        grid=(t // tm,),
        in_specs=[row(d), _resident((1, d)), _resident(wt.shape), _resident((dh, 1)),
                  _resident((dh, 1)), _resident((di, 1)), col(dh // 2), col(dh // 2),
                  col(di // 2), col(di // 2)],
        out_specs=[seq_t(nq), row(nkv), seq_t(nvt), seq_t(nqi), row(di), seq_t(idx_heads),
                   row(nq)],
        out_shape=[shp_t(nq, BF16), shp(nkv, BF16), shp_t(nvt, BF16), shp_t(nqi, BF16),
                   shp(di, BF16), shp_t(idx_heads, F32), shp(nq, BF16)],
        compiler_params=_params("parallel"),
        name="dsa_inproj",
    )(x2d, attn_gain.reshape(1, d), wt, qg, k_gain.reshape(dh, 1), kidx_gain.reshape(di, 1),
      c64, s64, c32, s32)

    topk = min(TOPK_MAX, s // 4)
    tq = min(Q_TILE, s)
    k, ki = k.reshape(b, s, nkv), ki.reshape(b, s, di)
    out = out.reshape(b, s, nq)
    for tile in range(s // tq):
        ke = (tile + 1) * tq
        operands = [qt, k, vt, qit, ki, wit]
        in_specs = [pl.BlockSpec((1, nq, tq), lambda bi, tile=tile: (bi, 0, tile)),
                    pl.BlockSpec((1, ke, nkv), lambda bi: (bi, 0, 0)),
                    pl.BlockSpec((1, nvt, ke), lambda bi: (bi, 0, 0)),
                    pl.BlockSpec((1, nqi, tq), lambda bi, tile=tile: (bi, 0, tile)),
                    pl.BlockSpec((1, ke, di), lambda bi: (bi, 0, 0)),
                    pl.BlockSpec((1, idx_heads, tq), lambda bi, tile=tile: (bi, 0, tile))]
        operands.append(out)
        in_specs.append(pl.BlockSpec(memory_space=pl.ANY))
        out = pl.pallas_call(
            functools.partial(_dsa_attn_kernel, tile=tile, topk=topk, heads=heads,
                              kv_heads=kv_heads, dh=dh, idx_heads=idx_heads,
                              index_steps=(ke - 1).bit_length() + 1),
            grid=(b,),
            in_specs=in_specs,
            out_specs=pl.BlockSpec((1, tq, nq), lambda bi, tile=tile: (bi, tile, 0)),
            out_shape=jax.ShapeDtypeStruct((b, s, nq), BF16),
            scratch_shapes=[pltpu.VMEM((ke, tq), F32), pltpu.VMEM((heads, ke, tq), F32),
                            pltpu.VMEM((heads, ke, tq), BF16), pltpu.VMEM((ke, tq), BF16)],
            input_output_aliases={len(operands) - 1: 0},
            compiler_params=_params("parallel"),
            name=f"dsa_attn_{tile}",
        )(*operands)
    return out.reshape(t, nq)


def kernel(x, positions, attn_norm, ret_w_in, ret_out_norm, ret_w_out, dsa_w_in, dsa_q_norm,
           dsa_k_norm, dsa_kidx_norm, dsa_w_out, mlp_norm, mlp_w_up, mlp_w_down):
    b, s, d = x.shape
    depth = attn_norm.shape[0]
    x2d = x.reshape(b * s, d)
    tables = None
    for i in range(depth):
        j = i // 2
        if i % 2 == 0:
            a, made = _retention_mixer(x2d, b, s, attn_norm[i], ret_w_in[j], positions)
            tables = made if tables is None else tables
            w_out = ret_out_norm[j].reshape(-1, 1) * ret_w_out[j]
        else:
            a = _sparse_attention_mixer(x2d, b, s, attn_norm[i], dsa_w_in[j], dsa_q_norm[j],
                                        dsa_k_norm[j], dsa_kidx_norm[j], tables)
            w_out = dsa_w_out[j]
        x2d = _mixer_out_and_mlp(a, x2d, w_out, mlp_norm[i], mlp_w_up, mlp_w_down, i)
    return x2d.reshape(b, s, d)
```

```python
import functools
import math

import jax
import jax.numpy as jnp
from jax import lax
from jax.experimental import pallas as pl
from jax.experimental.pallas import tpu as pltpu

F32 = jnp.float32
BF16 = jnp.bfloat16

EPS = 1e-6
ROPE_THETA = 10000.0
RET_HEADS = 4
RET_CHUNK = 256
ATT_HEADS = 8
ATT_KV_HEADS = 2
IDX_HEADS = 8
IDX_DH = 64
TOPK_MAX = 256
LANES = 128
SUBLANES = 8

ROW_TILE = 512
DSA_ROW_TILE = 1024
RET_ROW_BLOCK = 1024
Q_TILE = 256
KEY_BLOCK = 128
SWEEP_UNROLL = 8
VMEM_LIMIT = 56 * 1024 * 1024
BISECT_STEPS = 16


def _params(*sem):
    return pltpu.CompilerParams(dimension_semantics=sem, vmem_limit_bytes=VMEM_LIMIT)


def _resident(shape):
    nd = len(shape)
    return pl.BlockSpec(shape, lambda *_: (0,) * nd, pipeline_mode=pl.Buffered(1))


def _rms(x, gain):
    return x * lax.rsqrt(jnp.mean(x * x, axis=-1, keepdims=True) + EPS) * gain


def _dot(a, b):
    return jnp.dot(a, b, preferred_element_type=F32)


def _dot_nt(a, b):
    return lax.dot_general(a, b, (((1,), (1,)), ((), ())), preferred_element_type=F32)


def _dot_tn(a, b):
    return lax.dot_general(a, b, (((0,), (0,)), ((), ())), preferred_element_type=F32)


def _rope_slab_in_kernel(c, pos_ref, inv_ref, c64t_ref, s64t_ref, c32t_ref, s32t_ref, ct_ref,
                         st_ref):
    lanes = slice(c * LANES, (c + 1) * LANES)
    ang_t = inv_ref[...] * pos_ref[:, lanes].astype(F32)
    cos_t = jnp.cos(ang_t)
    sin_t = jnp.sin(ang_t)
    ct_ref[c] = cos_t
    st_ref[c] = sin_t
    for step, c_ref, s_ref in ((2, c64t_ref, s64t_ref), (4, c32t_ref, s32t_ref)):
        rows = pl.ds(0, LANES // step, stride=step)
        c_ref[:, lanes] = ct_ref[c, rows, :]
        s_ref[:, lanes] = st_ref[c, rows, :]
    return cos_t.T, sin_t.T


def _ret_inproj_kernel(x_ref, gain_ref, w_ref, pos_ref, inv_ref, q_ref, k_ref, v_ref, g_ref,
                       c64t_ref, s64t_ref, c32t_ref, s32t_ref, ct_ref, st_ref,
                       *, heads, dk, dv, kscale):
    hn = _rms(x_ref[...], gain_ref[...]).astype(BF16)
    half = dk // 2
    width = heads * dv
    for off, out_ref, swish in ((2 * heads * dk + width, g_ref, True),
                                (2 * heads * dk, v_ref, False)):
        for c in range(width // dv):
            z = _dot(hn, w_ref[:, off + c * dv:off + (c + 1) * dv].astype(BF16))
            if swish:
                z = z * (0.5 * jnp.tanh(0.5 * z) + 0.5)
            out_ref[:, c * dv:(c + 1) * dv] = z.astype(BF16)
    trig = [_rope_slab_in_kernel(c, pos_ref, inv_ref, c64t_ref, s64t_ref, c32t_ref, s32t_ref,
                                 ct_ref, st_ref) for c in range(ct_ref.shape[0])]
    cos = jnp.concatenate([t[0] for t in trig], axis=0)
    sin = jnp.concatenate([t[1] for t in trig], axis=0)
    for h in range(heads):
        for off, out_ref, scale in ((0, q_ref, None), (heads * dk, k_ref, kscale)):
            z = _dot(hn, w_ref[:, off + h * dk:off + (h + 1) * dk].astype(BF16))
            x1, x2 = z[:, :half], z[:, half:]
            o1 = x1 * cos - x2 * sin
            o2 = x2 * cos + x1 * sin
            if scale is not None:
                o1, o2 = o1 * scale, o2 * scale
            out_ref[:, h * dk:h * dk + half] = o1.astype(BF16)
            out_ref[:, h * dk + half:(h + 1) * dk] = o2.astype(BF16)


def _ret_core_kernel(q_ref, k_ref, v_ref, g_ref, dm_ref, qd_ref, kd_ref, cd_ref,
                     o_ref, state_ref, *, heads, chunk):
    @pl.when(pl.program_id(1) == 0)
    def _():
        state_ref[...] = jnp.zeros_like(state_ref)

    dk = q_ref.shape[2] // heads
    dv = v_ref.shape[2] // heads
    hs = range(heads)
    states = [state_ref[h] for h in hs]
    for c in range(q_ref.shape[1] // chunk):
        rows = slice(c * chunk, (c + 1) * chunk)
        q = [q_ref[0, rows, h * dk:(h + 1) * dk] for h in hs]
        k = [k_ref[0, rows, h * dk:(h + 1) * dk] for h in hs]
        v = [v_ref[0, rows, h * dv:(h + 1) * dv] for h in hs]
        scores = [_dot_nt(q[h], k[h]) for h in hs]
        kv = [_dot_tn((k[h].astype(F32) * kd_ref[h]).astype(BF16), v[h]) for h in hs]
        cross = [_dot(q[h], states[h].astype(BF16)) for h in hs]
        intra = [_dot((scores[h] * dm_ref[h]).astype(BF16), v[h]) for h in hs]
        for h in hs:
            o = intra[h] + cross[h] * qd_ref[h]
            y = o * lax.rsqrt(jnp.mean(o * o, axis=-1, keepdims=True) + EPS)
            gate = g_ref[0, rows, h * dv:(h + 1) * dv].astype(F32)
            o_ref[0, rows, h * dv:(h + 1) * dv] = (y * gate).astype(BF16)
            states[h] = states[h] * cd_ref[h] + kv[h]
    for h in hs:
        state_ref[h] = states[h]


def _retention_mixer(x2d, b, s, attn_gain, w_in, positions):
    t, d = x2d.shape
    heads = RET_HEADS
    dk = d // heads
    dv = 2 * dk
    chunk = RET_CHUNK
    tm = min(ROW_TILE, t)
    w = w_in
    row = lambda width: pl.BlockSpec((tm, width), lambda i: (i, 0))
    col = lambda n: pl.BlockSpec((n, tm), lambda i: (0, i))
    tab = lambda n: jax.ShapeDtypeStruct((n, t), F32)
    inv = ROPE_THETA ** (-jnp.arange(LANES, dtype=F32) / LANES)
    q, k, v, g, *tables = pl.pallas_call(
        functools.partial(_ret_inproj_kernel, heads=heads, dk=dk, dv=dv, kscale=dk ** -0.5),
        grid=(t // tm,),
        in_specs=[row(d), _resident((1, d)), _resident(w.shape),
                  pl.BlockSpec((1, tm), lambda i: (0, i)), _resident((LANES, 1))],
        out_specs=[row(heads * dk), row(heads * dk), row(heads * dv), row(heads * dv),
                   col(64), col(64), col(32), col(32)],
        out_shape=[jax.ShapeDtypeStruct((t, heads * dk), BF16),
                   jax.ShapeDtypeStruct((t, heads * dk), BF16),
                   jax.ShapeDtypeStruct((t, heads * dv), BF16),
                   jax.ShapeDtypeStruct((t, heads * dv), BF16),
                   tab(64), tab(64), tab(32), tab(32)],
        scratch_shapes=[pltpu.VMEM((tm // LANES, LANES, LANES), F32)] * 2,
        compiler_params=_params("parallel"),
        name="ret_inproj",
    )(x2d, attn_gain.reshape(1, d), w, positions.reshape(1, t), inv.reshape(LANES, 1))

    log_gamma = jnp.log1p(-(2.0 ** (-5.0 - jnp.arange(heads, dtype=F32))))
    i = jnp.arange(chunk, dtype=F32)
    diff = i[:, None] - i[None, :]
    dm = jnp.where(diff >= 0, jnp.exp(log_gamma[:, None, None] * jnp.maximum(diff, 0.0)), 0.0)
    qd = jnp.exp(log_gamma[:, None] * (i + 1.0))
    kd = jnp.exp(log_gamma[:, None] * (chunk - 1.0 - i))
    cd = jnp.exp(log_gamma * chunk)
    qd = jnp.broadcast_to(qd[:, :, None], (heads, chunk, dv))
    kd = jnp.broadcast_to(kd[:, :, None], (heads, chunk, dk))
    cd = jnp.broadcast_to(cd[:, None, None], (heads, 1, dv))

    rb = min(RET_ROW_BLOCK, s)
    seq = lambda width: pl.BlockSpec((1, rb, width), lambda bi, r: (bi, r, 0))
    o = pl.pallas_call(
        functools.partial(_ret_core_kernel, heads=heads, chunk=chunk),
        grid=(b, s // rb),
        in_specs=[seq(heads * dk), seq(heads * dk), seq(heads * dv), seq(heads * dv),
                  _resident(dm.shape), _resident(qd.shape), _resident(kd.shape),
                  _resident(cd.shape)],
        out_specs=seq(heads * dv),
        out_shape=jax.ShapeDtypeStruct((b, s, heads * dv), BF16),
        scratch_shapes=[pltpu.VMEM((heads, dk, dv), F32)],
        compiler_params=_params("parallel", "arbitrary"),
        name="ret_core",
    )(q.reshape(b, s, -1), k.reshape(b, s, -1), v.reshape(b, s, -1), g.reshape(b, s, -1),
      dm, qd, kd, cd)
    return o.reshape(t, heads * dv), tuple(tables)


def _tail_kernel(a_ref, x_ref, wo_ref, gain_ref, wup_ref, wdn_ref, o_ref, *, ff_chunk):
    x1 = x_ref[...] + _dot(a_ref[...], wo_ref[...])
    hn = _rms(x1, gain_ref[...]).astype(BF16)
    acc = x1
    for c in range(wup_ref.shape[2] // ff_chunk):
        cols = slice(c * ff_chunk, (c + 1) * ff_chunk)
        u = jnp.maximum(_dot(hn, wup_ref[0, :, cols].astype(BF16)), 0.0)
        acc = acc + _dot((u * u).astype(BF16), wdn_ref[0, cols, :].astype(BF16))
    o_ref[...] = acc


def _mixer_out_and_mlp(a2d, x2d, w_out, mlp_gain, w_up_all, w_down_all, layer):
    t, d = x2d.shape
    tm = min(ROW_TILE, t)
    wo = w_out.astype(BF16)
    row = lambda width: pl.BlockSpec((tm, width), lambda i: (i, 0))
    of_layer = lambda w: pl.BlockSpec((1,) + w.shape[1:], lambda i: (layer, 0, 0),
                                      pipeline_mode=pl.Buffered(1))
    return pl.pallas_call(
        functools.partial(_tail_kernel, ff_chunk=2048),
        grid=(t // tm,),
        in_specs=[row(a2d.shape[1]), row(d), _resident(wo.shape), _resident((1, d)),
                  of_layer(w_up_all), of_layer(w_down_all)],
        out_specs=row(d),
        out_shape=jax.ShapeDtypeStruct((t, d), F32),
        compiler_params=_params("parallel"),
        name="outproj_mlp",
    )(a2d, x2d, wo, mlp_gain.reshape(1, d), w_up_all, w_down_all)


def _norm_rope_rows(z, gain, cos, sin):
    if gain is not None:
        z = z * lax.rsqrt(jnp.mean(z * z, axis=0, keepdims=True) + EPS) * gain
    half = z.shape[0] // 2
    x1, x2 = z[:half], z[half:]
    return x1 * cos - x2 * sin, x2 * cos + x1 * sin


def _dsa_inproj_kernel(x_ref, gain_ref, wt_ref, qg_ref, kg_ref, kig_ref, c64_ref, s64_ref,
                       c32_ref, s32_ref, qt_ref, k_ref, vt_ref, qit_ref, ki_ref, wit_ref, buf_ref,
                       *, heads, kv_heads, dh, idx_heads, di, wscale):
    hn = _rms(x_ref[...], gain_ref[...]).astype(BF16)
    tm = hn.shape[0]
    proj = _dot_nt(wt_ref[...], hn)
    c64, s64 = c64_ref[...], s64_ref[...]
    c32, s32 = c32_ref[...], s32_ref[...]
    qg = jnp.broadcast_to(qg_ref[...], (dh, tm))
    kg = jnp.broadcast_to(kg_ref[...], (dh, tm))
    for h in range(heads):
        o1, o2 = _norm_rope_rows(proj[h * dh:(h + 1) * dh], qg, c64, s64)
        qt_ref[0, h * dh:h * dh + dh // 2, :] = o1.astype(BF16)
        qt_ref[0, h * dh + dh // 2:(h + 1) * dh, :] = o2.astype(BF16)
    off = heads * dh
    for h in range(kv_heads):
        o1, o2 = _norm_rope_rows(proj[off + h * dh:off + (h + 1) * dh], kg, c64, s64)
        k_ref[:, h * dh:(h + 1) * dh] = jnp.concatenate([o1, o2], axis=0).T.astype(BF16)
    off += kv_heads * dh
    vrows = vt_ref.shape[1] // kv_heads
    for g in range(kv_heads):
        vt_ref[0, g * vrows:g * vrows + dh, :] = proj[off + g * dh:off + (g + 1) * dh].astype(BF16)
        vt_ref[0, g * vrows + dh:(g + 1) * vrows, :] = jnp.ones((vrows - dh, tm), BF16)
    off += kv_heads * dh
    for h in range(idx_heads):
        o1, o2 = _norm_rope_rows(proj[off + h * di:off + (h + 1) * di], None, c32, s32)
        qit_ref[0, h * di:h * di + di // 2, :] = o1.astype(BF16)
        qit_ref[0, h * di + di // 2:(h + 1) * di, :] = o2.astype(BF16)
    off += idx_heads * di
    kig = jnp.broadcast_to(kig_ref[...], (di, tm))
    o1, o2 = _norm_rope_rows(proj[off:off + di], kig, c32, s32)
    ki_t = jnp.concatenate([o1, o2, proj[off + di:off + LANES]], axis=0)
    ki_ref[...] = ki_t.T[:, :di].astype(BF16)
    off += LANES
    wit_ref[0] = proj[off:off + idx_heads] * wscale
    buf_ref[...] = jnp.zeros_like(buf_ref)


def _dsa_attn_kernel(*refs, tile, topk, heads, kv_heads, dh, idx_heads, index_steps):
    qt_ref, k_ref, vt_ref, qit_ref, ki_ref, wit_ref = refs[:6]
    o_ref, sc_ref, lg_ref, e_ref, cap_ref = refs[-5:]
    tq = qt_ref.shape[2]
    ke = k_ref.shape[1]
    di = qit_ref.shape[1] // idx_heads
    kb = KEY_BLOCK
    slabs = kb // SUBLANES
    j = tile
    nblk = ke // kb
    neg = float(jnp.finfo(F32).min)
    big = float(jnp.finfo(F32).max)
    diag = (lax.broadcasted_iota(jnp.int32, (tq, tq), 0)
            <= lax.broadcasted_iota(jnp.int32, (tq, tq), 1))

    def rows_of(i):
        return pl.ds(pl.multiple_of(i * kb, kb), kb)

    def full(val):
        return jnp.full((SUBLANES, tq), val, F32)

    def spread(row):
        return jnp.broadcast_to(row, (SUBLANES, tq))

    group = heads // kv_heads

    def raw_logits(head, i):
        g = head // group
        lg_ref[head, rows_of(i), :] = _dot(k_ref[0, rows_of(i), g * dh:(g + 1) * dh],
                                           qt_ref[0, head * dh:(head + 1) * dh, :])

    def sweep(fn, init, head=None, unrolled=False):
        unroll = nblk if unrolled else max(u for u in range(1, SWEEP_UNROLL + 1) if nblk % u == 0)
        acc = lax.fori_loop(0, nblk, lambda i, a: fn(i, sc_ref[rows_of(i), :], a), init,
                            unroll=unroll)
        if head is not None:
            g = head // group
            lg = _dot(k_ref[0, :, g * dh:(g + 1) * dh], qt_ref[0, head * dh:(head + 1) * dh, :])
            lg_ref[head] = lg
            acc = acc + jnp.minimum(jnp.maximum(lg[ke - SUBLANES:, :], 0.0), 0.0)
        return acc

    if ke <= topk:
        sc_ref[...] = jnp.where(diag, 0.0, neg)
        for h in range(heads):
            lax.fori_loop(0, nblk, lambda i, c, h=h: (raw_logits(h, i), c)[1], 0, unroll=True)
    else:
        wit = wit_ref[0]
        ki = ki_ref[0]
        score = jnp.zeros((ke, tq), F32)
        for h in range(idx_heads):
            rel = jnp.maximum(_dot(ki, qit_ref[0, h * di:(h + 1) * di, :]), 0.0)
            score = score + rel * wit[h:h + 1, :]
        last = score[ke - tq:, :]
        row_max = jnp.max(jnp.where(diag, last, neg), axis=0, keepdims=True)
        row_min = jnp.min(jnp.where(diag, last, big), axis=0, keepdims=True)
        if ke > tq:
            row_max = jnp.maximum(row_max, jnp.max(score[:ke - tq, :], axis=0, keepdims=True))
            row_min = jnp.minimum(row_min, jnp.min(score[:ke - tq, :], axis=0, keepdims=True))
            sc_ref[:ke - tq, :] = score[:ke - tq, :]
        sc_ref[ke - tq:, :] = jnp.where(diag, last, neg)
        q_pos = j * tq + lax.broadcasted_iota(jnp.int32, (1, tq), 1)
        kvec = jnp.minimum(q_pos + 1, topk).astype(F32)

        def count_ge(th, head=None, unrolled=False):
            thb = spread(th)[None]
            acc = sweep(lambda i, blk, a: a + jnp.sum(
                jnp.where(blk.reshape(slabs, SUBLANES, tq) >= thb, 1.0, 0.0), axis=0),
                jnp.zeros((SUBLANES, tq), F32), head, unrolled)
            return jnp.sum(acc, axis=0, keepdims=True)

        def max_below(hi):
            hib = spread(hi)[None]

            def step(i, blk, a):
                blk = blk.reshape(slabs, SUBLANES, tq)
                return jnp.maximum(a, jnp.max(jnp.where(blk < hib, blk, neg), axis=0))

            return jnp.max(sweep(step, full(neg)), axis=0, keepdims=True)

        steps = max(BISECT_STEPS, heads)
        carried = {(h * steps) // heads: h for h in range(heads)}
        lo, hi, c_hi = row_min, row_max + jnp.abs(row_max) * 1e-3 + 1e-3, jnp.zeros((1, tq), F32)
        for step in range(steps):
            mid = lo * 0.5 + hi * 0.5
            c = count_ge(mid, carried.get(step), unrolled=True)
            ge = c >= kvec
            lo, hi, c_hi = jnp.where(ge, mid, lo), jnp.where(ge, hi, mid), jnp.where(ge, c_hi, c)

        def count_and_below(v):
            vb = spread(v)[None]

            def step(i, blk, a):
                blk = blk.reshape(slabs, SUBLANES, tq)
                ge = blk >= vb
                return (a[0] + jnp.sum(jnp.where(ge, 1.0, 0.0), axis=0),
                        jnp.maximum(a[1], jnp.max(jnp.where(ge, neg, blk), axis=0)))

            cnt, below = sweep(step, (jnp.zeros((SUBLANES, tq), F32), full(neg)))
            return jnp.sum(cnt, axis=0, keepdims=True), jnp.max(below, axis=0, keepdims=True)

        def pending(cv):
            return jnp.max(jnp.where(cv < kvec, 1.0, 0.0))

        def refine(state):
            c_hi, v, cv, below, _ = state
            short = cv < kvec
            c_hi = jnp.where(short, cv, c_hi)
            v = jnp.where(short, below, v)
            cv, below = count_and_below(v)
            return c_hi, v, cv, below, pending(cv)

        v = max_below(hi)
        cv, below = count_and_below(v)
        c_hi, v, cv, _, _ = lax.while_loop(lambda st: st[4] > 0.0, refine,
                                           (c_hi, v, cv, below, pending(cv)))
        need = kvec - c_hi
        surplus = jnp.max(jnp.where(cv > kvec, 1.0, 0.0)) > 0.0
        vb = spread(v)

        @pl.when(jnp.logical_not(surplus))
        def _():
            def to_bias(i, carry):
                blk = sc_ref[rows_of(i), :].reshape(slabs, SUBLANES, tq)
                sc_ref[rows_of(i), :] = jnp.where(blk >= vb[None], 0.0, neg).reshape(kb, tq)
                return carry
            lax.fori_loop(0, nblk, to_bias, 0)

        @pl.when(surplus)
        def _():
            def key_idx(i):
                return (lax.broadcasted_iota(jnp.int32, (slabs, SUBLANES, tq), 0) * SUBLANES
                        + lax.broadcasted_iota(jnp.int32, (slabs, SUBLANES, tq), 1) + i * kb)

            def idx_step(_, carry):
                lo_i, hi_i = carry
                mid_i = (lo_i + hi_i) >> 1
                midb = jnp.broadcast_to(mid_i, (SUBLANES, tq))
                acc = sweep(lambda i, blk, a: a + jnp.sum(jnp.where(
                    (blk.reshape(slabs, SUBLANES, tq) == vb[None]) & (key_idx(i) <= midb[None]),
                    1.0, 0.0), axis=0), jnp.zeros((SUBLANES, tq), F32))
                ok = jnp.sum(acc, axis=0, keepdims=True) >= need
                return jnp.where(ok, lo_i, mid_i), jnp.where(ok, mid_i, hi_i)

            _, istar = lax.fori_loop(0, index_steps, idx_step,
                                     (jnp.full((1, tq), -1, jnp.int32),
                                      jnp.full((1, tq), nblk * kb - 1, jnp.int32)))
            istarb = jnp.broadcast_to(istar, (SUBLANES, tq))

            def to_bias(i, carry):
                blk = sc_ref[rows_of(i), :].reshape(slabs, SUBLANES, tq)
                keep = (blk > vb[None]) | ((blk == vb[None]) & (key_idx(i) <= istarb[None]))
                sc_ref[rows_of(i), :] = jnp.where(keep, 0.0, neg).reshape(kb, tq)
                return carry
            lax.fori_loop(0, nblk, to_bias, 0)

    cap_val = float(jnp.finfo(BF16).max)

    packed = 2 * SUBLANES

    def max_pass(i, maxima):
        rows = rows_of(i)
        selected = sc_ref[rows, :] == 0.0
        cap_ref[rows, :] = jnp.where(selected, cap_val, 0.0).astype(BF16)
        bias = jnp.where(selected, 0.0, -cap_val).astype(BF16)
        new = []
        for h in range(heads):
            lg = lg_ref[h, rows, :].astype(BF16) + bias
            new.append(jnp.maximum(maxima[h], jnp.max(lg.reshape(kb // packed, packed, tq), axis=0)))
        return tuple(new)

    maxima = lax.fori_loop(0, nblk, max_pass, (jnp.full((packed, tq), -cap_val, BF16),) * heads,
                           unroll=min(nblk, 2))
    maxima = [m16.astype(F32) for m16 in maxima]

    vrows = vt_ref.shape[1] // kv_heads
    for h in range(heads):
        g = h // group
        m = jnp.max(maxima[h], axis=0, keepdims=True)
        for c in range(ke // tq):
            rows = slice(c * tq, (c + 1) * tq)
            e = jnp.exp2(lg_ref[h, rows, :] - m).astype(BF16)
            e_ref[h, rows, :] = jnp.minimum(e, cap_ref[rows, :])
        out_t = _dot(vt_ref[0, g * vrows:(g + 1) * vrows, :], e_ref[h])
        o_ref[0, :, h * dh:(h + 1) * dh] = (out_t[:dh] / out_t[dh:dh + 1]).T.astype(BF16)


def _sparse_attention_mixer(x2d, b, s, attn_gain, w_in, q_gain, k_gain, kidx_gain, tabs):
    t, d = x2d.shape
    heads, kv_heads, idx_heads, di = ATT_HEADS, ATT_KV_HEADS, IDX_HEADS, IDX_DH
    dh = d // heads
    c64, s64, c32, s32 = tabs
    tm = min(DSA_ROW_TILE, s)
    per_b = s // tm
    nq, nkv, nqi = heads * dh, kv_heads * dh, idx_heads * di
    nvt = kv_heads * (dh + 2 * SUBLANES)
    n_main = nq + 2 * nkv + nqi + di
    zeros = lambda n: jnp.zeros((d, n), w_in.dtype)
    wt = jnp.concatenate([w_in[:, :n_main], zeros(LANES - di), w_in[:, n_main:n_main + idx_heads],
                          zeros(2 * SUBLANES - idx_heads)], axis=1).T.astype(BF16)
    qg = (q_gain * (dh ** -0.5 * math.log2(math.e))).reshape(dh, 1)
    row = lambda width: pl.BlockSpec((tm, width), lambda i: (i, 0))
    col = lambda n: pl.BlockSpec((n, tm), lambda i: (0, i))
    seq_t = lambda n: pl.BlockSpec((1, n, tm), lambda i: (i // per_b, 0, i % per_b))
    shp = lambda width, dt: jax.ShapeDtypeStruct((t, width), dt)
    shp_t = lambda n, dt: jax.ShapeDtypeStruct((b, n, s), dt)
    qt, k, vt, qit, ki, wit, out = pl.pallas_call(
        functools.partial(_dsa_inproj_kernel, heads=heads, kv_heads=kv_heads, dh=dh,
                          idx_heads=idx_heads, di=di, wscale=idx_heads ** -0.5 * di ** -0.5),
        grid=(t // tm,),
        in_specs=[row(d), _resident((1, d)), _resident(wt.shape), _resident((dh, 1)),
                  _resident((dh, 1)), _resident((di, 1)), col(dh // 2), col(dh // 2),
                  col(di // 2), col(di // 2)],
        out_specs=[seq_t(nq), row(nkv), seq_t(nvt), seq_t(nqi), row(di), seq_t(idx_heads),
                   row(nq)],
        out_shape=[shp_t(nq, BF16), shp(nkv, BF16), shp_t(nvt, BF16), shp_t(nqi, BF16),
                   shp(di, BF16), shp_t(idx_heads, F32), shp(nq, BF16)],
        compiler_params=_params("parallel"),
        name="dsa_inproj",
    )(x2d, attn_gain.reshape(1, d), wt, qg, k_gain.reshape(dh, 1), kidx_gain.reshape(di, 1),
      c64, s64, c32, s32)

    topk = min(TOPK_MAX, s // 4)
    tq = min(Q_TILE, s)
    k, ki = k.reshape(b, s, nkv), ki.reshape(b, s, di)
    out = out.reshape(b, s, nq)
    for tile in range(s // tq):
        ke = (tile + 1) * tq
        operands = [qt, k, vt, qit, ki, wit]
        in_specs = [pl.BlockSpec((1, nq, tq), lambda bi, tile=tile: (bi, 0, tile)),
                    pl.BlockSpec((1, ke, nkv), lambda bi: (bi, 0, 0)),
                    pl.BlockSpec((1, nvt, ke), lambda bi: (bi, 0, 0)),
                    pl.BlockSpec((1, nqi, tq), lambda bi, tile=tile: (bi, 0, tile)),
                    pl.BlockSpec((1, ke, di), lambda bi: (bi, 0, 0)),
                    pl.BlockSpec((1, idx_heads, tq), lambda bi, tile=tile: (bi, 0, tile))]
        operands.append(out)
        in_specs.append(pl.BlockSpec(memory_space=pl.ANY))
        out = pl.pallas_call(
            functools.partial(_dsa_attn_kernel, tile=tile, topk=topk, heads=heads,
                              kv_heads=kv_heads, dh=dh, idx_heads=idx_heads,
                              index_steps=(ke - 1).bit_length() + 1),
            grid=(b,),
            in_specs=in_specs,
            out_specs=pl.BlockSpec((1, tq, nq), lambda bi, tile=tile: (bi, tile, 0)),
            out_shape=jax.ShapeDtypeStruct((b, s, nq), BF16),
            scratch_shapes=[pltpu.VMEM((ke, tq), F32), pltpu.VMEM((heads, ke, tq), F32),
                            pltpu.VMEM((heads, ke, tq), BF16), pltpu.VMEM((ke, tq), BF16)],
            input_output_aliases={len(operands) - 1: 0},
            compiler_params=_params("parallel"),
            name=f"dsa_attn_{tile}",
        )(*operands)
    return out.reshape(t, nq)


def kernel(x, positions, attn_norm, ret_w_in, ret_out_norm, ret_w_out, dsa_w_in, dsa_q_norm,
           dsa_k_norm, dsa_kidx_norm, dsa_w_out, mlp_norm, mlp_w_up, mlp_w_down):
    b, s, d = x.shape
    depth = attn_norm.shape[0]
    x2d = x.reshape(b * s, d)
    tables = None
    for i in range(depth):
        j = i // 2
        if i % 2 == 0:
            a, made = _retention_mixer(x2d, b, s, attn_norm[i], ret_w_in[j], positions)
            tables = made if tables is None else tables
            w_out = ret_out_norm[j].reshape(-1, 1) * ret_w_out[j]
        else:
            a = _sparse_attention_mixer(x2d, b, s, attn_norm[i], dsa_w_in[j], dsa_q_norm[j],
                                        dsa_k_norm[j], dsa_kidx_norm[j], tables)
            w_out = dsa_w_out[j]
        x2d = _mixer_out_and_mlp(a, x2d, w_out, mlp_norm[i], mlp_w_up, mlp_w_down, i)
    return x2d.reshape(b, s, d)
```

```python
import functools
import math

import jax
import jax.numpy as jnp
import numpy as np
from jax import lax
from jax.experimental import pallas as pl
from jax.experimental.pallas import tpu as pltpu

F32 = jnp.float32
BF16 = jnp.bfloat16

EPS = 1e-6
ROPE_THETA = 10000.0
RET_HEADS = 4
RET_CHUNK = 256
ATT_HEADS = 8
ATT_KV_HEADS = 2
IDX_HEADS = 8
IDX_DH = 64
TOPK_MAX = 256
LANES = 128
SUBLANES = 8

ROW_TILE = 512
DSA_ROW_TILE = 1024
RET_ROW_BLOCK = 1024
Q_TILE = 256
KEY_BLOCK = 128
SWEEP_UNROLL = 8
VMEM_LIMIT = 56 * 1024 * 1024
BISECT_STEPS = 16


def _params(*sem):
    return pltpu.CompilerParams(dimension_semantics=sem, vmem_limit_bytes=VMEM_LIMIT)


def _resident(shape):
    nd = len(shape)
    return pl.BlockSpec(shape, lambda *_: (0,) * nd, pipeline_mode=pl.Buffered(1))


def _rms(x, gain):
    return x * lax.rsqrt(jnp.mean(x * x, axis=-1, keepdims=True) + EPS) * gain


def _dot(a, b):
    return jnp.dot(a, b, preferred_element_type=F32)


def _dot_nt(a, b):
    return lax.dot_general(a, b, (((1,), (1,)), ((), ())), preferred_element_type=F32)


def _dot_tn(a, b):
    return lax.dot_general(a, b, (((0,), (0,)), ((), ())), preferred_element_type=F32)


def _rope_slab_in_kernel(c, pos_ref, inv_ref, c64t_ref, s64t_ref, c32t_ref, s32t_ref, ct_ref,
                         st_ref):
    lanes = slice(c * LANES, (c + 1) * LANES)
    ang_t = inv_ref[...] * pos_ref[:, lanes].astype(F32)
    cos_t = jnp.cos(ang_t)
    sin_t = jnp.sin(ang_t)
    ct_ref[c] = cos_t
    st_ref[c] = sin_t
    for step, c_ref, s_ref in ((2, c64t_ref, s64t_ref), (4, c32t_ref, s32t_ref)):
        rows = pl.ds(0, LANES // step, stride=step)
        c_ref[:, lanes] = ct_ref[c, rows, :]
        s_ref[:, lanes] = st_ref[c, rows, :]
    return cos_t.T, sin_t.T


def _ret_inproj_kernel(x_ref, gain_ref, w_ref, pos_ref, inv_ref, q_ref, k_ref, v_ref, g_ref,
                       c64t_ref, s64t_ref, c32t_ref, s32t_ref, ct_ref, st_ref,
                       *, heads, dk, dv, kscale):
    hn = _rms(x_ref[...], gain_ref[...]).astype(BF16)
    half = dk // 2
    width = heads * dv
    for off, out_ref, swish in ((2 * heads * dk + width, g_ref, True),
                                (2 * heads * dk, v_ref, False)):
        for c in range(width // dv):
            z = _dot(hn, w_ref[:, off + c * dv:off + (c + 1) * dv].astype(BF16))
            if swish:
                z = z * (0.5 * jnp.tanh(0.5 * z) + 0.5)
            out_ref[:, c * dv:(c + 1) * dv] = z.astype(BF16)
    trig = [_rope_slab_in_kernel(c, pos_ref, inv_ref, c64t_ref, s64t_ref, c32t_ref, s32t_ref,
                                 ct_ref, st_ref) for c in range(ct_ref.shape[0])]
    cos = jnp.concatenate([t[0] for t in trig], axis=0)
    sin = jnp.concatenate([t[1] for t in trig], axis=0)
    for h in range(heads):
        for off, out_ref, scale in ((0, q_ref, None), (heads * dk, k_ref, kscale)):
            z = _dot(hn, w_ref[:, off + h * dk:off + (h + 1) * dk].astype(BF16))
            x1, x2 = z[:, :half], z[:, half:]
            o1 = x1 * cos - x2 * sin
            o2 = x2 * cos + x1 * sin
            if scale is not None:
                o1, o2 = o1 * scale, o2 * scale
            out_ref[:, h * dk:h * dk + half] = o1.astype(BF16)
            out_ref[:, h * dk + half:(h + 1) * dk] = o2.astype(BF16)


def _ret_core_kernel(q_ref, k_ref, v_ref, g_ref, dm_ref, qd_ref, kd_ref, cd_ref,
                     o_ref, state_ref, *, heads, chunk):
    @pl.when(pl.program_id(1) == 0)
    def _():
        state_ref[...] = jnp.zeros_like(state_ref)

    dk = q_ref.shape[2] // heads
    dv = v_ref.shape[2] // heads
    hs = range(heads)
    states = [state_ref[h] for h in hs]
    for c in range(q_ref.shape[1] // chunk):
        rows = slice(c * chunk, (c + 1) * chunk)
        q = [q_ref[0, rows, h * dk:(h + 1) * dk] for h in hs]
        k = [k_ref[0, rows, h * dk:(h + 1) * dk] for h in hs]
        v = [v_ref[0, rows, h * dv:(h + 1) * dv] for h in hs]
        scores = [_dot_nt(q[h], k[h]) for h in hs]
        kv = [_dot_tn((k[h].astype(F32) * kd_ref[h]).astype(BF16), v[h]) for h in hs]
        cross = [_dot(q[h], states[h].astype(BF16)) for h in hs]
        intra = [_dot((scores[h] * dm_ref[h]).astype(BF16), v[h]) for h in hs]
        for h in hs:
            o = intra[h] + cross[h] * qd_ref[h]
            y = o * lax.rsqrt(jnp.mean(o * o, axis=-1, keepdims=True) + EPS)
            gate = g_ref[0, rows, h * dv:(h + 1) * dv].astype(F32)
            o_ref[0, rows, h * dv:(h + 1) * dv] = (y * gate).astype(BF16)
            states[h] = states[h] * cd_ref[h] + kv[h]
    for h in hs:
        state_ref[h] = states[h]


def _retention_mixer(x2d, b, s, attn_gain, w_in, positions):
    t, d = x2d.shape
    heads = RET_HEADS
    dk = d // heads
    dv = 2 * dk
    chunk = RET_CHUNK
    tm = min(ROW_TILE, t)
    w = w_in
    row = lambda width: pl.BlockSpec((tm, width), lambda i: (i, 0))
    col = lambda n: pl.BlockSpec((n, tm), lambda i: (0, i))
    tab = lambda n: jax.ShapeDtypeStruct((n, t), F32)
    inv = ROPE_THETA ** (-jnp.arange(LANES, dtype=F32) / LANES)
    q, k, v, g, *tables = pl.pallas_call(
        functools.partial(_ret_inproj_kernel, heads=heads, dk=dk, dv=dv, kscale=dk ** -0.5),
        grid=(t // tm,),
        in_specs=[row(d), _resident((1, d)), _resident(w.shape),
                  pl.BlockSpec((1, tm), lambda i: (0, i)), _resident((LANES, 1))],
        out_specs=[row(heads * dk), row(heads * dk), row(heads * dv), row(heads * dv),
                   col(64), col(64), col(32), col(32)],
        out_shape=[jax.ShapeDtypeStruct((t, heads * dk), BF16),
                   jax.ShapeDtypeStruct((t, heads * dk), BF16),
                   jax.ShapeDtypeStruct((t, heads * dv), BF16),
                   jax.ShapeDtypeStruct((t, heads * dv), BF16),
                   tab(64), tab(64), tab(32), tab(32)],
        scratch_shapes=[pltpu.VMEM((tm // LANES, LANES, LANES), F32)] * 2,
        compiler_params=_params("parallel"),
        name="ret_inproj",
    )(x2d, attn_gain.reshape(1, d), w, positions.reshape(1, t), inv.reshape(LANES, 1))

    log_gamma = np.log1p(-(np.float32(2.0) ** (-5.0 - np.arange(heads, dtype=np.float32))))
    i = np.arange(chunk, dtype=np.float32)
    diff = i[:, None] - i[None, :]
    dm = np.where(diff >= 0, np.exp(log_gamma[:, None, None] * np.maximum(diff, 0.0)), 0.0)
    qd = np.exp(log_gamma[:, None] * (i + 1.0))
    kd = np.exp(log_gamma[:, None] * (chunk - 1.0 - i))
    cd = np.exp(log_gamma * chunk)
    dm = jnp.asarray(dm, F32)
    qd = jnp.asarray(np.broadcast_to(qd[:, :, None], (heads, chunk, dv)), F32)
    kd = jnp.asarray(np.broadcast_to(kd[:, :, None], (heads, chunk, dk)), F32)
    cd = jnp.asarray(np.broadcast_to(cd[:, None, None], (heads, 1, dv)), F32)

    rb = min(RET_ROW_BLOCK, s)
    seq = lambda width: pl.BlockSpec((1, rb, width), lambda bi, r: (bi, r, 0))
    o = pl.pallas_call(
        functools.partial(_ret_core_kernel, heads=heads, chunk=chunk),
        grid=(b, s // rb),
        in_specs=[seq(heads * dk), seq(heads * dk), seq(heads * dv), seq(heads * dv),
                  _resident(dm.shape), _resident(qd.shape), _resident(kd.shape),
                  _resident(cd.shape)],
        out_specs=seq(heads * dv),
        out_shape=jax.ShapeDtypeStruct((b, s, heads * dv), BF16),
        scratch_shapes=[pltpu.VMEM((heads, dk, dv), F32)],
        compiler_params=_params("parallel", "arbitrary"),
        name="ret_core",
    )(q.reshape(b, s, -1), k.reshape(b, s, -1), v.reshape(b, s, -1), g.reshape(b, s, -1),
      dm, qd, kd, cd)
    return o.reshape(t, heads * dv), tuple(tables)


def _tail_kernel(a_ref, x_ref, wo_ref, gain_ref, wup_ref, wdn_ref, o_ref, *, ff_chunk):
    x1 = x_ref[...] + _dot(a_ref[...], wo_ref[...])
    hn = _rms(x1, gain_ref[...]).astype(BF16)
    acc = x1
    for c in range(wup_ref.shape[2] // ff_chunk):
        cols = slice(c * ff_chunk, (c + 1) * ff_chunk)
        u = jnp.maximum(_dot(hn, wup_ref[0, :, cols].astype(BF16)), 0.0)
        acc = acc + _dot((u * u).astype(BF16), wdn_ref[0, cols, :].astype(BF16))
    o_ref[...] = acc


def _mixer_out_and_mlp(a2d, x2d, w_out, mlp_gain, w_up_all, w_down_all, layer):
    t, d = x2d.shape
    tm = min(ROW_TILE, t)
    wo = w_out.astype(BF16)
    row = lambda width: pl.BlockSpec((tm, width), lambda i: (i, 0))
    of_layer = lambda w: pl.BlockSpec((1,) + w.shape[1:], lambda i: (layer, 0, 0),
                                      pipeline_mode=pl.Buffered(1))
    return pl.pallas_call(
        functools.partial(_tail_kernel, ff_chunk=2048),
        grid=(t // tm,),
        in_specs=[row(a2d.shape[1]), row(d), _resident(wo.shape), _resident((1, d)),
                  of_layer(w_up_all), of_layer(w_down_all)],
        out_specs=row(d),
        out_shape=jax.ShapeDtypeStruct((t, d), F32),
        compiler_params=_params("parallel"),
        name="outproj_mlp",
    )(a2d, x2d, wo, mlp_gain.reshape(1, d), w_up_all, w_down_all)


def _norm_rope_rows(z, gain, cos, sin):
    if gain is not None:
        z = z * lax.rsqrt(jnp.mean(z * z, axis=0, keepdims=True) + EPS) * gain
    half = z.shape[0] // 2
    x1, x2 = z[:half], z[half:]
    return x1 * cos - x2 * sin, x2 * cos + x1 * sin


def _dsa_inproj_kernel(x_ref, gain_ref, wt_ref, qg_ref, kg_ref, kig_ref, c64_ref, s64_ref,
                       c32_ref, s32_ref, qt_ref, k_ref, vt_ref, qit_ref, ki_ref, wit_ref, buf_ref,
                       *, heads, kv_heads, dh, idx_heads, di, wscale):
    hn = _rms(x_ref[...], gain_ref[...]).astype(BF16)
    tm = hn.shape[0]
    proj = _dot_nt(wt_ref[...], hn)
    c64, s64 = c64_ref[...], s64_ref[...]
    c32, s32 = c32_ref[...], s32_ref[...]
    qg = jnp.broadcast_to(qg_ref[...], (dh, tm))
    kg = jnp.broadcast_to(kg_ref[...], (dh, tm))
    for h in range(heads):
        o1, o2 = _norm_rope_rows(proj[h * dh:(h + 1) * dh], qg, c64, s64)
        qt_ref[0, h * dh:h * dh + dh // 2, :] = o1.astype(BF16)
        qt_ref[0, h * dh + dh // 2:(h + 1) * dh, :] = o2.astype(BF16)
    off = heads * dh
    for h in range(kv_heads):
        o1, o2 = _norm_rope_rows(proj[off + h * dh:off + (h + 1) * dh], kg, c64, s64)
        k_ref[:, h * dh:(h + 1) * dh] = jnp.concatenate([o1, o2], axis=0).T.astype(BF16)
    off += kv_heads * dh
    vrows = vt_ref.shape[1] // kv_heads
    for g in range(kv_heads):
        vt_ref[0, g * vrows:g * vrows + dh, :] = proj[off + g * dh:off + (g + 1) * dh].astype(BF16)
        vt_ref[0, g * vrows + dh:(g + 1) * vrows, :] = jnp.ones((vrows - dh, tm), BF16)
    off += kv_heads * dh
    for h in range(idx_heads):
        o1, o2 = _norm_rope_rows(proj[off + h * di:off + (h + 1) * di], None, c32, s32)
        qit_ref[0, h * di:h * di + di // 2, :] = o1.astype(BF16)
        qit_ref[0, h * di + di // 2:(h + 1) * di, :] = o2.astype(BF16)
    off += idx_heads * di
    kig = jnp.broadcast_to(kig_ref[...], (di, tm))
    o1, o2 = _norm_rope_rows(proj[off:off + di], kig, c32, s32)
    ki_t = jnp.concatenate([o1, o2, proj[off + di:off + LANES]], axis=0)
    ki_ref[...] = ki_t.T[:, :di].astype(BF16)
    off += LANES
    wit_ref[0] = proj[off:off + idx_heads] * wscale
    buf_ref[...] = jnp.zeros_like(buf_ref)


def _dsa_attn_kernel(*refs, tile, topk, heads, kv_heads, dh, idx_heads, index_steps):
    qt_ref, k_ref, vt_ref, qit_ref, ki_ref, wit_ref = refs[:6]
    o_ref, sc_ref, lg_ref, e_ref, cap_ref = refs[-5:]
    tq = qt_ref.shape[2]
    ke = k_ref.shape[1]
    di = qit_ref.shape[1] // idx_heads
    kb = KEY_BLOCK
    slabs = kb // SUBLANES
    j = tile
    nblk = ke // kb
    neg = float(jnp.finfo(F32).min)
    big = float(jnp.finfo(F32).max)
    diag = (lax.broadcasted_iota(jnp.int32, (tq, tq), 0)
            <= lax.broadcasted_iota(jnp.int32, (tq, tq), 1))

    def rows_of(i):
        return pl.ds(pl.multiple_of(i * kb, kb), kb)

    def full(val):
        return jnp.full((SUBLANES, tq), val, F32)

    def spread(row):
        return jnp.broadcast_to(row, (SUBLANES, tq))

    group = heads // kv_heads

    def raw_logits(head, i):
        g = head // group
        lg_ref[head, rows_of(i), :] = _dot(k_ref[0, rows_of(i), g * dh:(g + 1) * dh],
                                           qt_ref[0, head * dh:(head + 1) * dh, :])

    def sweep(fn, init, head=None, unrolled=False):
        unroll = nblk if unrolled else max(u for u in range(1, SWEEP_UNROLL + 1) if nblk % u == 0)
        acc = lax.fori_loop(0, nblk, lambda i, a: fn(i, sc_ref[rows_of(i), :], a), init,
                            unroll=unroll)
        if head is not None:
            g = head // group
            lg = _dot(k_ref[0, :, g * dh:(g + 1) * dh], qt_ref[0, head * dh:(head + 1) * dh, :])
            lg_ref[head] = lg
            acc = acc + jnp.minimum(jnp.maximum(lg[ke - SUBLANES:, :], 0.0), 0.0)
        return acc

    if ke <= topk:
        sc_ref[...] = jnp.where(diag, 0.0, neg)
        for h in range(heads):
            lax.fori_loop(0, nblk, lambda i, c, h=h: (raw_logits(h, i), c)[1], 0, unroll=True)
    else:
        wit = wit_ref[0]
        ki = ki_ref[0]
        score = jnp.zeros((ke, tq), F32)
        for h in range(idx_heads):
            rel = jnp.maximum(_dot(ki, qit_ref[0, h * di:(h + 1) * di, :]), 0.0)
            score = score + rel * wit[h:h + 1, :]
        last = score[ke - tq:, :]
        row_max = jnp.max(jnp.where(diag, last, neg), axis=0, keepdims=True)
        row_min = jnp.min(jnp.where(diag, last, big), axis=0, keepdims=True)
        if ke > tq:
            row_max = jnp.maximum(row_max, jnp.max(score[:ke - tq, :], axis=0, keepdims=True))
            row_min = jnp.minimum(row_min, jnp.min(score[:ke - tq, :], axis=0, keepdims=True))
            sc_ref[:ke - tq, :] = score[:ke - tq, :]
        sc_ref[ke - tq:, :] = jnp.where(diag, last, neg)
        q_pos = j * tq + lax.broadcasted_iota(jnp.int32, (1, tq), 1)
        kvec = jnp.minimum(q_pos + 1, topk).astype(F32)

        def count_ge(th, head=None, unrolled=False):
            thb = spread(th)[None]
            acc = sweep(lambda i, blk, a: a + jnp.sum(
                jnp.where(blk.reshape(slabs, SUBLANES, tq) >= thb, 1.0, 0.0), axis=0),
                jnp.zeros((SUBLANES, tq), F32), head, unrolled)
            return jnp.sum(acc, axis=0, keepdims=True)

        def max_below(hi):
            hib = spread(hi)[None]

            def step(i, blk, a):
                blk = blk.reshape(slabs, SUBLANES, tq)
                return jnp.maximum(a, jnp.max(jnp.where(blk < hib, blk, neg), axis=0))

            return jnp.max(sweep(step, full(neg)), axis=0, keepdims=True)

        steps = max(BISECT_STEPS, heads)
        carried = {(h * steps) // heads: h for h in range(heads)}
        lo, hi, c_hi = row_min, row_max + jnp.abs(row_max) * 1e-3 + 1e-3, jnp.zeros((1, tq), F32)
        for step in range(steps):
            mid = lo * 0.5 + hi * 0.5
            c = count_ge(mid, carried.get(step), unrolled=True)
            ge = c >= kvec
            lo, hi, c_hi = jnp.where(ge, mid, lo), jnp.where(ge, hi, mid), jnp.where(ge, c_hi, c)

        def count_and_below(v):
            vb = spread(v)[None]

            def step(i, blk, a):
                blk = blk.reshape(slabs, SUBLANES, tq)
                ge = blk >= vb
                return (a[0] + jnp.sum(jnp.where(ge, 1.0, 0.0), axis=0),
                        jnp.maximum(a[1], jnp.max(jnp.where(ge, neg, blk), axis=0)))

            cnt, below = sweep(step, (jnp.zeros((SUBLANES, tq), F32), full(neg)))
            return jnp.sum(cnt, axis=0, keepdims=True), jnp.max(below, axis=0, keepdims=True)

        def pending(cv):
            return jnp.max(jnp.where(cv < kvec, 1.0, 0.0))

        def refine(state):
            c_hi, v, cv, below, _ = state
            short = cv < kvec
            c_hi = jnp.where(short, cv, c_hi)
            v = jnp.where(short, below, v)
            cv, below = count_and_below(v)
            return c_hi, v, cv, below, pending(cv)

        v = max_below(hi)
        cv, below = count_and_below(v)
        c_hi, v, cv, _, _ = lax.while_loop(lambda st: st[4] > 0.0, refine,
                                           (c_hi, v, cv, below, pending(cv)))
        need = kvec - c_hi
        surplus = jnp.max(jnp.where(cv > kvec, 1.0, 0.0)) > 0.0
        vb = spread(v)

        @pl.when(jnp.logical_not(surplus))
        def _():
            def to_bias(i, carry):
                blk = sc_ref[rows_of(i), :].reshape(slabs, SUBLANES, tq)
                sc_ref[rows_of(i), :] = jnp.where(blk >= vb[None], 0.0, neg).reshape(kb, tq)
                return carry
            lax.fori_loop(0, nblk, to_bias, 0)

        @pl.when(surplus)
        def _():
            def key_idx(i):
                return (lax.broadcasted_iota(jnp.int32, (slabs, SUBLANES, tq), 0) * SUBLANES
                        + lax.broadcasted_iota(jnp.int32, (slabs, SUBLANES, tq), 1) + i * kb)

            def idx_step(_, carry):
                lo_i, hi_i = carry
                mid_i = (lo_i + hi_i) >> 1
                midb = jnp.broadcast_to(mid_i, (SUBLANES, tq))
                acc = sweep(lambda i, blk, a: a + jnp.sum(jnp.where(
                    (blk.reshape(slabs, SUBLANES, tq) == vb[None]) & (key_idx(i) <= midb[None]),
                    1.0, 0.0), axis=0), jnp.zeros((SUBLANES, tq), F32))
                ok = jnp.sum(acc, axis=0, keepdims=True) >= need
                return jnp.where(ok, lo_i, mid_i), jnp.where(ok, mid_i, hi_i)

            _, istar = lax.fori_loop(0, index_steps, idx_step,
                                     (jnp.full((1, tq), -1, jnp.int32),
                                      jnp.full((1, tq), nblk * kb - 1, jnp.int32)))
            istarb = jnp.broadcast_to(istar, (SUBLANES, tq))

            def to_bias(i, carry):
                blk = sc_ref[rows_of(i), :].reshape(slabs, SUBLANES, tq)
                keep = (blk > vb[None]) | ((blk == vb[None]) & (key_idx(i) <= istarb[None]))
                sc_ref[rows_of(i), :] = jnp.where(keep, 0.0, neg).reshape(kb, tq)
                return carry
            lax.fori_loop(0, nblk, to_bias, 0)

    cap_val = float(jnp.finfo(BF16).max)

    packed = 2 * SUBLANES

    def max_pass(i, maxima):
        rows = rows_of(i)
        selected = sc_ref[rows, :] == 0.0
        cap_ref[rows, :] = jnp.where(selected, cap_val, 0.0).astype(BF16)
        bias = jnp.where(selected, 0.0, -cap_val).astype(BF16)
        new = []
        for h in range(heads):
            lg = lg_ref[h, rows, :].astype(BF16) + bias
            new.append(jnp.maximum(maxima[h], jnp.max(lg.reshape(kb // packed, packed, tq), axis=0)))
        return tuple(new)

    maxima = lax.fori_loop(0, nblk, max_pass, (jnp.full((packed, tq), -cap_val, BF16),) * heads,
                           unroll=min(nblk, 2))
    maxima = [m16.astype(F32) for m16 in maxima]

    vrows = vt_ref.shape[1] // kv_heads
    for h in range(heads):
        g = h // group
        m = jnp.max(maxima[h], axis=0, keepdims=True)
        for c in range(ke // tq):
            rows = slice(c * tq, (c + 1) * tq)
            e = jnp.exp2(lg_ref[h, rows, :] - m).astype(BF16)
            e_ref[h, rows, :] = jnp.minimum(e, cap_ref[rows, :])
        out_t = _dot(vt_ref[0, g * vrows:(g + 1) * vrows, :], e_ref[h])
        o_ref[0, :, h * dh:(h + 1) * dh] = (out_t[:dh] / out_t[dh:dh + 1]).T.astype(BF16)


def _sparse_attention_mixer(x2d, b, s, attn_gain, w_in, q_gain, k_gain, kidx_gain, tabs):
    t, d = x2d.shape
    heads, kv_heads, idx_heads, di = ATT_HEADS, ATT_KV_HEADS, IDX_HEADS, IDX_DH
    dh = d // heads
    c64, s64, c32, s32 = tabs
    tm = min(DSA_ROW_TILE, s)
    per_b = s // tm
    nq, nkv, nqi = heads * dh, kv_heads * dh, idx_heads * di
    nvt = kv_heads * (dh + 2 * SUBLANES)
    n_main = nq + 2 * nkv + nqi + di
    zeros = lambda n: jnp.zeros((d, n), w_in.dtype)
    wt = jnp.concatenate([w_in[:, :n_main], zeros(LANES - di), w_in[:, n_main:n_main + idx_heads],
                          zeros(2 * SUBLANES - idx_heads)], axis=1).T.astype(BF16)
    qg = (q_gain * (dh ** -0.5 * math.log2(math.e))).reshape(dh, 1)
    row = lambda width: pl.BlockSpec((tm, width), lambda i: (i, 0))
    col = lambda n: pl.BlockSpec((n, tm), lambda i: (0, i))
    seq_t = lambda n: pl.BlockSpec((1, n, tm), lambda i: (i // per_b, 0, i % per_b))
    shp = lambda width, dt: jax.ShapeDtypeStruct((t, width), dt)
    shp_t = lambda n, dt: jax.ShapeDtypeStruct((b, n, s), dt)
    qt, k, vt, qit, ki, wit, out = pl.pallas_call(
        functools.partial(_dsa_inproj_kernel, heads=heads, kv_heads=kv_heads, dh=dh,
                          idx_heads=idx_heads, di=di, wscale=idx_heads ** -0.5 * di ** -0.5),
        grid=(t // tm,),
        in_specs=[row(d), _resident((1, d)), _resident(wt.shape), _resident((dh, 1)),
                  _resident((dh, 1)), _resident((di, 1)), col(dh // 2), col(dh // 2),
                  col(di // 2), col(di // 2)],
        out_specs=[seq_t(nq), row(nkv), seq_t(nvt), seq_t(nqi), row(di), seq_t(idx_heads),
                   row(nq)],
        out_shape=[shp_t(nq, BF16), shp(nkv, BF16), shp_t(nvt, BF16), shp_t(nqi, BF16),
                   shp(di, BF16), shp_t(idx_heads, F32), shp(nq, BF16)],
        compiler_params=_params("parallel"),
        name="dsa_inproj",
    )(x2d, attn_gain.reshape(1, d), wt, qg, k_gain.reshape(dh, 1), kidx_gain.reshape(di, 1),
      c64, s64, c32, s32)

    topk = min(TOPK_MAX, s // 4)
    tq = min(Q_TILE, s)
    k, ki = k.reshape(b, s, nkv), ki.reshape(b, s, di)
    out = out.reshape(b, s, nq)
    for tile in range(s // tq):
        ke = (tile + 1) * tq
        operands = [qt, k, vt, qit, ki, wit]
        in_specs = [pl.BlockSpec((1, nq, tq), lambda bi, tile=tile: (bi, 0, tile)),
                    pl.BlockSpec((1, ke, nkv), lambda bi: (bi, 0, 0)),
                    pl.BlockSpec((1, nvt, ke), lambda bi: (bi, 0, 0)),
                    pl.BlockSpec((1, nqi, tq), lambda bi, tile=tile: (bi, 0, tile)),
                    pl.BlockSpec((1, ke, di), lambda bi: (bi, 0, 0)),
                    pl.BlockSpec((1, idx_heads, tq), lambda bi, tile=tile: (bi, 0, tile))]
        operands.append(out)
        in_specs.append(pl.BlockSpec(memory_space=pl.ANY))
        out = pl.pallas_call(
            functools.partial(_dsa_attn_kernel, tile=tile, topk=topk, heads=heads,
                              kv_heads=kv_heads, dh=dh, idx_heads=idx_heads,
                              index_steps=(ke - 1).bit_length() + 1),
            grid=(b,),
            in_specs=in_specs,
            out_specs=pl.BlockSpec((1, tq, nq), lambda bi, tile=tile: (bi, tile, 0)),
            out_shape=jax.ShapeDtypeStruct((b, s, nq), BF16),
            scratch_shapes=[pltpu.VMEM((ke, tq), F32), pltpu.VMEM((heads, ke, tq), F32),
                            pltpu.VMEM((heads, ke, tq), BF16), pltpu.VMEM((ke, tq), BF16)],
            input_output_aliases={len(operands) - 1: 0},
            compiler_params=_params("parallel"),
            name=f"dsa_attn_{tile}",
        )(*operands)
    return out.reshape(t, nq)


def kernel(x, positions, attn_norm, ret_w_in, ret_out_norm, ret_w_out, dsa_w_in, dsa_q_norm,
           dsa_k_norm, dsa_kidx_norm, dsa_w_out, mlp_norm, mlp_w_up, mlp_w_down):
    b, s, d = x.shape
    depth = attn_norm.shape[0]
    x2d = x.reshape(b * s, d)
    tables = None
    for i in range(depth):
        j = i // 2
        if i % 2 == 0:
            a, made = _retention_mixer(x2d, b, s, attn_norm[i], ret_w_in[j], positions)
            tables = made if tables is None else tables
            w_out = ret_out_norm[j].reshape(-1, 1) * ret_w_out[j]
        else:
            a = _sparse_attention_mixer(x2d, b, s, attn_norm[i], dsa_w_in[j], dsa_q_norm[j],
                                        dsa_k_norm[j], dsa_kidx_norm[j], tables)
            w_out = dsa_w_out[j]
        x2d = _mixer_out_and_mlp(a, x2d, w_out, mlp_norm[i], mlp_w_up, mlp_w_down, i)
    return x2d.reshape(b, s, d)
```

```python
import functools
import math

import jax
import jax.numpy as jnp
from jax import lax
from jax.experimental import pallas as pl
from jax.experimental.pallas import tpu as pltpu

F32 = jnp.float32
BF16 = jnp.bfloat16

EPS = 1e-6
ROPE_THETA = 10000.0
RET_HEADS = 4
RET_CHUNK = 256
ATT_HEADS = 8
ATT_KV_HEADS = 2
IDX_HEADS = 8
IDX_DH = 64
TOPK_MAX = 256
LANES = 128
SUBLANES = 8

ROW_TILE = 512
DSA_ROW_TILE = 1024
RET_ROW_BLOCK = 1024
Q_TILE = 256
KEY_BLOCK = 128
SWEEP_UNROLL = 8
VMEM_LIMIT = 56 * 1024 * 1024
BISECT_STEPS = 16


def _params(*sem):
    return pltpu.CompilerParams(dimension_semantics=sem, vmem_limit_bytes=VMEM_LIMIT)


def _resident(shape):
    nd = len(shape)
    return pl.BlockSpec(shape, lambda *_: (0,) * nd, pipeline_mode=pl.Buffered(1))


def _rms(x, gain):
    return x * lax.rsqrt(jnp.mean(x * x, axis=-1, keepdims=True) + EPS) * gain


def _dot(a, b):
    return jnp.dot(a, b, preferred_element_type=F32)


def _dot_nt(a, b):
    return lax.dot_general(a, b, (((1,), (1,)), ((), ())), preferred_element_type=F32)


def _dot_tn(a, b):
    return lax.dot_general(a, b, (((0,), (0,)), ((), ())), preferred_element_type=F32)


def _rope_slab_in_kernel(c, pos_ref, inv_ref, c64t_ref, s64t_ref, c32t_ref, s32t_ref, ct_ref,
                         st_ref):
    lanes = slice(c * LANES, (c + 1) * LANES)
    ang_t = inv_ref[...] * pos_ref[:, lanes].astype(F32)
    cos_t = jnp.cos(ang_t)
    sin_t = jnp.sin(ang_t)
    ct_ref[c] = cos_t
    st_ref[c] = sin_t
    for step, c_ref, s_ref in ((2, c64t_ref, s64t_ref), (4, c32t_ref, s32t_ref)):
        rows = pl.ds(0, LANES // step, stride=step)
        c_ref[:, lanes] = ct_ref[c, rows, :]
        s_ref[:, lanes] = st_ref[c, rows, :]
    return cos_t.T, sin_t.T


def _ret_inproj_kernel(x_ref, gain_ref, w_ref, pos_ref, inv_ref, q_ref, k_ref, v_ref, g_ref,
                       c64t_ref, s64t_ref, c32t_ref, s32t_ref, ct_ref, st_ref,
                       *, heads, dk, dv, kscale):
    hn = _rms(x_ref[...], gain_ref[...]).astype(BF16)
    half = dk // 2
    width = heads * dv
    for off, out_ref, swish in ((2 * heads * dk + width, g_ref, True),
                                (2 * heads * dk, v_ref, False)):
        for c in range(width // dv):
            z = _dot(hn, w_ref[:, off + c * dv:off + (c + 1) * dv].astype(BF16))
            if swish:
                z = z * (0.5 * jnp.tanh(0.5 * z) + 0.5)
            out_ref[:, c * dv:(c + 1) * dv] = z.astype(BF16)
    trig = [_rope_slab_in_kernel(c, pos_ref, inv_ref, c64t_ref, s64t_ref, c32t_ref, s32t_ref,
                                 ct_ref, st_ref) for c in range(ct_ref.shape[0])]
    cos = jnp.concatenate([t[0] for t in trig], axis=0)
    sin = jnp.concatenate([t[1] for t in trig], axis=0)
    for h in range(heads):
        for off, out_ref, scale in ((0, q_ref, None), (heads * dk, k_ref, kscale)):
            z = _dot(hn, w_ref[:, off + h * dk:off + (h + 1) * dk].astype(BF16))
            x1, x2 = z[:, :half], z[:, half:]
            o1 = x1 * cos - x2 * sin
            o2 = x2 * cos + x1 * sin
            if scale is not None:
                o1, o2 = o1 * scale, o2 * scale
            out_ref[:, h * dk:h * dk + half] = o1.astype(BF16)
            out_ref[:, h * dk + half:(h + 1) * dk] = o2.astype(BF16)


def _ret_core_kernel(q_ref, k_ref, v_ref, g_ref, dm_ref, qd_ref, kd_ref, cd_ref,
                     o_ref, state_ref, *, heads, chunk):
    @pl.when(pl.program_id(1) == 0)
    def _():
        state_ref[...] = jnp.zeros_like(state_ref)

    dk = q_ref.shape[2] // heads
    dv = v_ref.shape[2] // heads
    hs = range(heads)
    states = [state_ref[h] for h in hs]
    for c in range(q_ref.shape[1] // chunk):
        rows = slice(c * chunk, (c + 1) * chunk)
        q = [q_ref[0, rows, h * dk:(h + 1) * dk] for h in hs]
        k = [k_ref[0, rows, h * dk:(h + 1) * dk] for h in hs]
        v = [v_ref[0, rows, h * dv:(h + 1) * dv] for h in hs]
        scores = [_dot_nt(q[h], k[h]) for h in hs]
        kv = [_dot_tn((k[h].astype(F32) * kd_ref[h]).astype(BF16), v[h]) for h in hs]
        cross = [_dot(q[h], states[h].astype(BF16)) for h in hs]
        intra = [_dot((scores[h] * dm_ref[h]).astype(BF16), v[h]) for h in hs]
        for h in hs:
            o = intra[h] + cross[h] * qd_ref[h]
            y = o * lax.rsqrt(jnp.mean(o * o, axis=-1, keepdims=True) + EPS)
            gate = g_ref[0, rows, h * dv:(h + 1) * dv].astype(F32)
            o_ref[0, rows, h * dv:(h + 1) * dv] = (y * gate).astype(BF16)
            states[h] = states[h] * cd_ref[h] + kv[h]
    for h in hs:
        state_ref[h] = states[h]


def _retention_mixer(x2d, b, s, attn_gain, w_in, positions):
    t, d = x2d.shape
    heads = RET_HEADS
    dk = d // heads
    dv = 2 * dk
    chunk = RET_CHUNK
    tm = min(ROW_TILE, t)
    w = w_in
    row = lambda width: pl.BlockSpec((tm, width), lambda i: (i, 0))
    col = lambda n: pl.BlockSpec((n, tm), lambda i: (0, i))
    tab = lambda n: jax.ShapeDtypeStruct((n, t), F32)
    inv = ROPE_THETA ** (-jnp.arange(LANES, dtype=F32) / LANES)
    q, k, v, g, *tables = pl.pallas_call(
        functools.partial(_ret_inproj_kernel, heads=heads, dk=dk, dv=dv, kscale=dk ** -0.5),
        grid=(t // tm,),
        in_specs=[row(d), _resident((1, d)), _resident(w.shape),
                  pl.BlockSpec((1, tm), lambda i: (0, i)), _resident((LANES, 1))],
        out_specs=[row(heads * dk), row(heads * dk), row(heads * dv), row(heads * dv),
                   col(64), col(64), col(32), col(32)],
        out_shape=[jax.ShapeDtypeStruct((t, heads * dk), BF16),
                   jax.ShapeDtypeStruct((t, heads * dk), BF16),
                   jax.ShapeDtypeStruct((t, heads * dv), BF16),
                   jax.ShapeDtypeStruct((t, heads * dv), BF16),
                   tab(64), tab(64), tab(32), tab(32)],
        scratch_shapes=[pltpu.VMEM((tm // LANES, LANES, LANES), F32)] * 2,
        compiler_params=_params("parallel"),
        name="ret_inproj",
    )(x2d, attn_gain.reshape(1, d), w, positions.reshape(1, t), inv.reshape(LANES, 1))

    log_gamma = jnp.log1p(-(2.0 ** (-5.0 - jnp.arange(heads, dtype=F32))))
    i = jnp.arange(chunk, dtype=F32)
    diff = i[:, None] - i[None, :]
    dm = jnp.where(diff >= 0, jnp.exp(log_gamma[:, None, None] * jnp.maximum(diff, 0.0)), 0.0)
    qd = jnp.exp(log_gamma[:, None] * (i + 1.0))
    kd = jnp.exp(log_gamma[:, None] * (chunk - 1.0 - i))
    cd = jnp.exp(log_gamma * chunk)
    qd = jnp.broadcast_to(qd[:, :, None], (heads, chunk, dv))
    kd = jnp.broadcast_to(kd[:, :, None], (heads, chunk, dk))
    cd = jnp.broadcast_to(cd[:, None, None], (heads, 1, dv))

    rb = min(RET_ROW_BLOCK, s)
    seq = lambda width: pl.BlockSpec((1, rb, width), lambda bi, r: (bi, r, 0))
    o = pl.pallas_call(
        functools.partial(_ret_core_kernel, heads=heads, chunk=chunk),
        grid=(b, s // rb),
        in_specs=[seq(heads * dk), seq(heads * dk), seq(heads * dv), seq(heads * dv),
                  _resident(dm.shape), _resident(qd.shape), _resident(kd.shape),
                  _resident(cd.shape)],
        out_specs=seq(heads * dv),
        out_shape=jax.ShapeDtypeStruct((b, s, heads * dv), BF16),
        scratch_shapes=[pltpu.VMEM((heads, dk, dv), F32)],
        compiler_params=_params("parallel", "arbitrary"),
        name="ret_core",
    )(q.reshape(b, s, -1), k.reshape(b, s, -1), v.reshape(b, s, -1), g.reshape(b, s, -1),
      dm, qd, kd, cd)
    return o.reshape(t, heads * dv), tuple(tables)


def _tail_kernel(a_ref, x_ref, wo_ref, gain_ref, wup_ref, wdn_ref, o_ref, *, ff_chunk):
    x1 = x_ref[...] + _dot(a_ref[...], wo_ref[...])
    hn = _rms(x1, gain_ref[...]).astype(BF16)
    acc = x1
    for c in range(wup_ref.shape[2] // ff_chunk):
        cols = slice(c * ff_chunk, (c + 1) * ff_chunk)
        u = jnp.maximum(_dot(hn, wup_ref[0, :, cols].astype(BF16)), 0.0)
        acc = acc + _dot((u * u).astype(BF16), wdn_ref[0, cols, :].astype(BF16))
    o_ref[...] = acc


def _mixer_out_and_mlp(a2d, x2d, w_out, mlp_gain, w_up_all, w_down_all, layer):
    t, d = x2d.shape
    tm = min(ROW_TILE, t)
    wo = w_out.astype(BF16)
    row = lambda width: pl.BlockSpec((tm, width), lambda i: (i, 0))
    of_layer = lambda w: pl.BlockSpec((1,) + w.shape[1:], lambda i: (layer, 0, 0),
                                      pipeline_mode=pl.Buffered(1))
    return pl.pallas_call(
        functools.partial(_tail_kernel, ff_chunk=2048),
        grid=(t // tm,),
        in_specs=[row(a2d.shape[1]), row(d), _resident(wo.shape), _resident((1, d)),
                  of_layer(w_up_all), of_layer(w_down_all)],
        out_specs=row(d),
        out_shape=jax.ShapeDtypeStruct((t, d), F32),
        compiler_params=_params("parallel"),
        name="outproj_mlp",
    )(a2d, x2d, wo, mlp_gain.reshape(1, d), w_up_all, w_down_all)


def _norm_rope_rows(z, gain, cos, sin):
    if gain is not None:
        z = z * lax.rsqrt(jnp.mean(z * z, axis=0, keepdims=True) + EPS) * gain
    half = z.shape[0] // 2
    x1, x2 = z[:half], z[half:]
    return x1 * cos - x2 * sin, x2 * cos + x1 * sin


def _dsa_inproj_kernel(x_ref, gain_ref, wt_ref, qg_ref, kg_ref, kig_ref, c64_ref, s64_ref,
                       c32_ref, s32_ref, qt_ref, k_ref, vt_ref, qit_ref, ki_ref, wit_ref, buf_ref,
                       *, heads, kv_heads, dh, idx_heads, di, wscale):
    hn = _rms(x_ref[...], gain_ref[...]).astype(BF16)
    tm = hn.shape[0]
    proj = _dot_nt(wt_ref[...], hn)
    c64, s64 = c64_ref[...], s64_ref[...]
    c32, s32 = c32_ref[...], s32_ref[...]
    qg = jnp.broadcast_to(qg_ref[...], (dh, tm))
    kg = jnp.broadcast_to(kg_ref[...], (dh, tm))
    for h in range(heads):
        o1, o2 = _norm_rope_rows(proj[h * dh:(h + 1) * dh], qg, c64, s64)
        qt_ref[0, h * dh:h * dh + dh // 2, :] = o1.astype(BF16)
        qt_ref[0, h * dh + dh // 2:(h + 1) * dh, :] = o2.astype(BF16)
    off = heads * dh
    for h in range(kv_heads):
        o1, o2 = _norm_rope_rows(proj[off + h * dh:off + (h + 1) * dh], kg, c64, s64)
        k_ref[:, h * dh:(h + 1) * dh] = jnp.concatenate([o1, o2], axis=0).T.astype(BF16)
    off += kv_heads * dh
    vrows = vt_ref.shape[1] // kv_heads
    for g in range(kv_heads):
        vt_ref[0, g * vrows:g * vrows + dh, :] = proj[off + g * dh:off + (g + 1) * dh].astype(BF16)
        vt_ref[0, g * vrows + dh:(g + 1) * vrows, :] = jnp.ones((vrows - dh, tm), BF16)
    off += kv_heads * dh
    for h in range(idx_heads):
        o1, o2 = _norm_rope_rows(proj[off + h * di:off + (h + 1) * di], None, c32, s32)
        qit_ref[0, h * di:h * di + di // 2, :] = o1.astype(BF16)
        qit_ref[0, h * di + di // 2:(h + 1) * di, :] = o2.astype(BF16)
    off += idx_heads * di
    kig = jnp.broadcast_to(kig_ref[...], (di, tm))
    o1, o2 = _norm_rope_rows(proj[off:off + di], kig, c32, s32)
    ki_t = jnp.concatenate([o1, o2, proj[off + di:off + LANES]], axis=0)
    ki_ref[...] = ki_t.T[:, :di].astype(BF16)
    off += LANES
    wit_ref[0] = proj[off:off + idx_heads] * wscale
    buf_ref[...] = jnp.zeros_like(buf_ref)


def _dsa_attn_kernel(*refs, tile, topk, heads, kv_heads, dh, idx_heads, index_steps):
    qt_ref, k_ref, vt_ref, qit_ref, ki_ref, wit_ref = refs[:6]
    o_ref, sc_ref, lg_ref, e_ref, cap_ref = refs[-5:]
    tq = qt_ref.shape[2]
    ke = k_ref.shape[1]
    di = qit_ref.shape[1] // idx_heads
    kb = KEY_BLOCK
    slabs = kb // SUBLANES
    j = tile
    nblk = ke // kb
    neg = float(jnp.finfo(F32).min)
    big = float(jnp.finfo(F32).max)
    diag = (lax.broadcasted_iota(jnp.int32, (tq, tq), 0)
            <= lax.broadcasted_iota(jnp.int32, (tq, tq), 1))

    def rows_of(i):
        return pl.ds(pl.multiple_of(i * kb, kb), kb)

    def full(val):
        return jnp.full((SUBLANES, tq), val, F32)

    def spread(row):
        return jnp.broadcast_to(row, (SUBLANES, tq))

    group = heads // kv_heads

    def raw_logits(head, i):
        g = head // group
        lg_ref[head, rows_of(i), :] = _dot(k_ref[0, rows_of(i), g * dh:(g + 1) * dh],
                                           qt_ref[0, head * dh:(head + 1) * dh, :])

    def sweep(fn, init, head=None, unrolled=False):
        unroll = nblk if unrolled else max(u for u in range(1, SWEEP_UNROLL + 1) if nblk % u == 0)
        acc = lax.fori_loop(0, nblk, lambda i, a: fn(i, sc_ref[rows_of(i), :], a), init,
                            unroll=unroll)
        if head is not None:
            g = head // group
            lg = _dot(k_ref[0, :, g * dh:(g + 1) * dh], qt_ref[0, head * dh:(head + 1) * dh, :])
            lg_ref[head] = lg
            acc = acc + jnp.minimum(jnp.maximum(lg[ke - SUBLANES:, :], 0.0), 0.0)
        return acc

    if ke <= topk:
        sc_ref[...] = jnp.where(diag, 0.0, neg)
        for h in range(heads):
            lax.fori_loop(0, nblk, lambda i, c, h=h: (raw_logits(h, i), c)[1], 0, unroll=True)
    else:
        wit = wit_ref[0]
        ki = ki_ref[0]
        score = jnp.zeros((ke, tq), F32)
        for h in range(idx_heads):
            rel = jnp.maximum(_dot(ki, qit_ref[0, h * di:(h + 1) * di, :]), 0.0)
            score = score + rel * wit[h:h + 1, :]
        last = score[ke - tq:, :]
        row_max = jnp.max(jnp.where(diag, last, neg), axis=0, keepdims=True)
        row_min = jnp.min(jnp.where(diag, last, big), axis=0, keepdims=True)
        if ke > tq:
            row_max = jnp.maximum(row_max, jnp.max(score[:ke - tq, :], axis=0, keepdims=True))
            row_min = jnp.minimum(row_min, jnp.min(score[:ke - tq, :], axis=0, keepdims=True))
            sc_ref[:ke - tq, :] = score[:ke - tq, :]
        sc_ref[ke - tq:, :] = jnp.where(diag, last, neg)
        q_pos = j * tq + lax.broadcasted_iota(jnp.int32, (1, tq), 1)
        kvec = jnp.minimum(q_pos + 1, topk).astype(F32)

        def count_ge(th, head=None, unrolled=False):
            thb = spread(th)[None]
            acc = sweep(lambda i, blk, a: a + jnp.sum(
                jnp.where(blk.reshape(slabs, SUBLANES, tq) >= thb, 1.0, 0.0), axis=0),
                jnp.zeros((SUBLANES, tq), F32), head, unrolled)
            return jnp.sum(acc, axis=0, keepdims=True)

        def max_below(hi):
            hib = spread(hi)[None]

            def step(i, blk, a):
                blk = blk.reshape(slabs, SUBLANES, tq)
                return jnp.maximum(a, jnp.max(jnp.where(blk < hib, blk, neg), axis=0))

            return jnp.max(sweep(step, full(neg)), axis=0, keepdims=True)

        steps = max(BISECT_STEPS, heads)
        carried = {(h * steps) // heads: h for h in range(heads)}
        lo, hi, c_hi = row_min, row_max + jnp.abs(row_max) * 1e-3 + 1e-3, jnp.zeros((1, tq), F32)
        for step in range(steps):
            mid = lo * 0.5 + hi * 0.5
            c = count_ge(mid, carried.get(step), unrolled=True)
            ge = c >= kvec
            lo, hi, c_hi = jnp.where(ge, mid, lo), jnp.where(ge, hi, mid), jnp.where(ge, c_hi, c)

        def count_and_below(v):
            vb = spread(v)[None]

            def step(i, blk, a):
                blk = blk.reshape(slabs, SUBLANES, tq)
                ge = blk >= vb
                return (a[0] + jnp.sum(jnp.where(ge, 1.0, 0.0), axis=0),
                        jnp.maximum(a[1], jnp.max(jnp.where(ge, neg, blk), axis=0)))

            cnt, below = sweep(step, (jnp.zeros((SUBLANES, tq), F32), full(neg)))
            return jnp.sum(cnt, axis=0, keepdims=True), jnp.max(below, axis=0, keepdims=True)

        def pending(cv):
            return jnp.max(jnp.where(cv < kvec, 1.0, 0.0))

        def refine(state):
            c_hi, v, cv, below, _ = state
            short = cv < kvec
            c_hi = jnp.where(short, cv, c_hi)
            v = jnp.where(short, below, v)
            cv, below = count_and_below(v)
            return c_hi, v, cv, below, pending(cv)

        v = max_below(hi)
        cv, below = count_and_below(v)
        c_hi, v, cv, _, _ = lax.while_loop(lambda st: st[4] > 0.0, refine,
                                           (c_hi, v, cv, below, pending(cv)))
        need = kvec - c_hi
        surplus = jnp.max(jnp.where(cv > kvec, 1.0, 0.0)) > 0.0
        vb = spread(v)

        @pl.when(jnp.logical_not(surplus))
        def _():
            def to_bias(i, carry):
                blk = sc_ref[rows_of(i), :].reshape(slabs, SUBLANES, tq)
                sc_ref[rows_of(i), :] = jnp.where(blk >= vb[None], 0.0, neg).reshape(kb, tq)
                return carry
            lax.fori_loop(0, nblk, to_bias, 0)

        @pl.when(surplus)
        def _():
            def key_idx(i):
                return (lax.broadcasted_iota(jnp.int32, (slabs, SUBLANES, tq), 0) * SUBLANES
                        + lax.broadcasted_iota(jnp.int32, (slabs, SUBLANES, tq), 1) + i * kb)

            def idx_step(_, carry):
                lo_i, hi_i = carry
                mid_i = (lo_i + hi_i) >> 1
                midb = jnp.broadcast_to(mid_i, (SUBLANES, tq))
                acc = sweep(lambda i, blk, a: a + jnp.sum(jnp.where(
                    (blk.reshape(slabs, SUBLANES, tq) == vb[None]) & (key_idx(i) <= midb[None]),
                    1.0, 0.0), axis=0), jnp.zeros((SUBLANES, tq), F32))
                ok = jnp.sum(acc, axis=0, keepdims=True) >= need
                return jnp.where(ok, lo_i, mid_i), jnp.where(ok, mid_i, hi_i)

            _, istar = lax.fori_loop(0, index_steps, idx_step,
                                     (jnp.full((1, tq), -1, jnp.int32),
                                      jnp.full((1, tq), nblk * kb - 1, jnp.int32)))
            istarb = jnp.broadcast_to(istar, (SUBLANES, tq))

            def to_bias(i, carry):
                blk = sc_ref[rows_of(i), :].reshape(slabs, SUBLANES, tq)
                keep = (blk > vb[None]) | ((blk == vb[None]) & (key_idx(i) <= istarb[None]))
                sc_ref[rows_of(i), :] = jnp.where(keep, 0.0, neg).reshape(kb, tq)
                return carry
            lax.fori_loop(0, nblk, to_bias, 0)

    cap_val = float(jnp.finfo(BF16).max)

    packed = 2 * SUBLANES

    def max_pass(i, maxima):
        rows = rows_of(i)
        selected = sc_ref[rows, :] == 0.0
        cap_ref[rows, :] = jnp.where(selected, cap_val, 0.0).astype(BF16)
        bias = jnp.where(selected, 0.0, -cap_val).astype(BF16)
        new = []
        for h in range(heads):
            lg = lg_ref[h, rows, :].astype(BF16) + bias
            new.append(jnp.maximum(maxima[h], jnp.max(lg.reshape(kb // packed, packed, tq), axis=0)))
        return tuple(new)

    maxima = lax.fori_loop(0, nblk, max_pass, (jnp.full((packed, tq), -cap_val, BF16),) * heads,
                           unroll=min(nblk, 2))
    maxima = [m16.astype(F32) for m16 in maxima]

    vrows = vt_ref.shape[1] // kv_heads
    for h in range(heads):
        g = h // group
        m = jnp.max(maxima[h], axis=0, keepdims=True)
        for c in range(ke // tq):
            rows = slice(c * tq, (c + 1) * tq)
            e = jnp.exp2((lg_ref[h, rows, :] - m).astype(BF16))
            e_ref[h, rows, :] = jnp.minimum(e, cap_ref[rows, :])
        out_t = _dot(vt_ref[0, g * vrows:(g + 1) * vrows, :], e_ref[h])
        o_ref[0, :, h * dh:(h + 1) * dh] = (out_t[:dh] / out_t[dh:dh + 1]).T.astype(BF16)


def _sparse_attention_mixer(x2d, b, s, attn_gain, w_in, q_gain, k_gain, kidx_gain, tabs):
    t, d = x2d.shape
    heads, kv_heads, idx_heads, di = ATT_HEADS, ATT_KV_HEADS, IDX_HEADS, IDX_DH
    dh = d // heads
    c64, s64, c32, s32 = tabs
    tm = min(DSA_ROW_TILE, s)
    per_b = s // tm
    nq, nkv, nqi = heads * dh, kv_heads * dh, idx_heads * di
    nvt = kv_heads * (dh + 2 * SUBLANES)
    n_main = nq + 2 * nkv + nqi + di
    zeros = lambda n: jnp.zeros((d, n), w_in.dtype)
    wt = jnp.concatenate([w_in[:, :n_main], zeros(LANES - di), w_in[:, n_main:n_main + idx_heads],
                          zeros(2 * SUBLANES - idx_heads)], axis=1).T.astype(BF16)
    qg = (q_gain * (dh ** -0.5 * math.log2(math.e))).reshape(dh, 1)
    row = lambda width: pl.BlockSpec((tm, width), lambda i: (i, 0))
    col = lambda n: pl.BlockSpec((n, tm), lambda i: (0, i))
    seq_t = lambda n: pl.BlockSpec((1, n, tm), lambda i: (i // per_b, 0, i % per_b))
    shp = lambda width, dt: jax.ShapeDtypeStruct((t, width), dt)
    shp_t = lambda n, dt: jax.ShapeDtypeStruct((b, n, s), dt)
    qt, k, vt, qit, ki, wit, out = pl.pallas_call(
        functools.partial(_dsa_inproj_kernel, heads=heads, kv_heads=kv_heads, dh=dh,
                          idx_heads=idx_heads, di=di, wscale=idx_heads ** -0.5 * di ** -0.5),
        grid=(t // tm,),
        in_specs=[row(d), _resident((1, d)), _resident(wt.shape), _resident((dh, 1)),
                  _resident((dh, 1)), _resident((di, 1)), col(dh // 2), col(dh // 2),
                  col(di // 2), col(di // 2)],
        out_specs=[seq_t(nq), row(nkv), seq_t(nvt), seq_t(nqi), row(di), seq_t(idx_heads),
                   row(nq)],
        out_shape=[shp_t(nq, BF16), shp(nkv, BF16), shp_t(nvt, BF16), shp_t(nqi, BF16),
                   shp(di, BF16), shp_t(idx_heads, F32), shp(nq, BF16)],
        compiler_params=_params("parallel"),
        name="dsa_inproj",
    )(x2d, attn_gain.reshape(1, d), wt, qg, k_gain.reshape(dh, 1), kidx_gain.reshape(di, 1),
      c64, s64, c32, s32)

    topk = min(TOPK_MAX, s // 4)
    tq = min(Q_TILE, s)
    k, ki = k.reshape(b, s, nkv), ki.reshape(b, s, di)
    out = out.reshape(b, s, nq)
    for tile in range(s // tq):
        ke = (tile + 1) * tq
        operands = [qt, k, vt, qit, ki, wit]
        in_specs = [pl.BlockSpec((1, nq, tq), lambda bi, tile=tile: (bi, 0, tile)),
                    pl.BlockSpec((1, ke, nkv), lambda bi: (bi, 0, 0)),
                    pl.BlockSpec((1, nvt, ke), lambda bi: (bi, 0, 0)),
                    pl.BlockSpec((1, nqi, tq), lambda bi, tile=tile: (bi, 0, tile)),
                    pl.BlockSpec((1, ke, di), lambda bi: (bi, 0, 0)),
                    pl.BlockSpec((1, idx_heads, tq), lambda bi, tile=tile: (bi, 0, tile))]
        operands.append(out)
        in_specs.append(pl.BlockSpec(memory_space=pl.ANY))
        out = pl.pallas_call(
            functools.partial(_dsa_attn_kernel, tile=tile, topk=topk, heads=heads,
                              kv_heads=kv_heads, dh=dh, idx_heads=idx_heads,
                              index_steps=(ke - 1).bit_length() + 1),
            grid=(b,),
            in_specs=in_specs,
            out_specs=pl.BlockSpec((1, tq, nq), lambda bi, tile=tile: (bi, tile, 0)),
            out_shape=jax.ShapeDtypeStruct((b, s, nq), BF16),
            scratch_shapes=[pltpu.VMEM((ke, tq), F32), pltpu.VMEM((heads, ke, tq), F32),
                            pltpu.VMEM((heads, ke, tq), BF16), pltpu.VMEM((ke, tq), BF16)],
            input_output_aliases={len(operands) - 1: 0},
            compiler_params=_params("parallel"),
            name=f"dsa_attn_{tile}",
        )(*operands)
    return out.reshape(t, nq)


def kernel(x, positions, attn_norm, ret_w_in, ret_out_norm, ret_w_out, dsa_w_in, dsa_q_norm,
           dsa_k_norm, dsa_kidx_norm, dsa_w_out, mlp_norm, mlp_w_up, mlp_w_down):
    b, s, d = x.shape
    depth = attn_norm.shape[0]
    x2d = x.reshape(b * s, d)
    tables = None
    for i in range(depth):
        j = i // 2
        if i % 2 == 0:
            a, made = _retention_mixer(x2d, b, s, attn_norm[i], ret_w_in[j], positions)
            tables = made if tables is None else tables
            w_out = ret_out_norm[j].reshape(-1, 1) * ret_w_out[j]
        else:
            a = _sparse_attention_mixer(x2d, b, s, attn_norm[i], dsa_w_in[j], dsa_q_norm[j],
                                        dsa_k_norm[j], dsa_kidx_norm[j], tables)
            w_out = dsa_w_out[j]
        x2d = _mixer_out_and_mlp(a, x2d, w_out, mlp_norm[i], mlp_w_up, mlp_w_down, i)
    return x2d.reshape(b, s, d)
```
